```python
import math
import jax, jax.numpy as jnp
from jax import lax
import numpy as np

D_MODEL = 4096
BATCH = 1
SEQ = 8192
DEPTH = 1
DEC_BATCH = 32
DEC_SEQ = 8
PAST_LEN = 8192
PAGE_SIZE = 128

D_MIX = D_MODEL
ATTN_WIDTH = D_MIX // 2
POOL_WIDTH = D_MIX - ATTN_WIDTH
HEAD_DIM = 128
N_HEADS = ATTN_WIDTH // HEAD_DIM
N_KV_HEADS = max(1, N_HEADS // 4)
KV_GROUP = N_HEADS // N_KV_HEADS
KV_WIDTH = N_KV_HEADS * HEAD_DIM
ATTN_SCALE = HEAD_DIM ** -0.5
MOBA_BLOCK = 256
MOBA_TOPK = 3
Q_CHUNK = 16
POOL_WINDOWS = (2, 4, 8, 16)
N_POOL_GROUPS = len(POOL_WINDOWS)
POOL_GROUP_WIDTH = POOL_WIDTH // N_POOL_GROUPS
POOL_HIST = max(POOL_WINDOWS) - 1
NUM_BUCKETS = 32
MAX_EXACT = NUM_BUCKETS // 2
MAX_DISTANCE = 4096
RMS_EPS = 1e-6
NEG_INF = -1e30
IN_WIDTH = 2 * ATTN_WIDTH + 2 * KV_WIDTH + 2 * POOL_WIDTH
SPLITS = (ATTN_WIDTH, ATTN_WIDTH + KV_WIDTH, ATTN_WIDTH + 2 * KV_WIDTH,
          2 * ATTN_WIDTH + 2 * KV_WIDTH, 2 * ATTN_WIDTH + 2 * KV_WIDTH + POOL_WIDTH)

kernel_name = "moba_pool_hybrid_step"


def rms_norm(x, gain):
    xf = x.astype(jnp.float32)
    y = xf * lax.rsqrt(jnp.mean(xf * xf, axis=-1, keepdims=True) + RMS_EPS)
    return (y * gain.astype(jnp.float32)).astype(x.dtype)


def rel_bucket(dist):
    n = jnp.maximum(dist, 0)
    nf = jnp.maximum(n, 1).astype(jnp.float32)
    large = MAX_EXACT + (jnp.log(nf / MAX_EXACT) / math.log(MAX_DISTANCE / MAX_EXACT)
                         * (NUM_BUCKETS - MAX_EXACT)).astype(jnp.int32)
    large = jnp.minimum(large, NUM_BUCKETS - 1)
    return jnp.where(n < MAX_EXACT, n, large)


def ceil_blocks(length):
    return -(-length // MOBA_BLOCK) * MOBA_BLOCK


def pad_rows(x, length, axis):
    pad = [(0, 0)] * x.ndim
    pad[axis] = (0, length - x.shape[axis])
    return jnp.pad(x, pad)


def project(h, gain, w):
    n, t, _ = h.shape
    z = rms_norm(h, gain) @ w
    q, k, v, g_attn, u, g_pool = jnp.split(z, list(SPLITS), axis=-1)
    return (q.reshape(n, t, N_HEADS, HEAD_DIM),
            k.reshape(n, t, N_KV_HEADS, HEAD_DIM),
            v.reshape(n, t, N_KV_HEADS, HEAD_DIM),
            g_attn, u, g_pool)


def moba_chunk(qc, pc, bc, vc, k_full, v_full, rel_bias):
    kp = bc[..., None] * MOBA_BLOCK + jnp.arange(MOBA_BLOCK, dtype=jnp.int32)
    kvh = (jnp.arange(N_HEADS) // KV_GROUP)[None, :, None, None]
    kg = k_full[kp, kvh]
    vg = v_full[kp, kvh]
    logits = jnp.einsum('chd,chjkd->chjk', qc, kg).astype(jnp.float32) * ATTN_SCALE
    dist = pc[:, None, None, None] - kp
    hidx = jnp.arange(N_HEADS)[None, :, None, None]
    logits = logits + rel_bias[rel_bucket(dist), hidx].astype(jnp.float32)
    valid = vc[:, None, :, None] & (dist >= 0)
    logits = jnp.where(valid, logits, NEG_INF)
    c, h, j, b = logits.shape
    p = jax.nn.softmax(logits.reshape(c, h, j * b), axis=-1).reshape(c, h, j, b)
    return jnp.einsum('chjk,chjkd->chd', p.astype(vg.dtype), vg)


def moba_sequence(q, k_full, v_full, pos, rel_bias, q_chunk):
    n_q = q.shape[0]
    n_blk = k_full.shape[0] // MOBA_BLOCK
    kvh = jnp.arange(N_HEADS) // KV_GROUP
    k_mean = k_full.astype(jnp.float32).reshape(
        n_blk, MOBA_BLOCK, N_KV_HEADS, HEAD_DIM).mean(axis=1)[:, kvh]
    n_past = pos // MOBA_BLOCK
    gate = jnp.einsum('qhd,nhd->qhn', q.astype(jnp.float32), k_mean)
    gate = jnp.where(jnp.arange(n_blk)[None, None, :] < n_past[:, None, None], gate, NEG_INF)
    k_sel = min(MOBA_TOPK, n_blk)
    _, sel = lax.top_k(gate, k_sel)
    own = jnp.broadcast_to(n_past[:, None, None], (n_q, N_HEADS, 1)).astype(sel.dtype)
    blocks = jnp.concatenate([sel, own], axis=-1)
    n_j = k_sel + 1
    blk_valid = jnp.concatenate(
        [jnp.arange(k_sel)[None, :] < n_past[:, None], jnp.ones((n_q, 1), dtype=bool)], axis=-1)
    n_c = n_q // q_chunk
    xs = (q.reshape(n_c, q_chunk, N_HEADS, HEAD_DIM), pos.reshape(n_c, q_chunk),
          blocks.reshape(n_c, q_chunk, N_HEADS, n_j), blk_valid.reshape(n_c, q_chunk, n_j))
    out = lax.map(lambda a: moba_chunk(a[0], a[1], a[2], a[3], k_full, v_full, rel_bias), xs)
    return out.reshape(n_q, N_HEADS, HEAD_DIM)


def pool_mixer(u_ext, pos, w_pool, pool_scale):
    n, total, _ = u_ext.shape
    t = total - POOL_HIST
    cs = jnp.concatenate([jnp.zeros((n, 1, POOL_WIDTH), jnp.float32),
                          jnp.cumsum(u_ext, axis=1)], axis=1)
    u_new = u_ext[:, POOL_HIST:]
    parts = []
    for g, w in enumerate(POOL_WINDOWS):
        sl = slice(g * POOL_GROUP_WIDTH, (g + 1) * POOL_GROUP_WIDTH)
        win_sum = (cs[:, POOL_HIST + 1:POOL_HIST + 1 + t, sl]
                   - cs[:, POOL_HIST + 1 - w:POOL_HIST + 1 - w + t, sl])
        count = jnp.minimum(w, pos + 1).astype(jnp.float32)
        parts.append(win_sum / count[None, :, None] - u_new[..., sl])
    z = jnp.stack(parts, axis=2)
    y = jnp.einsum('ntgc,gcd->ntgd', z, w_pool.astype(jnp.float32))
    return y.reshape(n, t, POOL_WIDTH) * pool_scale.astype(jnp.float32)


def merge(h, attn, g_attn, pooled, g_pool, w):
    n, t, _ = h.shape
    y_a = attn.reshape(n, t, ATTN_WIDTH) * jax.nn.silu(g_attn)
    y_p = pooled.astype(h.dtype) * jax.nn.silu(g_pool)
    return h + jnp.concatenate([y_a, y_p], axis=-1) @ w


def setup_inputs(seed: int = 0) -> dict:
    key = jax.random.key(seed)
    ks = jax.random.split(key, 13)
    n_pages = PAST_LEN // PAGE_SIZE
    n_used = DEC_BATCH * n_pages
    n_phys = n_used + (n_used + 3) // 4
    nrm = lambda k, shape: jax.random.normal(k, shape, jnp.float32)
    return {
        "x_prompt": nrm(ks[0], (BATCH, SEQ, D_MODEL)),
        "x_sample": nrm(ks[1], (DEC_BATCH, DEC_SEQ, D_MODEL)),
        "cache_k": nrm(ks[2], (DEPTH, n_phys, PAGE_SIZE, N_KV_HEADS, HEAD_DIM)),
        "cache_v": nrm(ks[3], (DEPTH, n_phys, PAGE_SIZE, N_KV_HEADS, HEAD_DIM)),
        "state_pool": nrm(ks[4], (DEPTH, DEC_BATCH, POOL_HIST, POOL_WIDTH)),
        "page_table": jax.random.permutation(ks[5], n_phys)[:n_used].reshape(
            DEC_BATCH, n_pages).astype(jnp.int32),
        "norm_in": 1.0 + 0.1 * nrm(ks[6], (DEPTH, D_MODEL)),
        "w_in": nrm(ks[7], (DEPTH, D_MODEL, IN_WIDTH)) * D_MODEL ** -0.5,
        "w_pool": nrm(ks[8], (DEPTH, N_POOL_GROUPS, POOL_GROUP_WIDTH, POOL_GROUP_WIDTH))
                  * POOL_GROUP_WIDTH ** -0.5,
        "pool_scale": 1.0 + 0.1 * nrm(ks[9], (DEPTH, POOL_WIDTH)),
        "w_out": nrm(ks[10], (DEPTH, D_MIX, D_MODEL)) * D_MIX ** -0.5,
        "rel_bias": 0.5 * nrm(ks[11], (NUM_BUCKETS, N_HEADS)),
        "norm_out": 1.0 + 0.1 * nrm(ks[12], (D_MODEL,)),
    }


def reference(x_prompt, x_sample, cache_k, cache_v, state_pool, page_table,
              norm_in, w_in, w_pool, pool_scale, w_out, rel_bias, norm_out):
    pos_p = jnp.arange(SEQ, dtype=jnp.int32)
    pos_s = PAST_LEN + jnp.arange(DEC_SEQ, dtype=jnp.int32)
    len_p = ceil_blocks(SEQ)
    len_s = ceil_blocks(PAST_LEN + DEC_SEQ)
    h_p, h_s = x_prompt, x_sample
    kp_l, vp_l, pp_l, ks_l, vs_l, ps_l = [], [], [], [], [], []
    for layer in range(DEPTH):
        q, k, v, g_a, u, g_p = project(h_p, norm_in[layer], w_in[layer])
        k_pad = pad_rows(k, len_p, 1)
        v_pad = pad_rows(v, len_p, 1)
        attn = lax.map(lambda a: moba_sequence(a[0], a[1], a[2], pos_p, rel_bias, Q_CHUNK),
                       (q, k_pad, v_pad))
        u_ext = jnp.concatenate([jnp.zeros((BATCH, POOL_HIST, POOL_WIDTH), jnp.float32),
                                 u.astype(jnp.float32)], axis=1)
        pooled = pool_mixer(u_ext, pos_p, w_pool[layer], pool_scale[layer])
        h_p = merge(h_p, attn, g_a, pooled, g_p, w_out[layer])
        kp_l.append(k)
        vp_l.append(v)
        pp_l.append(u_ext[:, -POOL_HIST:].astype(x_prompt.dtype))
        q, k, v, g_a, u, g_p = project(h_s, norm_in[layer], w_in[layer])
        ck = cache_k[layer]
        cv = cache_v[layer]

        def attend_one(a, ck=ck, cv=cv):
            qb, kb, vb, pt = a
            k_full = jnp.concatenate([ck[pt].reshape(-1, N_KV_HEADS, HEAD_DIM), kb], axis=0)
            v_full = jnp.concatenate([cv[pt].reshape(-1, N_KV_HEADS, HEAD_DIM), vb], axis=0)
            return moba_sequence(qb, pad_rows(k_full, len_s, 0), pad_rows(v_full, len_s, 0),
                                 pos_s, rel_bias, DEC_SEQ)

        attn = lax.map(attend_one, (q, k, v, page_table))
        u_ext = jnp.concatenate([state_pool[layer].astype(jnp.float32),
                                 u.astype(jnp.float32)], axis=1)
        pooled = pool_mixer(u_ext, pos_s, w_pool[layer], pool_scale[layer])
        h_s = merge(h_s, attn, g_a, pooled, g_p, w_out[layer])
        ks_l.append(k)
        vs_l.append(v)
        ps_l.append(u_ext[:, -POOL_HIST:].astype(state_pool.dtype))
    y_prompt = rms_norm(h_p, norm_out)
    y_sample = rms_norm(h_s, norm_out)
    k_prompt = jnp.stack(kp_l)
    v_prompt = jnp.stack(vp_l)
    pool_prompt = jnp.stack(pp_l)
    k_sample = jnp.stack(ks_l)
    v_sample = jnp.stack(vs_l)
    pool_sample = jnp.stack(ps_l)
    return (y_prompt, y_sample, k_prompt, v_prompt, pool_prompt, k_sample, v_sample, pool_sample)
```

```python
import functools
import math

import numpy as np
import jax
import jax.numpy as jnp
from jax import lax
from jax.experimental import pallas as pl
from jax.experimental.pallas import tpu as pltpu

F32 = jnp.float32
BF16 = jnp.bfloat16

HEAD_DIM = 128
N_HEADS = 16
N_KV_HEADS = 4
KV_GROUP = N_HEADS // N_KV_HEADS
ATTN_WIDTH = N_HEADS * HEAD_DIM
KV_WIDTH = N_KV_HEADS * HEAD_DIM
GROUP_WIDTH = KV_GROUP * HEAD_DIM
ATTN_SCALE = HEAD_DIM ** -0.5
MOBA_BLOCK = 256
MOBA_TOPK = 3
PAGE_SIZE = 128
PAGES_PER_BLOCK = MOBA_BLOCK // PAGE_SIZE
POOL_WINDOWS = (2, 4, 8, 16)
N_POOL_GROUPS = len(POOL_WINDOWS)
POOL_HIST = max(POOL_WINDOWS) - 1
POOL_HALO = POOL_HIST + 1
NUM_BUCKETS = 32
MAX_EXACT = NUM_BUCKETS // 2
MAX_DISTANCE = 4096
RMS_EPS = 1e-6
NEG_INF = -1e30
GATE_MASKED = -3.0e38
GATE_VALID_MIN = -1.0e38
LANES = 128
TABLE_DELTAS = 3

VMEM_LIMIT = 56 * 1024 * 1024


def _rel_bucket(dist):
    n = jnp.maximum(dist, 0)
    nf = jnp.maximum(n, 1).astype(F32)
    large = MAX_EXACT + (jnp.log(nf / MAX_EXACT) / math.log(MAX_DISTANCE / MAX_EXACT)
                         * (NUM_BUCKETS - MAX_EXACT)).astype(jnp.int32)
    large = jnp.minimum(large, NUM_BUCKETS - 1)
    return jnp.where(n < MAX_EXACT, n, large)


def _rel_bucket_np(dist):
    n = np.maximum(dist, 0)
    nf = np.maximum(n, 1).astype(np.float32)
    large = MAX_EXACT + (np.log(nf / np.float32(MAX_EXACT)) / np.float32(math.log(MAX_DISTANCE / MAX_EXACT))
                         * np.float32(NUM_BUCKETS - MAX_EXACT)).astype(np.int32)
    large = np.minimum(large, NUM_BUCKETS - 1)
    return np.where(n < MAX_EXACT, n, large)


def _silu(x):
    return x * jax.nn.sigmoid(x)


def _dot_nt(a, b, precision=None):
    return lax.dot_general(a, b, (((1,), (1,)), ((), ())), precision=precision,
                           preferred_element_type=F32)


def _select_top_blocks(gate, lane, n_past):
    gate = jnp.where(lane < n_past, gate, GATE_MASKED)
    lane_f = lane.astype(F32)
    sel = jnp.zeros(gate.shape, jnp.bool_)
    for _ in range(MOBA_TOPK):
        mx = jnp.max(gate, axis=-1, keepdims=True)
        first = jnp.min(jnp.where(gate == mx, lane_f, float(LANES)), axis=-1, keepdims=True)
        pick = (lane_f == first) & (mx > GATE_VALID_MIN)
        sel = sel | pick
        gate = jnp.where(pick, GATE_MASKED, gate)
    return sel


def _inproj_kernel(x_ref, gain_ref, w_ref, o_ref, xn_ref, *, row_chunk):
    @pl.when(pl.program_id(1) == 0)
    def _():
        for r in range(0, x_ref.shape[0], row_chunk):
            x = x_ref[r:r + row_chunk, :]
            ms = jnp.mean(x * x, axis=-1, keepdims=True)
            xn_ref[r:r + row_chunk, :] = (x * lax.rsqrt(ms + RMS_EPS) * gain_ref[...]).astype(BF16)

    o_ref[...] = jnp.dot(xn_ref[...], w_ref[...], preferred_element_type=F32)


def _in_projection(x, gain, w_bf16, tm, tn):
    m, d = x.shape
    n = w_bf16.shape[1]
    return pl.pallas_call(
        functools.partial(_inproj_kernel, row_chunk=min(tm, 128)),
        grid=(m // tm, n // tn),
        in_specs=[pl.BlockSpec((tm, d), lambda i, j: (i, 0)),
                  pl.BlockSpec((1, d), lambda i, j: (0, 0)),
                  pl.BlockSpec((d, tn), lambda i, j: (0, j))],
        out_specs=pl.BlockSpec((tm, tn), lambda i, j: (i, j)),
        out_shape=jax.ShapeDtypeStruct((m, n), F32),
        scratch_shapes=[pltpu.VMEM((tm, d), BF16)],
        compiler_params=pltpu.CompilerParams(
            dimension_semantics=("parallel", "arbitrary"), vmem_limit_bytes=VMEM_LIMIT),
    )(x, gain.reshape(1, d), w_bf16)


def _kmean_kernel(k_ref, o_ref, *, blocks_per_step):
    k = k_ref[...].reshape(blocks_per_step, MOBA_BLOCK, KV_WIDTH)
    o_ref[...] = jnp.sum(k, axis=1) * (1.0 / MOBA_BLOCK)


def _block_means(z, n_blk, k_col0):
    bps = 8
    assert n_blk % bps == 0 and k_col0 % KV_WIDTH == 0
    return pl.pallas_call(
        functools.partial(_kmean_kernel, blocks_per_step=bps),
        grid=(n_blk // bps,),
        in_specs=[pl.BlockSpec((bps * MOBA_BLOCK, KV_WIDTH), lambda i: (i, k_col0 // KV_WIDTH))],
        out_specs=pl.BlockSpec((bps, KV_WIDTH), lambda i: (i, 0)),
        out_shape=jax.ShapeDtypeStruct((n_blk, KV_WIDTH), F32),
        compiler_params=pltpu.CompilerParams(vmem_limit_bytes=VMEM_LIMIT),
    )(z)


def _prompt_attn_kernel(ti_ref, tn_ref, thr_ref,
                        q_ref, k_ref, v_ref, km_ref, ga_ref, tab_ref, blo_ref, bhi_ref,
                        o_ref, qaug_ref, m_ref, l_ref, acc_ref, dmat_ref):
    g = pl.program_id(0)
    s = pl.program_id(1)
    i = ti_ref[s]
    n = tn_ref[s]
    delta = i - n
    blk = MOBA_BLOCK

    @pl.when(n == 0)
    def _init():
        m_ref[...] = jnp.full(m_ref.shape, NEG_INF, F32)
        l_ref[...] = jnp.zeros(l_ref.shape, F32)
        acc_ref[...] = jnp.zeros(acc_ref.shape, F32)
        dmat_ref[...] = (lax.broadcasted_iota(jnp.int32, (blk, blk), 0)
                         - lax.broadcasted_iota(jnp.int32, (blk, blk), 1))
        lane = lax.broadcasted_iota(jnp.int32, (blk, LANES), 1)
        for hh in range(KV_GROUP):
            qh = q_ref[:, hh * HEAD_DIM:(hh + 1) * HEAD_DIM]
            gate = _dot_nt(qh, km_ref[...], precision=lax.Precision.HIGHEST)
            sel = _select_top_blocks(gate, lane, i)
            selbias = jnp.where(sel | (lane == i), 0.0, NEG_INF)
            rows = slice(hh * blk, (hh + 1) * blk)
            qaug_ref[rows, 0:HEAD_DIM] = (qh * ATTN_SCALE).astype(BF16)
            qaug_ref[rows, HEAD_DIM:HEAD_DIM + LANES] = selbias.astype(BF16)

    lane = lax.broadcasted_iota(jnp.int32, (blk, LANES), 1)
    onehot = jnp.where(lane == n, 1.0, 0.0).astype(BF16)
    kaug = jnp.concatenate([k_ref[...].astype(BF16), onehot], axis=1)
    vb = v_ref[...].astype(BF16)

    def sweep(bias_of_head):
        for hh in range(KV_GROUP):
            rows = slice(hh * blk, (hh + 1) * blk)
            sc = _dot_nt(qaug_ref[rows, :], kaug) + bias_of_head(hh)
            m_old = m_ref[rows, :]
            m_new = jnp.maximum(m_old, jnp.max(sc, axis=-1, keepdims=True))
            alpha = jnp.exp(m_old - m_new)
            p = jnp.exp(sc - m_new)
            l_ref[rows, :] = alpha * l_ref[rows, :] + jnp.sum(p, axis=-1, keepdims=True)
            acc_ref[rows, :] = alpha * acc_ref[rows, :] + jnp.dot(
                p.astype(BF16), vb, preferred_element_type=F32)
            m_ref[rows, :] = m_new

    @pl.when(delta < TABLE_DELTAS)
    def _near():
        sweep(lambda hh: tab_ref[0, hh])

    @pl.when(delta >= TABLE_DELTAS)
    def _far():
        thr = thr_ref[delta]

        def bias(hh):
            head = g * KV_GROUP + hh
            return jnp.where(dmat_ref[...] >= thr, bhi_ref[delta * N_HEADS + head],
                             blo_ref[delta * N_HEADS + head])

        sweep(bias)

    @pl.when(n == i)
    def _finish():
        for hh in range(KV_GROUP):
            rows = slice(hh * blk, (hh + 1) * blk)
            cols = slice(hh * HEAD_DIM, (hh + 1) * HEAD_DIM)
            attn = acc_ref[rows, :] / l_ref[rows, :]
            o_ref[:, cols] = (attn * _silu(ga_ref[:, cols])).astype(BF16)


def _prompt_bias_tables(rel_bias, n_blk):
    blk = MOBA_BLOCK
    r = jnp.arange(blk, dtype=jnp.int32)
    dist = (jnp.arange(TABLE_DELTAS, dtype=jnp.int32)[:, None, None] * blk
            + r[None, :, None] - r[None, None, :])
    tab = jnp.moveaxis(rel_bias.astype(F32)[_rel_bucket(dist)], -1, 1)
    tab = jnp.where((dist >= 0)[:, None], tab, NEG_INF)

    nd = max(n_blk, TABLE_DELTAS + 1)
    off = np.arange(-(blk - 1), blk, dtype=np.int32)
    dist_np = np.arange(nd, dtype=np.int32)[:, None] * blk + off[None, :]
    buckets_np = _rel_bucket_np(dist_np)
    span = buckets_np[TABLE_DELTAS:, -1] - buckets_np[TABLE_DELTAS:, 0]
    assert np.all(np.diff(buckets_np[TABLE_DELTAS:], axis=1) >= 0) and np.all(span <= 1), \
        "more than one bias bucket boundary inside a far block pair"
    buckets = _rel_bucket(jnp.asarray(dist_np))
    b_lo, b_hi = buckets[:, 0], buckets[:, -1]
    first_hi = jnp.argmax(buckets == b_hi[:, None], axis=1).astype(jnp.int32)
    thr = first_hi - (blk - 1)
    blo = rel_bias.astype(F32)[b_lo].reshape(-1)
    bhi = rel_bias.astype(F32)[b_hi].reshape(-1)
    return tab, thr, blo, bhi


def _prompt_attention(z, kmean_pad, rel_bias, seq, k_col0, v_col0, ga_col0):
    blk = MOBA_BLOCK
    n_blk = seq // blk
    tab, thr, blo, bhi = _prompt_bias_tables(rel_bias, n_blk)
    ti = np.concatenate([np.full(i + 1, i, np.int32) for i in range(n_blk)])
    tn = np.concatenate([np.arange(i + 1, dtype=np.int32) for i in range(n_blk)])
    kc, vc, gc = k_col0 // HEAD_DIM, v_col0 // HEAD_DIM, ga_col0 // GROUP_WIDTH
    smem = pl.BlockSpec(memory_space=pltpu.SMEM)
    grid_spec = pltpu.PrefetchScalarGridSpec(
        num_scalar_prefetch=3,
        grid=(N_KV_HEADS, len(ti)),
        in_specs=[
            pl.BlockSpec((blk, GROUP_WIDTH), lambda g, s, ti, tn, th: (ti[s], g)),
            pl.BlockSpec((blk, HEAD_DIM), lambda g, s, ti, tn, th: (tn[s], kc + g)),
            pl.BlockSpec((blk, HEAD_DIM), lambda g, s, ti, tn, th: (tn[s], vc + g)),
            pl.BlockSpec((LANES, HEAD_DIM), lambda g, s, ti, tn, th: (0, g)),
            pl.BlockSpec((blk, GROUP_WIDTH), lambda g, s, ti, tn, th: (ti[s], gc + g)),
            pl.BlockSpec((1, KV_GROUP, blk, blk),
                         lambda g, s, ti, tn, th: (jnp.minimum(ti[s] - tn[s], TABLE_DELTAS - 1), g, 0, 0)),
            smem, smem,
        ],
        out_specs=pl.BlockSpec((blk, GROUP_WIDTH), lambda g, s, ti, tn, th: (ti[s], g)),
        scratch_shapes=[
            pltpu.VMEM((KV_GROUP * blk, HEAD_DIM + LANES), BF16),
            pltpu.VMEM((KV_GROUP * blk, 1), F32),
            pltpu.VMEM((KV_GROUP * blk, 1), F32),
            pltpu.VMEM((KV_GROUP * blk, HEAD_DIM), F32),
            pltpu.VMEM((blk, blk), jnp.int32),
        ],
    )
    return pl.pallas_call(
        _prompt_attn_kernel,
        grid_spec=grid_spec,
        out_shape=jax.ShapeDtypeStruct((seq, ATTN_WIDTH), BF16),
        compiler_params=pltpu.CompilerParams(
            dimension_semantics=("arbitrary", "arbitrary"), vmem_limit_bytes=VMEM_LIMIT),
    )(jnp.asarray(ti), jnp.asarray(tn), thr, z, z, z, kmean_pad, z, tab, blo, bhi)


def _sample_attn_kernel(pt_ref, q_ref, knew_ref, vnew_ref, ga_ref, ka_ref, kb_ref, va_ref, vb_ref,
                        bias_ref, o_ref, qf_ref, qb_ref, gate_ref, mx_ref, l_ref, part_ref,
                        *, n_past, dec_seq):
    n = pl.program_id(1)
    rows = N_HEADS * dec_seq
    grp_rows = KV_GROUP * dec_seq
    lane = lax.broadcasted_iota(jnp.int32, (rows, LANES), 1)

    @pl.when(n == 0)
    def _init():
        gate_ref[...] = jnp.zeros(gate_ref.shape, F32)
        mx_ref[...] = jnp.zeros(mx_ref.shape, F32)
        l_ref[...] = jnp.zeros(l_ref.shape, F32)
        col_grp = lax.broadcasted_iota(jnp.int32, (dec_seq, KV_WIDTH), 1) // HEAD_DIM
        for h in range(N_HEADS):
            qh = q_ref[:, h * HEAD_DIM:(h + 1) * HEAD_DIM]
            wide = jnp.concatenate([qh] * N_KV_HEADS, axis=1)
            wide = jnp.where(col_grp == h // KV_GROUP, wide, 0.0)
            qf_ref[h * dec_seq:(h + 1) * dec_seq, :] = wide
            qb_ref[h * dec_seq:(h + 1) * dec_seq, :] = (wide * ATTN_SCALE).astype(BF16)

    def block_partial(kblk, vblk):
        sc = _dot_nt(qb_ref[...], kblk.astype(BF16)) + bias_ref[0]
        m_n = jnp.max(sc, axis=-1, keepdims=True)
        p = jnp.exp(sc - m_n)
        l_n = jnp.sum(p, axis=-1, keepdims=True)
        o_all = jnp.dot(p.astype(BF16), vblk.astype(BF16), preferred_element_type=F32)
        part_ref[n] = jnp.concatenate(
            [o_all[gi * grp_rows:(gi + 1) * grp_rows, gi * HEAD_DIM:(gi + 1) * HEAD_DIM]
             for gi in range(N_KV_HEADS)], axis=0)
        mx_ref[...] = jnp.where(lane == n, m_n, mx_ref[...])
        l_ref[...] = jnp.where(lane == n, l_n, l_ref[...])

    @pl.when(n < n_past)
    def _past():
        kblk = jnp.concatenate([ka_ref[0], kb_ref[0]], axis=0)
        vblk = jnp.concatenate([va_ref[0], vb_ref[0]], axis=0)
        kmean = jnp.sum(kblk, axis=0, keepdims=True) * (1.0 / MOBA_BLOCK)
        gate = jnp.sum(qf_ref[...] * kmean, axis=-1, keepdims=True)
        gate_ref[...] = jnp.where(lane == n, gate, gate_ref[...])
        block_partial(kblk, vblk)

    @pl.when(n == n_past)
    def _own():
        pad = jnp.zeros((MOBA_BLOCK - dec_seq, KV_WIDTH), F32)
        block_partial(jnp.concatenate([knew_ref[...], pad], axis=0),
                      jnp.concatenate([vnew_ref[...], pad], axis=0))
        sel = _select_top_blocks(gate_ref[...], lane, n_past)
        include = sel | (lane == n_past)
        mx = mx_ref[...]
        m_tot = jnp.max(jnp.where(include, mx, GATE_MASKED), axis=-1, keepdims=True)
        e = jnp.where(include, jnp.exp(jnp.minimum(mx - m_tot, 0.0)), 0.0)
        denom = jnp.sum(e * l_ref[...], axis=-1, keepdims=True)
        num = jnp.zeros((rows, HEAD_DIM), F32)
        for j in range(n_past + 1):
            num = num + e[:, j:j + 1] * part_ref[j]
        attn = num / denom
        for h in range(N_HEADS):
            cols = slice(h * HEAD_DIM, (h + 1) * HEAD_DIM)
            o_ref[0, :, cols] = (attn[h * dec_seq:(h + 1) * dec_seq, :] * _silu(ga_ref[:, cols])).astype(BF16)


def _sample_bias_table(rel_bias, past_len, dec_seq, n_past):
    kpos = jnp.arange((n_past + 1) * MOBA_BLOCK, dtype=jnp.int32).reshape(n_past + 1, 1, 1, MOBA_BLOCK)
    qpos = (past_len + jnp.arange(dec_seq, dtype=jnp.int32)).reshape(1, 1, dec_seq, 1)
    dist = qpos - kpos
    vals = jnp.moveaxis(rel_bias.astype(F32)[_rel_bucket(dist[:, 0])], -1, 1)
    vals = jnp.where(dist >= 0, vals, NEG_INF)
    return vals.reshape(n_past + 1, N_HEADS * dec_seq, MOBA_BLOCK)


def _sample_attention(q_s, k_s, v_s, ga_s, cache_k, cache_v, page_table, rel_bias, past_len, dec_seq):
    bsz, n_pages = page_table.shape
    n_past = past_len // MOBA_BLOCK
    assert past_len % MOBA_BLOCK == 0 and n_past + 1 <= LANES and n_pages == n_past * PAGES_PER_BLOCK
    n_phys = cache_k.shape[0]
    ck = cache_k.reshape(n_phys, PAGE_SIZE, KV_WIDTH)
    cv = cache_v.reshape(n_phys, PAGE_SIZE, KV_WIDTH)
    bias = _sample_bias_table(rel_bias, past_len, dec_seq, n_past)
    rows = N_HEADS * dec_seq

    def page_map(which):
        def index_map(b, n, pt):
            blk = jnp.minimum(n, n_past - 1)
            return (pt[b * n_pages + blk * PAGES_PER_BLOCK + which], 0, 0)
        return index_map

    page_spec = lambda which: pl.BlockSpec((1, PAGE_SIZE, KV_WIDTH), page_map(which))
    row_spec = lambda width: pl.BlockSpec((dec_seq, width), lambda b, n, pt: (b, 0))
    grid_spec = pltpu.PrefetchScalarGridSpec(
        num_scalar_prefetch=1,
        grid=(bsz, n_past + 1),
        in_specs=[row_spec(ATTN_WIDTH), row_spec(KV_WIDTH), row_spec(KV_WIDTH), row_spec(ATTN_WIDTH),
                  page_spec(0), page_spec(1), page_spec(0), page_spec(1),
                  pl.BlockSpec((1, rows, MOBA_BLOCK), lambda b, n, pt: (n, 0, 0))],
        out_specs=pl.BlockSpec((1, dec_seq, ATTN_WIDTH), lambda b, n, pt: (b, 0, 0)),
        scratch_shapes=[
            pltpu.VMEM((rows, KV_WIDTH), F32),
            pltpu.VMEM((rows, KV_WIDTH), BF16),
            pltpu.VMEM((rows, LANES), F32),
            pltpu.VMEM((rows, LANES), F32),
            pltpu.VMEM((rows, LANES), F32),
            pltpu.VMEM((n_past + 1, rows, HEAD_DIM), F32),
        ],
    )
    out = pl.pallas_call(
        functools.partial(_sample_attn_kernel, n_past=n_past, dec_seq=dec_seq),
        grid_spec=grid_spec,
        out_shape=jax.ShapeDtypeStruct((bsz, dec_seq, ATTN_WIDTH), BF16),
        compiler_params=pltpu.CompilerParams(
            dimension_semantics=("arbitrary", "arbitrary"), vmem_limit_bytes=VMEM_LIMIT),
    )(page_table.reshape(-1), q_s, k_s, v_s, ga_s, ck, ck, cv, cv, bias)
    return out.reshape(bsz * dec_seq, ATTN_WIDTH)


def _window_mix(ext_ref, window, rows, cols, count, u, gp, w, scale):
    win = u
    for j in range(1, window):
        win = win + ext_ref[POOL_HALO - j:POOL_HALO - j + rows, cols]
    zed = win / count - u
    y = jnp.dot(zed.astype(BF16), w, preferred_element_type=F32)
    return (y * scale) * _silu(gp)


def _pool_prompt_kernel(prev_ref, u_ref, gp_ref, w_ref, scale_ref, o_ref, ext_ref, *, groups_per_step):
    i = pl.program_id(0)
    rows = u_ref.shape[0]
    gw = w_ref.shape[1]
    ext_ref[0:POOL_HALO, :] = jnp.where(i == 0, 0.0, prev_ref[...])
    ext_ref[POOL_HALO:POOL_HALO + rows, :] = u_ref[...]
    pos = i * rows + lax.broadcasted_iota(jnp.int32, (rows, 1), 0)
    for part in range(N_POOL_GROUPS // groups_per_step):
        @pl.when(pl.program_id(1) == part)
        def _(part=part):
            for gl in range(groups_per_step):
                window = POOL_WINDOWS[part * groups_per_step + gl]
                cols = slice(gl * gw, (gl + 1) * gw)
                count = jnp.minimum(window, pos + 1).astype(F32)
                y = _window_mix(ext_ref, window, rows, cols, count, u_ref[:, cols], gp_ref[:, cols],
                                w_ref[gl], scale_ref[:, cols])
                o_ref[:, cols] = y.astype(BF16)


def _pool_prompt(z, w_pool_bf16, pool_scale, seq, u_col0, gp_col0, tm):
    n_groups, gw, _ = w_pool_bf16.shape
    gps = 2
    bw = gps * gw
    assert n_groups == N_POOL_GROUPS and u_col0 % bw == 0 and gp_col0 % bw == 0 and tm % POOL_HALO == 0
    uc, gc = u_col0 // bw, gp_col0 // bw
    halo_blocks = tm // POOL_HALO
    return pl.pallas_call(
        functools.partial(_pool_prompt_kernel, groups_per_step=gps),
        grid=(seq // tm, n_groups // gps),
        in_specs=[pl.BlockSpec((POOL_HALO, bw), lambda i, c: (jnp.maximum(i * halo_blocks - 1, 0), uc + c)),
                  pl.BlockSpec((tm, bw), lambda i, c: (i, uc + c)),
                  pl.BlockSpec((tm, bw), lambda i, c: (i, gc + c)),
                  pl.BlockSpec((gps, gw, gw), lambda i, c: (c, 0, 0)),
                  pl.BlockSpec((1, bw), lambda i, c: (0, c))],
        out_specs=pl.BlockSpec((tm, bw), lambda i, c: (i, c)),
        out_shape=jax.ShapeDtypeStruct((seq, n_groups * gw), BF16),
        scratch_shapes=[pltpu.VMEM((POOL_HALO + tm, bw), F32)],
        compiler_params=pltpu.CompilerParams(
            dimension_semantics=("parallel", "arbitrary"), vmem_limit_bytes=VMEM_LIMIT),
    )(z, z, z, w_pool_bf16, pool_scale.reshape(1, n_groups * gw))


def _pool_sample_kernel(ext_in_ref, gp_ref, w_ref, scale_ref, o_ref, ext_ref, *, bsz, dec_seq):
    seg = POOL_HALO + dec_seq
    rows = bsz * seg
    gw = w_ref.shape[1]
    ext_ref[0:POOL_HALO, :] = jnp.zeros((POOL_HALO, ext_ref.shape[1]), F32)
    ext_ref[POOL_HALO:POOL_HALO + rows, :] = ext_in_ref[...]
    for gi, window in enumerate(POOL_WINDOWS):
        cols = slice(gi * gw, (gi + 1) * gw)
        u_all = ext_in_ref[:, cols]
        win = u_all
        for j in range(1, window):
            win = win + ext_ref[POOL_HALO - j:POOL_HALO - j + rows, cols]
        zed = (win / float(window) - u_all).reshape(bsz, seg, gw)[:, POOL_HALO:, :].reshape(bsz * dec_seq, gw)
        y = jnp.dot(zed.astype(BF16), w_ref[gi], preferred_element_type=F32)
        o_ref[:, cols] = ((y * scale_ref[:, cols]) * _silu(gp_ref[:, cols])).astype(BF16)


def _pool_sample(u_ext, gp_s, w_pool_bf16, pool_scale, bsz, dec_seq):
    pw = u_ext.shape[1]
    rows = u_ext.shape[0]
    vm = pl.BlockSpec(memory_space=pltpu.VMEM)
    return pl.pallas_call(
        functools.partial(_pool_sample_kernel, bsz=bsz, dec_seq=dec_seq),
        in_specs=[vm, vm, vm, vm],
        out_specs=vm,
        out_shape=jax.ShapeDtypeStruct((bsz * dec_seq, pw), BF16),
        scratch_shapes=[pltpu.VMEM((POOL_HALO + rows, pw), F32)],
        compiler_params=pltpu.CompilerParams(vmem_limit_bytes=VMEM_LIMIT),
    )(u_ext, gp_s, w_pool_bf16, pool_scale.reshape(1, pw))


def _outproj_kernel(ya_ref, yp_ref, w_ref, x_ref, gain_ref, o_ref, *, k_attn):
    k = pl.program_id(1)

    @pl.when(k == 0)
    def _():
        o_ref[...] = x_ref[...]

    @pl.when(k < k_attn)
    def _():
        o_ref[...] += jnp.dot(ya_ref[...], w_ref[...], preferred_element_type=F32)

    @pl.when(k >= k_attn)
    def _():
        o_ref[...] += jnp.dot(yp_ref[...], w_ref[...], preferred_element_type=F32)

    @pl.when(k == pl.num_programs(1) - 1)
    def _():
        h = o_ref[...]
        ms = jnp.mean(h * h, axis=-1, keepdims=True)
        o_ref[...] = h * lax.rsqrt(ms + RMS_EPS) * gain_ref[...]


def _out_projection(ya, yp, w_bf16, x, gain, tm, tk):
    m, d = x.shape
    ka, kp = ya.shape[1] // tk, yp.shape[1] // tk
    return pl.pallas_call(
        functools.partial(_outproj_kernel, k_attn=ka),
        grid=(m // tm, ka + kp),
        in_specs=[pl.BlockSpec((tm, tk), lambda i, k: (i, jnp.minimum(k, ka - 1))),
                  pl.BlockSpec((tm, tk), lambda i, k: (i, jnp.maximum(k - ka, 0))),
                  pl.BlockSpec((tk, d), lambda i, k: (k, 0)),
                  pl.BlockSpec((tm, d), lambda i, k: (i, 0)),
                  pl.BlockSpec((1, d), lambda i, k: (0, 0))],
        out_specs=pl.BlockSpec((tm, d), lambda i, k: (i, 0)),
        out_shape=jax.ShapeDtypeStruct((m, d), F32),
        compiler_params=pltpu.CompilerParams(
            dimension_semantics=("parallel", "arbitrary"), vmem_limit_bytes=VMEM_LIMIT),
    )(ya, yp, w_bf16, x, gain.reshape(1, d))


def kernel(x_prompt, x_sample, cache_k, cache_v, state_pool, page_table, norm_in, w_in, w_pool,
           pool_scale, w_out, rel_bias, norm_out):
    depth = norm_in.shape[0]
    assert depth == 1, "single layer step"
    n_batch, seq, d_model = x_prompt.shape
    bsz, dec_seq, _ = x_sample.shape
    assert n_batch == 1 and seq % (8 * MOBA_BLOCK) == 0 and dec_seq == 8
    past_len = page_table.shape[1] * PAGE_SIZE
    assert past_len >= POOL_HIST
    pool_width = pool_scale.shape[1]
    k0 = ATTN_WIDTH
    v0 = k0 + KV_WIDTH
    ga0 = v0 + KV_WIDTH
    u0 = ga0 + ATTN_WIDTH
    gp0 = u0 + pool_width
    assert w_in.shape[2] == gp0 + pool_width and w_out.shape[1] == ATTN_WIDTH + pool_width

    w_in_b = w_in[0].astype(BF16)
    w_out_b = w_out[0].astype(BF16)
    w_pool_b = w_pool[0].astype(BF16)
    xp = x_prompt[0]
    xs = x_sample.reshape(bsz * dec_seq, d_model)

    z = _in_projection(xp, norm_in[0], w_in_b, tm=512, tn=512)
    n_blk = seq // MOBA_BLOCK
    kmean = _block_means(z, n_blk, k0)
    kmean_pad = jnp.pad(kmean, ((0, LANES - n_blk), (0, 0)))
    ya = _prompt_attention(z, kmean_pad, rel_bias, seq, k0, v0, ga0)
    yp = _pool_prompt(z, w_pool_b, pool_scale[0], seq, u0, gp0, tm=256)
    y_prompt = _out_projection(ya, yp, w_out_b, xp, norm_out, tm=512, tk=512)

    zs = _in_projection(xs, norm_in[0], w_in_b, tm=bsz * dec_seq, tn=512)
    q_s, k_s, v_s = zs[:, :k0], zs[:, k0:v0], zs[:, v0:ga0]
    ga_s, u_s, gp_s = zs[:, ga0:u0], zs[:, u0:gp0], zs[:, gp0:]
    ya_s = _sample_attention(q_s, k_s, v_s, ga_s, cache_k[0], cache_v[0], page_table, rel_bias,
                             past_len, dec_seq)
    u_s3 = u_s.reshape(bsz, dec_seq, pool_width)
    u_ext = jnp.concatenate([jnp.zeros((bsz, 1, pool_width), F32), state_pool[0].astype(F32), u_s3], axis=1)
    yp_s = _pool_sample(u_ext.reshape(bsz * (POOL_HALO + dec_seq), pool_width), gp_s, w_pool_b,
                        pool_scale[0], bsz, dec_seq)
    y_sample = _out_projection(ya_s, yp_s, w_out_b, xs, norm_out, tm=bsz * dec_seq, tk=512)

    k_prompt = z[:, k0:v0].reshape(1, 1, seq, N_KV_HEADS, HEAD_DIM)
    v_prompt = z[:, v0:ga0].reshape(1, 1, seq, N_KV_HEADS, HEAD_DIM)
    pool_prompt = z[seq - POOL_HIST:, u0:gp0].reshape(1, 1, POOL_HIST, pool_width)
    k_sample = k_s.reshape(1, bsz, dec_seq, N_KV_HEADS, HEAD_DIM)
    v_sample = v_s.reshape(1, bsz, dec_seq, N_KV_HEADS, HEAD_DIM)
    pool_sample = u_ext[:, 1 + dec_seq:, :].reshape(1, bsz, POOL_HIST, pool_width).astype(state_pool.dtype)
    return (y_prompt.reshape(1, seq, d_model), y_sample.reshape(bsz, dec_seq, d_model),
            k_prompt, v_prompt, pool_prompt, k_sample, v_sample, pool_sample)
```

```python
import functools
import math

import numpy as np
import jax
import jax.numpy as jnp
from jax import lax
from jax.experimental import pallas as pl
from jax.experimental.pallas import tpu as pltpu

F32 = jnp.float32
BF16 = jnp.bfloat16

HEAD_DIM = 128
N_HEADS = 16
N_KV_HEADS = 4
KV_GROUP = N_HEADS // N_KV_HEADS
ATTN_WIDTH = N_HEADS * HEAD_DIM
KV_WIDTH = N_KV_HEADS * HEAD_DIM
GROUP_WIDTH = KV_GROUP * HEAD_DIM
ATTN_SCALE = HEAD_DIM ** -0.5
MOBA_BLOCK = 256
MOBA_TOPK = 3
PAGE_SIZE = 128
PAGES_PER_BLOCK = MOBA_BLOCK // PAGE_SIZE
POOL_WINDOWS = (2, 4, 8, 16)
N_POOL_GROUPS = len(POOL_WINDOWS)
POOL_HIST = max(POOL_WINDOWS) - 1
POOL_HALO = POOL_HIST + 1
NUM_BUCKETS = 32
MAX_EXACT = NUM_BUCKETS // 2
MAX_DISTANCE = 4096
RMS_EPS = 1e-6
NEG_INF = -1e30
GATE_MASKED = -3.0e38
GATE_VALID_MIN = -1.0e38
LANES = 128
TABLE_DELTAS = 3

VMEM_LIMIT = 56 * 1024 * 1024


def _rel_bucket(dist):
    n = jnp.maximum(dist, 0)
    nf = jnp.maximum(n, 1).astype(F32)
    large = MAX_EXACT + (jnp.log(nf / MAX_EXACT) / math.log(MAX_DISTANCE / MAX_EXACT)
                         * (NUM_BUCKETS - MAX_EXACT)).astype(jnp.int32)
    large = jnp.minimum(large, NUM_BUCKETS - 1)
    return jnp.where(n < MAX_EXACT, n, large)


def _rel_bucket_np(dist):
    n = np.maximum(dist, 0)
    nf = np.maximum(n, 1).astype(np.float32)
    large = MAX_EXACT + (np.log(nf / np.float32(MAX_EXACT)) / np.float32(math.log(MAX_DISTANCE / MAX_EXACT))
                         * np.float32(NUM_BUCKETS - MAX_EXACT)).astype(np.int32)
    large = np.minimum(large, NUM_BUCKETS - 1)
    return np.where(n < MAX_EXACT, n, large)


def _bias_of_bucket(rel_bias, bucket, bucket_axis):
    onehot = (jnp.expand_dims(bucket, bucket_axis)
              == jnp.arange(NUM_BUCKETS, dtype=jnp.int32).reshape(
                  (NUM_BUCKETS,) + (1,) * (bucket.ndim - bucket_axis))).astype(F32)
    out = jnp.tensordot(rel_bias.astype(F32).T, onehot, axes=((1,), (bucket_axis,)),
                        precision=lax.Precision.HIGHEST)
    return jnp.moveaxis(out, 0, bucket_axis)


def _silu(x):
    return x * jax.nn.sigmoid(x)


def _dot_nt(a, b, precision=None):
    return lax.dot_general(a, b, (((1,), (1,)), ((), ())), precision=precision,
                           preferred_element_type=F32)


def _select_top_blocks(gate, lane, n_past):
    gate = jnp.where(lane < n_past, gate, GATE_MASKED)
    lane_f = lane.astype(F32)
    sel = jnp.zeros(gate.shape, jnp.bool_)
    for _ in range(MOBA_TOPK):
        mx = jnp.max(gate, axis=-1, keepdims=True)
        first = jnp.min(jnp.where(gate == mx, lane_f, float(LANES)), axis=-1, keepdims=True)
        pick = (lane_f == first) & (mx > GATE_VALID_MIN)
        sel = sel | pick
        gate = jnp.where(pick, GATE_MASKED, gate)
    return sel


def _inproj_kernel(x_ref, gain_ref, w_ref, o_ref, xn_ref, *, row_chunk):
    @pl.when(pl.program_id(1) == 0)
    def _():
        for r in range(0, x_ref.shape[0], row_chunk):
            x = x_ref[r:r + row_chunk, :]
            ms = jnp.mean(x * x, axis=-1, keepdims=True)
            xn_ref[r:r + row_chunk, :] = (x * lax.rsqrt(ms + RMS_EPS) * gain_ref[...]).astype(BF16)

    o_ref[...] = jnp.dot(xn_ref[...], w_ref[...], preferred_element_type=F32)


def _in_projection(x, gain, w_bf16, tm, tn):
    m, d = x.shape
    n = w_bf16.shape[1]
    return pl.pallas_call(
        functools.partial(_inproj_kernel, row_chunk=min(tm, 128)),
        grid=(m // tm, n // tn),
        in_specs=[pl.BlockSpec((tm, d), lambda i, j: (i, 0)),
                  pl.BlockSpec((1, d), lambda i, j: (0, 0)),
                  pl.BlockSpec((d, tn), lambda i, j: (0, j))],
        out_specs=pl.BlockSpec((tm, tn), lambda i, j: (i, j)),
        out_shape=jax.ShapeDtypeStruct((m, n), F32),
        scratch_shapes=[pltpu.VMEM((tm, d), BF16)],
        compiler_params=pltpu.CompilerParams(
            dimension_semantics=("parallel", "arbitrary"), vmem_limit_bytes=VMEM_LIMIT),
    )(x, gain.reshape(1, d), w_bf16)


def _kmean_kernel(k_ref, o_ref, *, blocks_per_step):
    k = k_ref[...].reshape(blocks_per_step, MOBA_BLOCK, KV_WIDTH)
    o_ref[...] = jnp.sum(k, axis=1) * (1.0 / MOBA_BLOCK)


def _block_means(z, n_blk, k_col0):
    bps = 8
    assert n_blk % bps == 0 and k_col0 % KV_WIDTH == 0
    return pl.pallas_call(
        functools.partial(_kmean_kernel, blocks_per_step=bps),
        grid=(n_blk // bps,),
        in_specs=[pl.BlockSpec((bps * MOBA_BLOCK, KV_WIDTH), lambda i: (i, k_col0 // KV_WIDTH))],
        out_specs=pl.BlockSpec((bps, KV_WIDTH), lambda i: (i, 0)),
        out_shape=jax.ShapeDtypeStruct((n_blk, KV_WIDTH), F32),
        compiler_params=pltpu.CompilerParams(vmem_limit_bytes=VMEM_LIMIT),
    )(z)


def _prompt_attn_kernel(ti_ref, tn_ref, thr_ref,
                        q_ref, k_ref, v_ref, km_ref, ga_ref, tab_ref, blo_ref, bhi_ref,
                        o_ref, qaug_ref, m_ref, l_ref, acc_ref, dmat_ref):
    g = pl.program_id(0)
    s = pl.program_id(1)
    i = ti_ref[s]
    n = tn_ref[s]
    delta = i - n
    blk = MOBA_BLOCK
    all_rows = KV_GROUP * blk

    @pl.when(n == 0)
    def _init():
        m_ref[...] = jnp.full(m_ref.shape, NEG_INF, F32)
        l_ref[...] = jnp.zeros(l_ref.shape, F32)
        acc_ref[...] = jnp.zeros(acc_ref.shape, F32)
        dmat_ref[...] = (lax.broadcasted_iota(jnp.int32, (blk, blk), 0)
                         - lax.broadcasted_iota(jnp.int32, (blk, blk), 1))
        lane = lax.broadcasted_iota(jnp.int32, (all_rows, LANES), 1)
        q_all = jnp.concatenate(
            [q_ref[:, hh * HEAD_DIM:(hh + 1) * HEAD_DIM] for hh in range(KV_GROUP)], axis=0)
        gate = _dot_nt(q_all, km_ref[...], precision=lax.Precision.HIGHEST)
        sel = _select_top_blocks(gate, lane, i)
        selbias = jnp.where(sel | (lane == i), 0.0, NEG_INF)
        qaug_ref[:, 0:HEAD_DIM] = (q_all * ATTN_SCALE).astype(BF16)
        qaug_ref[:, HEAD_DIM:HEAD_DIM + LANES] = selbias.astype(BF16)

    lane = lax.broadcasted_iota(jnp.int32, (blk, LANES), 1)
    onehot = jnp.where(lane == n, 1.0, 0.0).astype(BF16)
    kaug = jnp.concatenate([k_ref[...].astype(BF16), onehot], axis=1)
    vaug = jnp.concatenate([v_ref[...].astype(BF16), jnp.ones((blk, LANES), BF16)], axis=1)

    def sweep(bias):
        sc = _dot_nt(qaug_ref[...], kaug) + bias
        m_old = m_ref[...]
        m_new = jnp.maximum(m_old, jnp.max(sc, axis=-1, keepdims=True))
        alpha = jnp.exp(m_old - m_new)
        p = jnp.exp(sc - jnp.concatenate([m_new] * (blk // LANES), axis=1))
        pv = jnp.dot(p.astype(BF16), vaug, preferred_element_type=F32)
        l_ref[...] = alpha * l_ref[...] + pv[:, HEAD_DIM:]
        acc_ref[...] = alpha * acc_ref[...] + pv[:, :HEAD_DIM]
        m_ref[...] = m_new

    @pl.when(delta < TABLE_DELTAS)
    def _near():
        sweep(tab_ref[0].reshape(all_rows, blk))

    @pl.when(delta >= TABLE_DELTAS)
    def _far():
        high = dmat_ref[...] >= thr_ref[delta]
        base = delta * N_HEADS + g * KV_GROUP
        sweep(jnp.concatenate([jnp.where(high, bhi_ref[base + hh], blo_ref[base + hh])
                               for hh in range(KV_GROUP)], axis=0))

    @pl.when(n == i)
    def _finish():
        attn = acc_ref[...] / l_ref[...]
        for hh in range(KV_GROUP):
            cols = slice(hh * HEAD_DIM, (hh + 1) * HEAD_DIM)
            o_ref[:, cols] = (attn[hh * blk:(hh + 1) * blk, :] * _silu(ga_ref[:, cols])).astype(BF16)


def _prompt_bias_tables(rel_bias, n_blk):
    blk = MOBA_BLOCK
    r = jnp.arange(blk, dtype=jnp.int32)
    dist = (jnp.arange(TABLE_DELTAS, dtype=jnp.int32)[:, None, None] * blk
            + r[None, :, None] - r[None, None, :])
    tab = _bias_of_bucket(rel_bias, _rel_bucket(dist), 1)
    tab = jnp.where((dist >= 0)[:, None], tab, NEG_INF)

    nd = max(n_blk, TABLE_DELTAS + 1)
    off = np.arange(-(blk - 1), blk, dtype=np.int32)
    dist_np = np.arange(nd, dtype=np.int32)[:, None] * blk + off[None, :]
    buckets_np = _rel_bucket_np(dist_np)
    span = buckets_np[TABLE_DELTAS:, -1] - buckets_np[TABLE_DELTAS:, 0]
    assert np.all(np.diff(buckets_np[TABLE_DELTAS:], axis=1) >= 0) and np.all(span <= 1), \
        "more than one bias bucket boundary inside a far block pair"
    buckets = _rel_bucket(jnp.asarray(dist_np))
    b_lo, b_hi = buckets[:, 0], buckets[:, -1]
    first_hi = jnp.argmax(buckets == b_hi[:, None], axis=1).astype(jnp.int32)
    thr = first_hi - (blk - 1)
    blo = rel_bias.astype(F32)[b_lo].reshape(-1)
    bhi = rel_bias.astype(F32)[b_hi].reshape(-1)
    return tab, thr, blo, bhi


def _prompt_attention(z, kmean_pad, rel_bias, seq, k_col0, v_col0, ga_col0):
    blk = MOBA_BLOCK
    n_blk = seq // blk
    tab, thr, blo, bhi = _prompt_bias_tables(rel_bias, n_blk)
    ti = np.concatenate([np.full(i + 1, i, np.int32) for i in range(n_blk)])
    tn = np.concatenate([np.arange(i + 1, dtype=np.int32) for i in range(n_blk)])
    kc, vc, gc = k_col0 // HEAD_DIM, v_col0 // HEAD_DIM, ga_col0 // GROUP_WIDTH
    smem = pl.BlockSpec(memory_space=pltpu.SMEM)
    grid_spec = pltpu.PrefetchScalarGridSpec(
        num_scalar_prefetch=3,
        grid=(N_KV_HEADS, len(ti)),
        in_specs=[
            pl.BlockSpec((blk, GROUP_WIDTH), lambda g, s, ti, tn, th: (ti[s], g)),
            pl.BlockSpec((blk, HEAD_DIM), lambda g, s, ti, tn, th: (tn[s], kc + g)),
            pl.BlockSpec((blk, HEAD_DIM), lambda g, s, ti, tn, th: (tn[s], vc + g)),
            pl.BlockSpec((LANES, HEAD_DIM), lambda g, s, ti, tn, th: (0, g)),
            pl.BlockSpec((blk, GROUP_WIDTH), lambda g, s, ti, tn, th: (ti[s], gc + g)),
            pl.BlockSpec((1, KV_GROUP, blk, blk),
                         lambda g, s, ti, tn, th: (jnp.minimum(ti[s] - tn[s], TABLE_DELTAS - 1), g, 0, 0)),
            smem, smem,
        ],
        out_specs=pl.BlockSpec((blk, GROUP_WIDTH), lambda g, s, ti, tn, th: (ti[s], g)),
        scratch_shapes=[
            pltpu.VMEM((KV_GROUP * blk, HEAD_DIM + LANES), BF16),
            pltpu.VMEM((KV_GROUP * blk, LANES), F32),
            pltpu.VMEM((KV_GROUP * blk, LANES), F32),
            pltpu.VMEM((KV_GROUP * blk, HEAD_DIM), F32),
            pltpu.VMEM((blk, blk), jnp.int32),
        ],
    )
    return pl.pallas_call(
        _prompt_attn_kernel,
        grid_spec=grid_spec,
        out_shape=jax.ShapeDtypeStruct((seq, ATTN_WIDTH), BF16),
        compiler_params=pltpu.CompilerParams(
            dimension_semantics=("arbitrary", "arbitrary"), vmem_limit_bytes=VMEM_LIMIT),
    )(jnp.asarray(ti), jnp.asarray(tn), thr, z, z, z, kmean_pad, z, tab, blo, bhi)


def _sample_attn_kernel(pt_ref, q_ref, knew_ref, vnew_ref, ga_ref, *rest, n_past, dec_seq, blocks_per_step):
    n_pages = blocks_per_step * PAGES_PER_BLOCK
    k_pages, v_pages = rest[:n_pages], rest[n_pages:2 * n_pages]
    bias_ref, o_ref, qf_ref, qb_ref, gate_ref, mx_ref, l_ref, part_ref = rest[2 * n_pages:]
    t = pl.program_id(1)
    rows = N_HEADS * dec_seq
    grp_rows = KV_GROUP * dec_seq
    lane = lax.broadcasted_iota(jnp.int32, (rows, LANES), 1)

    @pl.when(t == 0)
    def _init():
        gate_ref[...] = jnp.zeros(gate_ref.shape, F32)
        mx_ref[...] = jnp.zeros(mx_ref.shape, F32)
        l_ref[...] = jnp.zeros(l_ref.shape, F32)
        col_grp = lax.broadcasted_iota(jnp.int32, (dec_seq, KV_WIDTH), 1) // HEAD_DIM
        for h in range(N_HEADS):
            qh = q_ref[:, h * HEAD_DIM:(h + 1) * HEAD_DIM]
            wide = jnp.concatenate([qh] * N_KV_HEADS, axis=1)
            wide = jnp.where(col_grp == h // KV_GROUP, wide, 0.0)
            qf_ref[h * dec_seq:(h + 1) * dec_seq, :] = wide
            qb_ref[h * dec_seq:(h + 1) * dec_seq, :] = (wide * ATTN_SCALE).astype(BF16)

    def block_partial(kblk, vblk, bias):
        sc = _dot_nt(qb_ref[...], kblk.astype(BF16)) + bias
        m_n = jnp.max(sc, axis=-1, keepdims=True)
        p = jnp.exp(sc - m_n)
        l_n = jnp.sum(p, axis=-1, keepdims=True)
        o_all = jnp.dot(p.astype(BF16), vblk.astype(BF16), preferred_element_type=F32)
        o_n = jnp.concatenate(
            [o_all[gi * grp_rows:(gi + 1) * grp_rows, gi * HEAD_DIM:(gi + 1) * HEAD_DIM]
             for gi in range(N_KV_HEADS)], axis=0)
        return m_n, l_n, o_n

    @pl.when(t < n_past // blocks_per_step)
    def _past():
        mx, ls, gt = mx_ref[...], l_ref[...], gate_ref[...]
        for jb in range(blocks_per_step):
            n = t * blocks_per_step + jb
            kblk = jnp.concatenate([k_pages[jb * PAGES_PER_BLOCK + pg][0] for pg in range(PAGES_PER_BLOCK)],
                                   axis=0)
            vblk = jnp.concatenate([v_pages[jb * PAGES_PER_BLOCK + pg][0] for pg in range(PAGES_PER_BLOCK)],
                                   axis=0)
            kmean = jnp.sum(kblk, axis=0, keepdims=True) * (1.0 / MOBA_BLOCK)
            gate = jnp.sum(qf_ref[...] * kmean, axis=-1, keepdims=True)
            m_n, l_n, o_n = block_partial(kblk, vblk, bias_ref[jb])
            part_ref[n] = o_n
            here = lane == n
            mx = jnp.where(here, m_n, mx)
            ls = jnp.where(here, l_n, ls)
            gt = jnp.where(here, gate, gt)
        mx_ref[...] = mx
        l_ref[...] = ls
        gate_ref[...] = gt

    @pl.when(t == n_past // blocks_per_step)
    def _own():
        pad = jnp.zeros((MOBA_BLOCK - dec_seq, KV_WIDTH), F32)
        m_n, l_n, o_n = block_partial(jnp.concatenate([knew_ref[...], pad], axis=0),
                                      jnp.concatenate([vnew_ref[...], pad], axis=0), bias_ref[0])
        own = lane == n_past
        mx = jnp.where(own, m_n, mx_ref[...])
        ls = jnp.where(own, l_n, l_ref[...])
        sel = _select_top_blocks(gate_ref[...], lane, n_past)
        include = sel | own
        m_tot = jnp.max(jnp.where(include, mx, GATE_MASKED), axis=-1, keepdims=True)
        e = jnp.where(include, jnp.exp(jnp.minimum(mx - m_tot, 0.0)), 0.0)
        denom = jnp.sum(e * ls, axis=-1, keepdims=True)
        num = e[:, n_past:n_past + 1] * o_n
        for j in range(n_past):
            num = num + e[:, j:j + 1] * part_ref[j]
        attn = num / denom
        for h in range(N_HEADS):
            cols = slice(h * HEAD_DIM, (h + 1) * HEAD_DIM)
            o_ref[0, :, cols] = (attn[h * dec_seq:(h + 1) * dec_seq, :] * _silu(ga_ref[:, cols])).astype(BF16)


def _sample_bias_table(rel_bias, past_len, dec_seq, n_blocks):
    kpos = jnp.arange(n_blocks * MOBA_BLOCK, dtype=jnp.int32).reshape(n_blocks, 1, MOBA_BLOCK)
    qpos = (past_len + jnp.arange(dec_seq, dtype=jnp.int32)).reshape(1, dec_seq, 1)
    dist = qpos - kpos
    vals = _bias_of_bucket(rel_bias, _rel_bucket(dist), 1)
    vals = jnp.where((dist >= 0)[:, None], vals, NEG_INF)
    return vals.reshape(n_blocks, N_HEADS * dec_seq, MOBA_BLOCK)


def _sample_attention(q_s, k_s, v_s, ga_s, cache_k, cache_v, page_table, rel_bias, past_len, dec_seq):
    bsz, n_pages = page_table.shape
    n_past = past_len // MOBA_BLOCK
    bps = 4
    assert past_len % MOBA_BLOCK == 0 and n_past + 1 <= LANES and n_pages == n_past * PAGES_PER_BLOCK
    assert n_past % bps == 0
    n_steps = n_past // bps + 1
    pages_per_step = bps * PAGES_PER_BLOCK
    bias = _sample_bias_table(rel_bias, past_len, dec_seq, n_steps * bps)
    rows = N_HEADS * dec_seq

    def page_spec(which):
        def index_map(b, t, pt):
            step = jnp.minimum(t, n_steps - 2)
            return (pt[b * n_pages + step * pages_per_step + which], 0, 0)
        return pl.BlockSpec((1, PAGE_SIZE, KV_WIDTH), index_map)

    row_spec = lambda width: pl.BlockSpec((dec_seq, width), lambda b, t, pt: (b, 0))
    grid_spec = pltpu.PrefetchScalarGridSpec(
        num_scalar_prefetch=1,
        grid=(bsz, n_steps),
        in_specs=([row_spec(ATTN_WIDTH), row_spec(KV_WIDTH), row_spec(KV_WIDTH), row_spec(ATTN_WIDTH)]
                  + [page_spec(j) for j in range(pages_per_step)]
                  + [page_spec(j) for j in range(pages_per_step)]
                  + [pl.BlockSpec((bps, rows, MOBA_BLOCK), lambda b, t, pt: (t, 0, 0))]),
        out_specs=pl.BlockSpec((1, dec_seq, ATTN_WIDTH), lambda b, t, pt: (b, 0, 0)),
        scratch_shapes=[
            pltpu.VMEM((rows, KV_WIDTH), F32),
            pltpu.VMEM((rows, KV_WIDTH), BF16),
            pltpu.VMEM((rows, LANES), F32),
            pltpu.VMEM((rows, LANES), F32),
            pltpu.VMEM((rows, LANES), F32),
            pltpu.VMEM((n_past, rows, HEAD_DIM), F32),
        ],
    )
    out = pl.pallas_call(
        functools.partial(_sample_attn_kernel, n_past=n_past, dec_seq=dec_seq, blocks_per_step=bps),
        grid_spec=grid_spec,
        out_shape=jax.ShapeDtypeStruct((bsz, dec_seq, ATTN_WIDTH), BF16),
        compiler_params=pltpu.CompilerParams(
            dimension_semantics=("arbitrary", "arbitrary"), vmem_limit_bytes=VMEM_LIMIT),
    )(page_table.reshape(-1), q_s, k_s, v_s, ga_s, *([cache_k] * pages_per_step),
      *([cache_v] * pages_per_step), bias)
    return out.reshape(bsz * dec_seq, ATTN_WIDTH)


def _window_mix(ext_ref, window, rows, cols, count, u, gp, w, scale):
    win = u
    for j in range(1, window):
        win = win + ext_ref[POOL_HALO - j:POOL_HALO - j + rows, cols]
    zed = win / count - u
    y = jnp.dot(zed.astype(BF16), w, preferred_element_type=F32)
    return (y * scale) * _silu(gp)


def _pool_prompt_kernel(prev_ref, u_ref, gp_ref, w_ref, scale_ref, o_ref, ext_ref, *, groups_per_step):
    i = pl.program_id(0)
    rows = u_ref.shape[0]
    gw = w_ref.shape[1]
    ext_ref[0:POOL_HALO, :] = jnp.where(i == 0, 0.0, prev_ref[...])
    ext_ref[POOL_HALO:POOL_HALO + rows, :] = u_ref[...]
    pos = i * rows + lax.broadcasted_iota(jnp.int32, (rows, 1), 0)
    for part in range(N_POOL_GROUPS // groups_per_step):
        @pl.when(pl.program_id(1) == part)
        def _(part=part):
            for gl in range(groups_per_step):
                window = POOL_WINDOWS[part * groups_per_step + gl]
                cols = slice(gl * gw, (gl + 1) * gw)
                count = jnp.minimum(window, pos + 1).astype(F32)
                y = _window_mix(ext_ref, window, rows, cols, count, u_ref[:, cols], gp_ref[:, cols],
                                w_ref[gl], scale_ref[:, cols])
                o_ref[:, cols] = y.astype(BF16)


def _pool_prompt(z, w_pool_bf16, pool_scale, seq, u_col0, gp_col0, tm):
    n_groups, gw, _ = w_pool_bf16.shape
    gps = 2
    bw = gps * gw
    assert n_groups == N_POOL_GROUPS and u_col0 % bw == 0 and gp_col0 % bw == 0 and tm % POOL_HALO == 0
    uc, gc = u_col0 // bw, gp_col0 // bw
    halo_blocks = tm // POOL_HALO
    return pl.pallas_call(
        functools.partial(_pool_prompt_kernel, groups_per_step=gps),
        grid=(seq // tm, n_groups // gps),
        in_specs=[pl.BlockSpec((POOL_HALO, bw), lambda i, c: (jnp.maximum(i * halo_blocks - 1, 0), uc + c)),
                  pl.BlockSpec((tm, bw), lambda i, c: (i, uc + c)),
                  pl.BlockSpec((tm, bw), lambda i, c: (i, gc + c)),
                  pl.BlockSpec((gps, gw, gw), lambda i, c: (c, 0, 0)),
                  pl.BlockSpec((1, bw), lambda i, c: (0, c))],
        out_specs=pl.BlockSpec((tm, bw), lambda i, c: (i, c)),
        out_shape=jax.ShapeDtypeStruct((seq, n_groups * gw), BF16),
        scratch_shapes=[pltpu.VMEM((POOL_HALO + tm, bw), F32)],
        compiler_params=pltpu.CompilerParams(
            dimension_semantics=("parallel", "arbitrary"), vmem_limit_bytes=VMEM_LIMIT),
    )(z, z, z, w_pool_bf16, pool_scale.reshape(1, n_groups * gw))


def _pool_sample_kernel(ext_in_ref, gp_ref, w_ref, scale_ref, o_ref, ext_ref, *, bsz, dec_seq):
    seg = POOL_HALO + dec_seq
    rows = bsz * seg
    gw = w_ref.shape[1]
    ext_ref[0:POOL_HALO, :] = jnp.zeros((POOL_HALO, ext_ref.shape[1]), F32)
    ext_ref[POOL_HALO:POOL_HALO + rows, :] = ext_in_ref[...]
    for gi, window in enumerate(POOL_WINDOWS):
        cols = slice(gi * gw, (gi + 1) * gw)
        u_all = ext_in_ref[:, cols]
        win = u_all
        for j in range(1, window):
            win = win + ext_ref[POOL_HALO - j:POOL_HALO - j + rows, cols]
        zed = (win / float(window) - u_all).reshape(bsz, seg, gw)[:, POOL_HALO:, :].reshape(bsz * dec_seq, gw)
        y = jnp.dot(zed.astype(BF16), w_ref[gi], preferred_element_type=F32)
        o_ref[:, cols] = ((y * scale_ref[:, cols]) * _silu(gp_ref[:, cols])).astype(BF16)


def _pool_sample(u_ext, gp_s, w_pool_bf16, pool_scale, bsz, dec_seq):
    pw = u_ext.shape[1]
    rows = u_ext.shape[0]
    vm = pl.BlockSpec(memory_space=pltpu.VMEM)
    return pl.pallas_call(
        functools.partial(_pool_sample_kernel, bsz=bsz, dec_seq=dec_seq),
        in_specs=[vm, vm, vm, vm],
        out_specs=vm,
        out_shape=jax.ShapeDtypeStruct((bsz * dec_seq, pw), BF16),
        scratch_shapes=[pltpu.VMEM((POOL_HALO + rows, pw), F32)],
        compiler_params=pltpu.CompilerParams(vmem_limit_bytes=VMEM_LIMIT),
    )(u_ext, gp_s, w_pool_bf16, pool_scale.reshape(1, pw))


def _outproj_kernel(ya_ref, yp_ref, w_ref, x_ref, gain_ref, o_ref, *, k_attn):
    k = pl.program_id(1)

    @pl.when(k == 0)
    def _():
        o_ref[...] = x_ref[...]

    @pl.when(k < k_attn)
    def _():
        o_ref[...] += jnp.dot(ya_ref[...], w_ref[...], preferred_element_type=F32)

    @pl.when(k >= k_attn)
    def _():
        o_ref[...] += jnp.dot(yp_ref[...], w_ref[...], preferred_element_type=F32)

    @pl.when(k == pl.num_programs(1) - 1)
    def _():
        h = o_ref[...]
        ms = jnp.mean(h * h, axis=-1, keepdims=True)
        o_ref[...] = h * lax.rsqrt(ms + RMS_EPS) * gain_ref[...]


def _out_projection(ya, yp, w_bf16, x, gain, tm, tk):
    m, d = x.shape
    ka, kp = ya.shape[1] // tk, yp.shape[1] // tk
    return pl.pallas_call(
        functools.partial(_outproj_kernel, k_attn=ka),
        grid=(m // tm, ka + kp),
        in_specs=[pl.BlockSpec((tm, tk), lambda i, k: (i, jnp.minimum(k, ka - 1))),
                  pl.BlockSpec((tm, tk), lambda i, k: (i, jnp.maximum(k - ka, 0))),
                  pl.BlockSpec((tk, d), lambda i, k: (k, 0)),
                  pl.BlockSpec((tm, d), lambda i, k: (i, 0)),
                  pl.BlockSpec((1, d), lambda i, k: (0, 0))],
        out_specs=pl.BlockSpec((tm, d), lambda i, k: (i, 0)),
        out_shape=jax.ShapeDtypeStruct((m, d), F32),
        compiler_params=pltpu.CompilerParams(
            dimension_semantics=("parallel", "arbitrary"), vmem_limit_bytes=VMEM_LIMIT),
    )(ya, yp, w_bf16, x, gain.reshape(1, d))


def kernel(x_prompt, x_sample, cache_k, cache_v, state_pool, page_table, norm_in, w_in, w_pool,
           pool_scale, w_out, rel_bias, norm_out):
    depth = norm_in.shape[0]
    assert depth == 1, "single layer step"
    n_batch, seq, d_model = x_prompt.shape
    bsz, dec_seq, _ = x_sample.shape
    assert n_batch == 1 and seq % (8 * MOBA_BLOCK) == 0 and dec_seq == 8
    past_len = page_table.shape[1] * PAGE_SIZE
    assert past_len >= POOL_HIST
    pool_width = pool_scale.shape[1]
    k0 = ATTN_WIDTH
    v0 = k0 + KV_WIDTH
    ga0 = v0 + KV_WIDTH
    u0 = ga0 + ATTN_WIDTH
    gp0 = u0 + pool_width
    assert w_in.shape[2] == gp0 + pool_width and w_out.shape[1] == ATTN_WIDTH + pool_width

    w_in_b = w_in[0].astype(BF16)
    w_out_b = w_out[0].astype(BF16)
    w_pool_b = w_pool[0].astype(BF16)
    xp = x_prompt[0]
    xs = x_sample.reshape(bsz * dec_seq, d_model)

    z = _in_projection(xp, norm_in[0], w_in_b, tm=512, tn=512)
    n_blk = seq // MOBA_BLOCK
    kmean = _block_means(z, n_blk, k0)
    kmean_pad = jnp.pad(kmean, ((0, LANES - n_blk), (0, 0)))
    ya = _prompt_attention(z, kmean_pad, rel_bias, seq, k0, v0, ga0)
    yp = _pool_prompt(z, w_pool_b, pool_scale[0], seq, u0, gp0, tm=256)
    y_prompt = _out_projection(ya, yp, w_out_b, xp, norm_out, tm=512, tk=512)

    zs = _in_projection(xs, norm_in[0], w_in_b, tm=bsz * dec_seq, tn=512)
    q_s, k_s, v_s = zs[:, :k0], zs[:, k0:v0], zs[:, v0:ga0]
    ga_s, u_s, gp_s = zs[:, ga0:u0], zs[:, u0:gp0], zs[:, gp0:]
    n_phys = cache_k.shape[1]
    ya_s = _sample_attention(q_s, k_s, v_s, ga_s, cache_k.reshape(n_phys, PAGE_SIZE, KV_WIDTH),
                             cache_v.reshape(n_phys, PAGE_SIZE, KV_WIDTH), page_table, rel_bias,
                             past_len, dec_seq)
    u_s3 = u_s.reshape(bsz, dec_seq, pool_width)
    u_ext = jnp.concatenate([jnp.zeros((bsz, 1, pool_width), F32), state_pool[0].astype(F32), u_s3], axis=1)
    yp_s = _pool_sample(u_ext.reshape(bsz * (POOL_HALO + dec_seq), pool_width), gp_s, w_pool_b,
                        pool_scale[0], bsz, dec_seq)
    y_sample = _out_projection(ya_s, yp_s, w_out_b, xs, norm_out, tm=bsz * dec_seq, tk=512)

    k_prompt = z[:, k0:v0].reshape(1, 1, seq, N_KV_HEADS, HEAD_DIM)
    v_prompt = z[:, v0:ga0].reshape(1, 1, seq, N_KV_HEADS, HEAD_DIM)
    pool_prompt = z[seq - POOL_HIST:, u0:gp0].reshape(1, 1, POOL_HIST, pool_width)
    k_sample = k_s.reshape(1, bsz, dec_seq, N_KV_HEADS, HEAD_DIM)
    v_sample = v_s.reshape(1, bsz, dec_seq, N_KV_HEADS, HEAD_DIM)
    pool_sample = u_ext[:, 1 + dec_seq:, :].reshape(1, bsz, POOL_HIST, pool_width).astype(state_pool.dtype)
    return (y_prompt.reshape(1, seq, d_model), y_sample.reshape(bsz, dec_seq, d_model),
            k_prompt, v_prompt, pool_prompt, k_sample, v_sample, pool_sample)
```

```python
import functools
import math

import numpy as np
import jax
import jax.numpy as jnp
from jax import lax
from jax.experimental import pallas as pl
from jax.experimental.pallas import tpu as pltpu

F32 = jnp.float32
BF16 = jnp.bfloat16

HEAD_DIM = 128
N_HEADS = 16
N_KV_HEADS = 4
KV_GROUP = N_HEADS // N_KV_HEADS
ATTN_WIDTH = N_HEADS * HEAD_DIM
KV_WIDTH = N_KV_HEADS * HEAD_DIM
GROUP_WIDTH = KV_GROUP * HEAD_DIM
ATTN_SCALE = HEAD_DIM ** -0.5
MOBA_BLOCK = 256
MOBA_TOPK = 3
PAGE_SIZE = 128
PAGES_PER_BLOCK = MOBA_BLOCK // PAGE_SIZE
POOL_WINDOWS = (2, 4, 8, 16)
N_POOL_GROUPS = len(POOL_WINDOWS)
POOL_HIST = max(POOL_WINDOWS) - 1
POOL_HALO = POOL_HIST + 1
NUM_BUCKETS = 32
MAX_EXACT = NUM_BUCKETS // 2
MAX_DISTANCE = 4096
RMS_EPS = 1e-6
NEG_INF = -1e30
GATE_MASKED = -3.0e38
GATE_VALID_MIN = -1.0e38
LANES = 128
TABLE_DELTAS = 3

VMEM_LIMIT = 56 * 1024 * 1024


def _rel_bucket(dist):
    n = jnp.maximum(dist, 0)
    nf = jnp.maximum(n, 1).astype(F32)
    large = MAX_EXACT + (jnp.log(nf / MAX_EXACT) / math.log(MAX_DISTANCE / MAX_EXACT)
                         * (NUM_BUCKETS - MAX_EXACT)).astype(jnp.int32)
    large = jnp.minimum(large, NUM_BUCKETS - 1)
    return jnp.where(n < MAX_EXACT, n, large)


def _rel_bucket_np(dist):
    n = np.maximum(dist, 0)
    nf = np.maximum(n, 1).astype(np.float32)
    large = MAX_EXACT + (np.log(nf / np.float32(MAX_EXACT)) / np.float32(math.log(MAX_DISTANCE / MAX_EXACT))
                         * np.float32(NUM_BUCKETS - MAX_EXACT)).astype(np.int32)
    large = np.minimum(large, NUM_BUCKETS - 1)
    return np.where(n < MAX_EXACT, n, large)


def _bias_of_bucket(rel_bias, bucket, bucket_axis):
    onehot = (jnp.expand_dims(bucket, bucket_axis)
              == jnp.arange(NUM_BUCKETS, dtype=jnp.int32).reshape(
                  (NUM_BUCKETS,) + (1,) * (bucket.ndim - bucket_axis))).astype(F32)
    out = jnp.tensordot(rel_bias.astype(F32).T, onehot, axes=((1,), (bucket_axis,)),
                        precision=lax.Precision.HIGHEST)
    return jnp.moveaxis(out, 0, bucket_axis)


def _silu(x):
    return x * jax.nn.sigmoid(x)


def _dot_nt(a, b, precision=None):
    return lax.dot_general(a, b, (((1,), (1,)), ((), ())), precision=precision,
                           preferred_element_type=F32)


def _select_top_blocks(gate, lane, n_past):
    gate = jnp.where(lane < n_past, gate, GATE_MASKED)
    lane_f = lane.astype(F32)
    sel = jnp.zeros(gate.shape, jnp.bool_)
    for _ in range(MOBA_TOPK):
        mx = jnp.max(gate, axis=-1, keepdims=True)
        first = jnp.min(jnp.where(gate == mx, lane_f, float(LANES)), axis=-1, keepdims=True)
        pick = (lane_f == first) & (mx > GATE_VALID_MIN)
        sel = sel | pick
        gate = jnp.where(pick, GATE_MASKED, gate)
    return sel


def _inproj_kernel(x_ref, gain_ref, w_ref, o_ref, xn_ref, *, row_chunk):
    @pl.when(pl.program_id(1) == 0)
    def _():
        for r in range(0, x_ref.shape[0], row_chunk):
            x = x_ref[r:r + row_chunk, :]
            ms = jnp.mean(x * x, axis=-1, keepdims=True)
            xn_ref[r:r + row_chunk, :] = (x * lax.rsqrt(ms + RMS_EPS) * gain_ref[...]).astype(BF16)

    o_ref[...] = jnp.dot(xn_ref[...], w_ref[...], preferred_element_type=F32)


def _in_projection(x, gain, w_bf16, tm, tn):
    m, d = x.shape
    n = w_bf16.shape[1]
    return pl.pallas_call(
        functools.partial(_inproj_kernel, row_chunk=min(tm, 128)),
        grid=(m // tm, n // tn),
        in_specs=[pl.BlockSpec((tm, d), lambda i, j: (i, 0)),
                  pl.BlockSpec((1, d), lambda i, j: (0, 0)),
                  pl.BlockSpec((d, tn), lambda i, j: (0, j))],
        out_specs=pl.BlockSpec((tm, tn), lambda i, j: (i, j)),
        out_shape=jax.ShapeDtypeStruct((m, n), F32),
        scratch_shapes=[pltpu.VMEM((tm, d), BF16)],
        compiler_params=pltpu.CompilerParams(
            dimension_semantics=("parallel", "arbitrary"), vmem_limit_bytes=VMEM_LIMIT),
    )(x, gain.reshape(1, d), w_bf16)


def _kmean_kernel(k_ref, o_ref, *, blocks_per_step):
    k = k_ref[...].reshape(blocks_per_step, MOBA_BLOCK, KV_WIDTH)
    o_ref[...] = jnp.sum(k, axis=1) * (1.0 / MOBA_BLOCK)


def _block_means(z, n_blk, k_col0):
    bps = 8
    assert n_blk % bps == 0 and k_col0 % KV_WIDTH == 0
    return pl.pallas_call(
        functools.partial(_kmean_kernel, blocks_per_step=bps),
        grid=(n_blk // bps,),
        in_specs=[pl.BlockSpec((bps * MOBA_BLOCK, KV_WIDTH), lambda i: (i, k_col0 // KV_WIDTH))],
        out_specs=pl.BlockSpec((bps, KV_WIDTH), lambda i: (i, 0)),
        out_shape=jax.ShapeDtypeStruct((n_blk, KV_WIDTH), F32),
        compiler_params=pltpu.CompilerParams(vmem_limit_bytes=VMEM_LIMIT),
    )(z)


def _prompt_attn_kernel(ti_ref, tn_ref, thr_ref,
                        q_ref, k_ref, v_ref, km_ref, ga_ref, tab_ref, blo_ref, bhi_ref,
                        o_ref, qaug_ref, m_ref, l_ref, acc_ref, dmat_ref):
    g = pl.program_id(0)
    s = pl.program_id(1)
    i = ti_ref[s]
    n = tn_ref[s]
    delta = i - n
    blk = MOBA_BLOCK
    all_rows = KV_GROUP * blk

    @pl.when(n == 0)
    def _init():
        m_ref[...] = jnp.full(m_ref.shape, NEG_INF, F32)
        l_ref[...] = jnp.zeros(l_ref.shape, F32)
        acc_ref[...] = jnp.zeros(acc_ref.shape, F32)
        dmat_ref[...] = (lax.broadcasted_iota(jnp.int32, (blk, blk), 0)
                         - lax.broadcasted_iota(jnp.int32, (blk, blk), 1))
        lane = lax.broadcasted_iota(jnp.int32, (all_rows, LANES), 1)
        q_all = jnp.concatenate(
            [q_ref[:, hh * HEAD_DIM:(hh + 1) * HEAD_DIM] for hh in range(KV_GROUP)], axis=0)
        gate = _dot_nt(q_all, km_ref[...], precision=lax.Precision.HIGHEST)
        sel = _select_top_blocks(gate, lane, i)
        selbias = jnp.where(sel | (lane == i), 0.0, NEG_INF)
        qaug_ref[:, 0:HEAD_DIM] = (q_all * ATTN_SCALE).astype(BF16)
        qaug_ref[:, HEAD_DIM:HEAD_DIM + LANES] = selbias.astype(BF16)

    lane = lax.broadcasted_iota(jnp.int32, (blk, LANES), 1)
    onehot = jnp.where(lane == n, 1.0, 0.0).astype(BF16)
    kaug = jnp.concatenate([k_ref[...].astype(BF16), onehot], axis=1)
    vaug = jnp.concatenate([v_ref[...].astype(BF16), jnp.ones((blk, LANES), BF16)], axis=1)

    def sweep(bias):
        sc = _dot_nt(qaug_ref[...], kaug) + bias
        m_old = m_ref[...]
        m_new = jnp.maximum(m_old, jnp.max(sc, axis=-1, keepdims=True))
        alpha = jnp.exp(m_old - m_new)
        p = jnp.exp(sc - jnp.concatenate([m_new] * (blk // LANES), axis=1))
        pv = jnp.dot(p.astype(BF16), vaug, preferred_element_type=F32)
        l_ref[...] = alpha * l_ref[...] + pv[:, HEAD_DIM:]
        acc_ref[...] = alpha * acc_ref[...] + pv[:, :HEAD_DIM]
        m_ref[...] = m_new

    @pl.when(delta < TABLE_DELTAS)
    def _near():
        sweep(tab_ref[0].reshape(all_rows, blk))

    @pl.when(delta >= TABLE_DELTAS)
    def _far():
        high = dmat_ref[...] >= thr_ref[delta]
        base = delta * N_HEADS + g * KV_GROUP
        sweep(jnp.concatenate([jnp.where(high, bhi_ref[base + hh], blo_ref[base + hh])
                               for hh in range(KV_GROUP)], axis=0))

    @pl.when(n == i)
    def _finish():
        attn = acc_ref[...] / l_ref[...]
        for hh in range(KV_GROUP):
            cols = slice(hh * HEAD_DIM, (hh + 1) * HEAD_DIM)
            o_ref[:, cols] = (attn[hh * blk:(hh + 1) * blk, :] * _silu(ga_ref[:, cols])).astype(BF16)


def _prompt_bias_tables(rel_bias, n_blk):
    blk = MOBA_BLOCK
    r = jnp.arange(blk, dtype=jnp.int32)
    dist = (jnp.arange(TABLE_DELTAS, dtype=jnp.int32)[:, None, None] * blk
            + r[None, :, None] - r[None, None, :])
    tab = _bias_of_bucket(rel_bias, _rel_bucket(dist), 1)
    tab = jnp.where((dist >= 0)[:, None], tab, NEG_INF)

    nd = max(n_blk, TABLE_DELTAS + 1)
    off = np.arange(-(blk - 1), blk, dtype=np.int32)
    dist_np = np.arange(nd, dtype=np.int32)[:, None] * blk + off[None, :]
    buckets_np = _rel_bucket_np(dist_np)
    span = buckets_np[TABLE_DELTAS:, -1] - buckets_np[TABLE_DELTAS:, 0]
    assert np.all(np.diff(buckets_np[TABLE_DELTAS:], axis=1) >= 0) and np.all(span <= 1), \
        "more than one bias bucket boundary inside a far block pair"
    buckets = _rel_bucket(jnp.asarray(dist_np))
    b_lo, b_hi = buckets[:, 0], buckets[:, -1]
    first_hi = jnp.argmax(buckets == b_hi[:, None], axis=1).astype(jnp.int32)
    thr = first_hi - (blk - 1)
    blo = rel_bias.astype(F32)[b_lo].reshape(-1)
    bhi = rel_bias.astype(F32)[b_hi].reshape(-1)
    return tab, thr, blo, bhi


def _prompt_attention(z, kmean_pad, rel_bias, seq, k_col0, v_col0, ga_col0):
    blk = MOBA_BLOCK
    n_blk = seq // blk
    tab, thr, blo, bhi = _prompt_bias_tables(rel_bias, n_blk)
    ti = np.concatenate([np.full(i + 1, i, np.int32) for i in range(n_blk)])
    tn = np.concatenate([np.arange(i + 1, dtype=np.int32) for i in range(n_blk)])
    kc, vc, gc = k_col0 // HEAD_DIM, v_col0 // HEAD_DIM, ga_col0 // GROUP_WIDTH
    smem = pl.BlockSpec(memory_space=pltpu.SMEM)
    grid_spec = pltpu.PrefetchScalarGridSpec(
        num_scalar_prefetch=3,
        grid=(N_KV_HEADS, len(ti)),
        in_specs=[
            pl.BlockSpec((blk, GROUP_WIDTH), lambda g, s, ti, tn, th: (ti[s], g)),
            pl.BlockSpec((blk, HEAD_DIM), lambda g, s, ti, tn, th: (tn[s], kc + g)),
            pl.BlockSpec((blk, HEAD_DIM), lambda g, s, ti, tn, th: (tn[s], vc + g)),
            pl.BlockSpec((LANES, HEAD_DIM), lambda g, s, ti, tn, th: (0, g)),
            pl.BlockSpec((blk, GROUP_WIDTH), lambda g, s, ti, tn, th: (ti[s], gc + g)),
            pl.BlockSpec((1, KV_GROUP, blk, blk),
                         lambda g, s, ti, tn, th: (jnp.minimum(ti[s] - tn[s], TABLE_DELTAS - 1), g, 0, 0)),
            smem, smem,
        ],
        out_specs=pl.BlockSpec((blk, GROUP_WIDTH), lambda g, s, ti, tn, th: (ti[s], g)),
        scratch_shapes=[
            pltpu.VMEM((KV_GROUP * blk, HEAD_DIM + LANES), BF16),
            pltpu.VMEM((KV_GROUP * blk, LANES), F32),
            pltpu.VMEM((KV_GROUP * blk, LANES), F32),
            pltpu.VMEM((KV_GROUP * blk, HEAD_DIM), F32),
            pltpu.VMEM((blk, blk), jnp.int32),
        ],
    )
    return pl.pallas_call(
        _prompt_attn_kernel,
        grid_spec=grid_spec,
        out_shape=jax.ShapeDtypeStruct((seq, ATTN_WIDTH), BF16),
        compiler_params=pltpu.CompilerParams(
            dimension_semantics=("arbitrary", "arbitrary"), vmem_limit_bytes=VMEM_LIMIT),
    )(jnp.asarray(ti), jnp.asarray(tn), thr, z, z, z, kmean_pad, z, tab, blo, bhi)


def _block_of_pages(page_refs):
    return jnp.concatenate(
        [jnp.concatenate([ref[0, pl.ds(gi, PAGE_SIZE, stride=N_KV_HEADS), :] for gi in range(N_KV_HEADS)],
                         axis=1) for ref in page_refs], axis=0)


def _sample_attn_kernel(pt_ref,q_ref, knew_ref, vnew_ref, ga_ref, *rest, n_past, dec_seq, blocks_per_step):
    n_pages = blocks_per_step * PAGES_PER_BLOCK
    k_pages, v_pages = rest[:n_pages], rest[n_pages:2 * n_pages]
    bias_ref, o_ref, qf_ref, qb_ref, gate_ref, mx_ref, l_ref, part_ref = rest[2 * n_pages:]
    t = pl.program_id(1)
    rows = N_HEADS * dec_seq
    grp_rows = KV_GROUP * dec_seq
    lane = lax.broadcasted_iota(jnp.int32, (rows, LANES), 1)

    @pl.when(t == 0)
    def _init():
        gate_ref[...] = jnp.zeros(gate_ref.shape, F32)
        mx_ref[...] = jnp.zeros(mx_ref.shape, F32)
        l_ref[...] = jnp.zeros(l_ref.shape, F32)
        col_grp = lax.broadcasted_iota(jnp.int32, (dec_seq, KV_WIDTH), 1) // HEAD_DIM
        for h in range(N_HEADS):
            qh = q_ref[:, h * HEAD_DIM:(h + 1) * HEAD_DIM]
            wide = jnp.concatenate([qh] * N_KV_HEADS, axis=1)
            wide = jnp.where(col_grp == h // KV_GROUP, wide, 0.0)
            qf_ref[h * dec_seq:(h + 1) * dec_seq, :] = wide
            qb_ref[h * dec_seq:(h + 1) * dec_seq, :] = (wide * ATTN_SCALE).astype(BF16)

    def block_partial(kblk, vblk, bias):
        sc = _dot_nt(qb_ref[...], kblk.astype(BF16)) + bias
        m_n = jnp.max(sc, axis=-1, keepdims=True)
        p = jnp.exp(sc - m_n)
        l_n = jnp.sum(p, axis=-1, keepdims=True)
        o_all = jnp.dot(p.astype(BF16), vblk.astype(BF16), preferred_element_type=F32)
        o_n = jnp.concatenate(
            [o_all[gi * grp_rows:(gi + 1) * grp_rows, gi * HEAD_DIM:(gi + 1) * HEAD_DIM]
             for gi in range(N_KV_HEADS)], axis=0)
        return m_n, l_n, o_n

    @pl.when(t < n_past // blocks_per_step)
    def _past():
        mx, ls, gt = mx_ref[...], l_ref[...], gate_ref[...]
        for jb in range(blocks_per_step):
            n = t * blocks_per_step + jb
            kblk = _block_of_pages(k_pages[jb * PAGES_PER_BLOCK:(jb + 1) * PAGES_PER_BLOCK])
            vblk = _block_of_pages(v_pages[jb * PAGES_PER_BLOCK:(jb + 1) * PAGES_PER_BLOCK])
            kmean = jnp.sum(kblk, axis=0, keepdims=True) * (1.0 / MOBA_BLOCK)
            gate = jnp.sum(qf_ref[...] * kmean, axis=-1, keepdims=True)
            m_n, l_n, o_n = block_partial(kblk, vblk, bias_ref[jb])
            part_ref[n] = o_n
            here = lane == n
            mx = jnp.where(here, m_n, mx)
            ls = jnp.where(here, l_n, ls)
            gt = jnp.where(here, gate, gt)
        mx_ref[...] = mx
        l_ref[...] = ls
        gate_ref[...] = gt

    @pl.when(t == n_past // blocks_per_step)
    def _own():
        pad = jnp.zeros((MOBA_BLOCK - dec_seq, KV_WIDTH), F32)
        m_n, l_n, o_n = block_partial(jnp.concatenate([knew_ref[...], pad], axis=0),
                                      jnp.concatenate([vnew_ref[...], pad], axis=0), bias_ref[0])
        own = lane == n_past
        mx = jnp.where(own, m_n, mx_ref[...])
        ls = jnp.where(own, l_n, l_ref[...])
        sel = _select_top_blocks(gate_ref[...], lane, n_past)
        include = sel | own
        m_tot = jnp.max(jnp.where(include, mx, GATE_MASKED), axis=-1, keepdims=True)
        e = jnp.where(include, jnp.exp(jnp.minimum(mx - m_tot, 0.0)), 0.0)
        denom = jnp.sum(e * ls, axis=-1, keepdims=True)
        num = e[:, n_past:n_past + 1] * o_n
        for j in range(n_past):
            num = num + e[:, j:j + 1] * part_ref[j]
        attn = num / denom
        for h in range(N_HEADS):
            cols = slice(h * HEAD_DIM, (h + 1) * HEAD_DIM)
            o_ref[0, :, cols] = (attn[h * dec_seq:(h + 1) * dec_seq, :] * _silu(ga_ref[:, cols])).astype(BF16)


def _sample_bias_table(rel_bias, past_len, dec_seq, n_blocks):
    kpos = jnp.arange(n_blocks * MOBA_BLOCK, dtype=jnp.int32).reshape(n_blocks, 1, MOBA_BLOCK)
    qpos = (past_len + jnp.arange(dec_seq, dtype=jnp.int32)).reshape(1, dec_seq, 1)
    dist = qpos - kpos
    vals = _bias_of_bucket(rel_bias, _rel_bucket(dist), 1)
    vals = jnp.where((dist >= 0)[:, None], vals, NEG_INF)
    return vals.reshape(n_blocks, N_HEADS * dec_seq, MOBA_BLOCK)


def _sample_attention(q_s, k_s, v_s, ga_s, cache_k, cache_v, page_table, rel_bias, past_len, dec_seq):
    bsz, n_pages = page_table.shape
    n_past = past_len // MOBA_BLOCK
    bps = 4
    assert past_len % MOBA_BLOCK == 0 and n_past + 1 <= LANES and n_pages == n_past * PAGES_PER_BLOCK
    assert n_past % bps == 0
    n_steps = n_past // bps + 1
    pages_per_step = bps * PAGES_PER_BLOCK
    bias = _sample_bias_table(rel_bias, past_len, dec_seq, n_steps * bps)
    rows = N_HEADS * dec_seq

    def page_spec(which):
        def index_map(b, t, pt):
            step = jnp.minimum(t, n_steps - 2)
            return (pt[b * n_pages + step * pages_per_step + which], 0, 0)
        return pl.BlockSpec((1, PAGE_SIZE * N_KV_HEADS, HEAD_DIM), index_map)

    row_spec = lambda width: pl.BlockSpec((dec_seq, width), lambda b, t, pt: (b, 0))
    grid_spec = pltpu.PrefetchScalarGridSpec(
        num_scalar_prefetch=1,
        grid=(bsz, n_steps),
        in_specs=([row_spec(ATTN_WIDTH), row_spec(KV_WIDTH), row_spec(KV_WIDTH), row_spec(ATTN_WIDTH)]
                  + [page_spec(j) for j in range(pages_per_step)]
                  + [page_spec(j) for j in range(pages_per_step)]
                  + [pl.BlockSpec((bps, rows, MOBA_BLOCK), lambda b, t, pt: (t, 0, 0))]),
        out_specs=pl.BlockSpec((1, dec_seq, ATTN_WIDTH), lambda b, t, pt: (b, 0, 0)),
        scratch_shapes=[
            pltpu.VMEM((rows, KV_WIDTH), F32),
            pltpu.VMEM((rows, KV_WIDTH), BF16),
            pltpu.VMEM((rows, LANES), F32),
            pltpu.VMEM((rows, LANES), F32),
            pltpu.VMEM((rows, LANES), F32),
            pltpu.VMEM((n_past, rows, HEAD_DIM), F32),
        ],
    )
    out = pl.pallas_call(
        functools.partial(_sample_attn_kernel, n_past=n_past, dec_seq=dec_seq, blocks_per_step=bps),
        grid_spec=grid_spec,
        out_shape=jax.ShapeDtypeStruct((bsz, dec_seq, ATTN_WIDTH), BF16),
        compiler_params=pltpu.CompilerParams(
            dimension_semantics=("arbitrary", "arbitrary"), vmem_limit_bytes=VMEM_LIMIT),
    )(page_table.reshape(-1), q_s, k_s, v_s, ga_s, *([cache_k] * pages_per_step),
      *([cache_v] * pages_per_step), bias)
    return out.reshape(bsz * dec_seq, ATTN_WIDTH)


def _window_mix(ext_ref, window, rows, cols, count, u, gp, w, scale):
    win = u
    for j in range(1, window):
        win = win + ext_ref[POOL_HALO - j:POOL_HALO - j + rows, cols]
    zed = win / count - u
    y = jnp.dot(zed.astype(BF16), w, preferred_element_type=F32)
    return (y * scale) * _silu(gp)


def _pool_prompt_kernel(prev_ref, u_ref, gp_ref, w_ref, scale_ref, o_ref, ext_ref, *, groups_per_step):
    i = pl.program_id(0)
    rows = u_ref.shape[0]
    gw = w_ref.shape[1]
    ext_ref[0:POOL_HALO, :] = jnp.where(i == 0, 0.0, prev_ref[...])
    ext_ref[POOL_HALO:POOL_HALO + rows, :] = u_ref[...]
    pos = i * rows + lax.broadcasted_iota(jnp.int32, (rows, 1), 0)
    for part in range(N_POOL_GROUPS // groups_per_step):
        @pl.when(pl.program_id(1) == part)
        def _(part=part):
            for gl in range(groups_per_step):
                window = POOL_WINDOWS[part * groups_per_step + gl]
                cols = slice(gl * gw, (gl + 1) * gw)
                count = jnp.minimum(window, pos + 1).astype(F32)
                y = _window_mix(ext_ref, window, rows, cols, count, u_ref[:, cols], gp_ref[:, cols],
                                w_ref[gl], scale_ref[:, cols])
                o_ref[:, cols] = y.astype(BF16)


def _pool_prompt(z, w_pool_bf16, pool_scale, seq, u_col0, gp_col0, tm):
    n_groups, gw, _ = w_pool_bf16.shape
    gps = 2
    bw = gps * gw
    assert n_groups == N_POOL_GROUPS and u_col0 % bw == 0 and gp_col0 % bw == 0 and tm % POOL_HALO == 0
    uc, gc = u_col0 // bw, gp_col0 // bw
    halo_blocks = tm // POOL_HALO
    return pl.pallas_call(
        functools.partial(_pool_prompt_kernel, groups_per_step=gps),
        grid=(seq // tm, n_groups // gps),
        in_specs=[pl.BlockSpec((POOL_HALO, bw), lambda i, c: (jnp.maximum(i * halo_blocks - 1, 0), uc + c)),
                  pl.BlockSpec((tm, bw), lambda i, c: (i, uc + c)),
                  pl.BlockSpec((tm, bw), lambda i, c: (i, gc + c)),
                  pl.BlockSpec((gps, gw, gw), lambda i, c: (c, 0, 0)),
                  pl.BlockSpec((1, bw), lambda i, c: (0, c))],
        out_specs=pl.BlockSpec((tm, bw), lambda i, c: (i, c)),
        out_shape=jax.ShapeDtypeStruct((seq, n_groups * gw), BF16),
        scratch_shapes=[pltpu.VMEM((POOL_HALO + tm, bw), F32)],
        compiler_params=pltpu.CompilerParams(
            dimension_semantics=("parallel", "arbitrary"), vmem_limit_bytes=VMEM_LIMIT),
    )(z, z, z, w_pool_bf16, pool_scale.reshape(1, n_groups * gw))


def _pool_sample_kernel(ext_in_ref, gp_ref, w_ref, scale_ref, o_ref, ext_ref, *, bsz, dec_seq):
    seg = POOL_HALO + dec_seq
    rows = bsz * seg
    gw = w_ref.shape[1]
    ext_ref[0:POOL_HALO, :] = jnp.zeros((POOL_HALO, ext_ref.shape[1]), F32)
    ext_ref[POOL_HALO:POOL_HALO + rows, :] = ext_in_ref[...]
    for gi, window in enumerate(POOL_WINDOWS):
        cols = slice(gi * gw, (gi + 1) * gw)
        u_all = ext_in_ref[:, cols]
        win = u_all
        for j in range(1, window):
            win = win + ext_ref[POOL_HALO - j:POOL_HALO - j + rows, cols]
        zed = (win / float(window) - u_all).reshape(bsz, seg, gw)[:, POOL_HALO:, :].reshape(bsz * dec_seq, gw)
        y = jnp.dot(zed.astype(BF16), w_ref[gi], preferred_element_type=F32)
        o_ref[:, cols] = ((y * scale_ref[:, cols]) * _silu(gp_ref[:, cols])).astype(BF16)


def _pool_sample(u_ext, gp_s, w_pool_bf16, pool_scale, bsz, dec_seq):
    pw = u_ext.shape[1]
    rows = u_ext.shape[0]
    vm = pl.BlockSpec(memory_space=pltpu.VMEM)
    return pl.pallas_call(
        functools.partial(_pool_sample_kernel, bsz=bsz, dec_seq=dec_seq),
        in_specs=[vm, vm, vm, vm],
        out_specs=vm,
        out_shape=jax.ShapeDtypeStruct((bsz * dec_seq, pw), BF16),
        scratch_shapes=[pltpu.VMEM((POOL_HALO + rows, pw), F32)],
        compiler_params=pltpu.CompilerParams(vmem_limit_bytes=VMEM_LIMIT),
    )(u_ext, gp_s, w_pool_bf16, pool_scale.reshape(1, pw))


def _outproj_kernel(ya_ref, yp_ref, w_ref, x_ref, gain_ref, o_ref, *, k_attn):
    k = pl.program_id(1)

    @pl.when(k == 0)
    def _():
        o_ref[...] = x_ref[...]

    @pl.when(k < k_attn)
    def _():
        o_ref[...] += jnp.dot(ya_ref[...], w_ref[...], preferred_element_type=F32)

    @pl.when(k >= k_attn)
    def _():
        o_ref[...] += jnp.dot(yp_ref[...], w_ref[...], preferred_element_type=F32)

    @pl.when(k == pl.num_programs(1) - 1)
    def _():
        h = o_ref[...]
        ms = jnp.mean(h * h, axis=-1, keepdims=True)
        o_ref[...] = h * lax.rsqrt(ms + RMS_EPS) * gain_ref[...]


def _out_projection(ya, yp, w_bf16, x, gain, tm, tk):
    m, d = x.shape
    ka, kp = ya.shape[1] // tk, yp.shape[1] // tk
    return pl.pallas_call(
        functools.partial(_outproj_kernel, k_attn=ka),
        grid=(m // tm, ka + kp),
        in_specs=[pl.BlockSpec((tm, tk), lambda i, k: (i, jnp.minimum(k, ka - 1))),
                  pl.BlockSpec((tm, tk), lambda i, k: (i, jnp.maximum(k - ka, 0))),
                  pl.BlockSpec((tk, d), lambda i, k: (k, 0)),
                  pl.BlockSpec((tm, d), lambda i, k: (i, 0)),
                  pl.BlockSpec((1, d), lambda i, k: (0, 0))],
        out_specs=pl.BlockSpec((tm, d), lambda i, k: (i, 0)),
        out_shape=jax.ShapeDtypeStruct((m, d), F32),
        compiler_params=pltpu.CompilerParams(
            dimension_semantics=("parallel", "arbitrary"), vmem_limit_bytes=VMEM_LIMIT),
    )(ya, yp, w_bf16, x, gain.reshape(1, d))


def kernel(x_prompt, x_sample, cache_k, cache_v, state_pool, page_table, norm_in, w_in, w_pool,
           pool_scale, w_out, rel_bias, norm_out):
    depth = norm_in.shape[0]
    assert depth == 1, "single layer step"
    n_batch, seq, d_model = x_prompt.shape
    bsz, dec_seq, _ = x_sample.shape
    assert n_batch == 1 and seq % (8 * MOBA_BLOCK) == 0 and dec_seq == 8
    past_len = page_table.shape[1] * PAGE_SIZE
    assert past_len >= POOL_HIST
    pool_width = pool_scale.shape[1]
    k0 = ATTN_WIDTH
    v0 = k0 + KV_WIDTH
    ga0 = v0 + KV_WIDTH
    u0 = ga0 + ATTN_WIDTH
    gp0 = u0 + pool_width
    assert w_in.shape[2] == gp0 + pool_width and w_out.shape[1] == ATTN_WIDTH + pool_width

    w_in_b = w_in[0].astype(BF16)
    w_out_b = w_out[0].astype(BF16)
    w_pool_b = w_pool[0].astype(BF16)
    xp = x_prompt[0]
    xs = x_sample.reshape(bsz * dec_seq, d_model)

    z = _in_projection(xp, norm_in[0], w_in_b, tm=512, tn=1024)
    n_blk = seq // MOBA_BLOCK
    kmean = _block_means(z, n_blk, k0)
    kmean_pad = jnp.pad(kmean, ((0, LANES - n_blk), (0, 0)))
    ya = _prompt_attention(z, kmean_pad, rel_bias, seq, k0, v0, ga0)
    yp = _pool_prompt(z, w_pool_b, pool_scale[0], seq, u0, gp0, tm=256)
    y_prompt = _out_projection(ya, yp, w_out_b, xp, norm_out, tm=512, tk=512)

    zs = _in_projection(xs, norm_in[0], w_in_b, tm=bsz * dec_seq, tn=1024)
    q_s, k_s, v_s = zs[:, :k0], zs[:, k0:v0], zs[:, v0:ga0]
    ga_s, u_s, gp_s = zs[:, ga0:u0], zs[:, u0:gp0], zs[:, gp0:]
    n_phys = cache_k.shape[1]
    page_rows = PAGE_SIZE * N_KV_HEADS
    ya_s = _sample_attention(q_s, k_s, v_s, ga_s, cache_k.reshape(n_phys, page_rows, HEAD_DIM),
                             cache_v.reshape(n_phys, page_rows, HEAD_DIM), page_table, rel_bias,
                             past_len, dec_seq)
    u_s3 = u_s.reshape(bsz, dec_seq, pool_width)
    u_ext = jnp.concatenate([jnp.zeros((bsz, 1, pool_width), F32), state_pool[0].astype(F32), u_s3], axis=1)
    yp_s = _pool_sample(u_ext.reshape(bsz * (POOL_HALO + dec_seq), pool_width), gp_s, w_pool_b,
                        pool_scale[0], bsz, dec_seq)
    y_sample = _out_projection(ya_s, yp_s, w_out_b, xs, norm_out, tm=bsz * dec_seq, tk=512)

    k_prompt = z[:, k0:v0].reshape(1, 1, seq, N_KV_HEADS, HEAD_DIM)
    v_prompt = z[:, v0:ga0].reshape(1, 1, seq, N_KV_HEADS, HEAD_DIM)
    pool_prompt = z[seq - POOL_HIST:, u0:gp0].reshape(1, 1, POOL_HIST, pool_width)
    k_sample = k_s.reshape(1, bsz, dec_seq, N_KV_HEADS, HEAD_DIM)
    v_sample = v_s.reshape(1, bsz, dec_seq, N_KV_HEADS, HEAD_DIM)
    pool_sample = u_ext[:, 1 + dec_seq:, :].reshape(1, bsz, POOL_HIST, pool_width).astype(state_pool.dtype)
    return (y_prompt.reshape(1, seq, d_model), y_sample.reshape(bsz, dec_seq, d_model),
            k_prompt, v_prompt, pool_prompt, k_sample, v_sample, pool_sample)
```

```python
import functools
import math

import numpy as np
import jax
import jax.numpy as jnp
from jax import lax
from jax.experimental import pallas as pl
from jax.experimental.pallas import tpu as pltpu

F32 = jnp.float32
BF16 = jnp.bfloat16

HEAD_DIM = 128
N_HEADS = 16
N_KV_HEADS = 4
KV_GROUP = N_HEADS // N_KV_HEADS
ATTN_WIDTH = N_HEADS * HEAD_DIM
KV_WIDTH = N_KV_HEADS * HEAD_DIM
GROUP_WIDTH = KV_GROUP * HEAD_DIM
ATTN_SCALE = HEAD_DIM ** -0.5
LOG2_E = math.log2(math.e)
MOBA_BLOCK = 256
MOBA_TOPK = 3
PAGE_SIZE = 128
PAGES_PER_BLOCK = MOBA_BLOCK // PAGE_SIZE
POOL_WINDOWS = (2, 4, 8, 16)
N_POOL_GROUPS = len(POOL_WINDOWS)
POOL_HIST = max(POOL_WINDOWS) - 1
POOL_HALO = POOL_HIST + 1
NUM_BUCKETS = 32
MAX_EXACT = NUM_BUCKETS // 2
MAX_DISTANCE = 4096
RMS_EPS = 1e-6
NEG_INF = -1e30
GATE_MASKED = -3.0e38
GATE_VALID_MIN = -1.0e38
LANES = 128
TABLE_DELTAS = 3
SWEEP_ROWS = 512
FAR_GROUP_BLOCKS = 2
FAR_GROUPS = 2

VMEM_LIMIT = 56 * 1024 * 1024


def _rel_bucket(dist):
    n = jnp.maximum(dist, 0)
    nf = jnp.maximum(n, 1).astype(F32)
    large = MAX_EXACT + (jnp.log(nf / MAX_EXACT) / math.log(MAX_DISTANCE / MAX_EXACT)
                         * (NUM_BUCKETS - MAX_EXACT)).astype(jnp.int32)
    large = jnp.minimum(large, NUM_BUCKETS - 1)
    return jnp.where(n < MAX_EXACT, n, large)


def _rel_bucket_np(dist):
    n = np.maximum(dist, 0)
    nf = np.maximum(n, 1).astype(np.float32)
    large = MAX_EXACT + (np.log(nf / np.float32(MAX_EXACT)) / np.float32(math.log(MAX_DISTANCE / MAX_EXACT))
                         * np.float32(NUM_BUCKETS - MAX_EXACT)).astype(np.int32)
    large = np.minimum(large, NUM_BUCKETS - 1)
    return np.where(n < MAX_EXACT, n, large)


def _bias_of_bucket(rel_bias, bucket, bucket_axis):
    onehot = (jnp.expand_dims(bucket, bucket_axis)
              == jnp.arange(NUM_BUCKETS, dtype=jnp.int32).reshape(
                  (NUM_BUCKETS,) + (1,) * (bucket.ndim - bucket_axis))).astype(F32)
    out = jnp.tensordot(rel_bias.astype(F32).T, onehot, axes=((1,), (bucket_axis,)),
                        precision=lax.Precision.HIGHEST)
    return jnp.moveaxis(out, 0, bucket_axis)


def _silu(x):
    return x * jax.nn.sigmoid(x)


def _dot_nt(a, b, precision=None):
    return lax.dot_general(a, b, (((1,), (1,)), ((), ())), precision=precision,
                           preferred_element_type=F32)


def _select_top_blocks(gate, block_id, n_past, axis=-1):
    gate = jnp.where(block_id < n_past, gate, GATE_MASKED)
    id_f = block_id.astype(F32)
    sel = jnp.zeros(gate.shape, jnp.bool_)
    for _ in range(MOBA_TOPK):
        mx = jnp.max(gate, axis=axis, keepdims=True)
        first = jnp.min(jnp.where(gate == mx, id_f, float(LANES)), axis=axis, keepdims=True)
        pick = (id_f == first) & (mx > GATE_VALID_MIN)
        sel = sel | pick
        gate = jnp.where(pick, GATE_MASKED, gate)
    return sel


def _inproj_kernel(x_ref, gain_ref, w_ref, o_ref, xn_ref, *, row_chunk):
    @pl.when(pl.program_id(1) == 0)
    def _():
        for r in range(0, x_ref.shape[0], row_chunk):
            x = x_ref[r:r + row_chunk, :]
            ms = jnp.mean(x * x, axis=-1, keepdims=True)
            xn_ref[r:r + row_chunk, :] = (x * lax.rsqrt(ms + RMS_EPS) * gain_ref[...]).astype(BF16)

    o_ref[...] = jnp.dot(xn_ref[...], w_ref[...], preferred_element_type=F32)


def _in_projection(x, gain, w_bf16, tm, tn):
    m, d = x.shape
    n = w_bf16.shape[1]
    return pl.pallas_call(
        functools.partial(_inproj_kernel, row_chunk=min(tm, 128)),
        grid=(m // tm, n // tn),
        in_specs=[pl.BlockSpec((tm, d), lambda i, j: (i, 0)),
                  pl.BlockSpec((1, d), lambda i, j: (0, 0)),
                  pl.BlockSpec((d, tn), lambda i, j: (0, j))],
        out_specs=pl.BlockSpec((tm, tn), lambda i, j: (i, j)),
        out_shape=jax.ShapeDtypeStruct((m, n), F32),
        scratch_shapes=[pltpu.VMEM((tm, d), BF16)],
        compiler_params=pltpu.CompilerParams(
            dimension_semantics=("parallel", "arbitrary"), vmem_limit_bytes=VMEM_LIMIT),
    )(x, gain.reshape(1, d), w_bf16)


def _kmean_kernel(k_ref, o_ref, *, blocks_per_step):
    k = k_ref[...].reshape(blocks_per_step, MOBA_BLOCK, KV_WIDTH)
    o_ref[...] = jnp.sum(k, axis=1) * (1.0 / MOBA_BLOCK)


def _block_means(z, n_blk, k_col0):
    bps = 8
    assert n_blk % bps == 0 and k_col0 % KV_WIDTH == 0
    return pl.pallas_call(
        functools.partial(_kmean_kernel, blocks_per_step=bps),
        grid=(n_blk // bps,),
        in_specs=[pl.BlockSpec((bps * MOBA_BLOCK, KV_WIDTH), lambda i: (i, k_col0 // KV_WIDTH))],
        out_specs=pl.BlockSpec((bps, KV_WIDTH), lambda i: (i, 0)),
        out_shape=jax.ShapeDtypeStruct((n_blk, KV_WIDTH), F32),
        compiler_params=pltpu.CompilerParams(vmem_limit_bytes=VMEM_LIMIT),
    )(z)


def _prompt_attn_kernel(thr_ref,
                        q_ref, k_ref, v_ref, km_ref, ga_ref, tab_ref, blo_ref, bhi_ref,
                        o_ref, kaug_ref, vaug_ref, qaug_ref, m_ref, l_ref, acc_ref, dmat_ref,
                        *, n_blk_pad):
    g = pl.program_id(0)
    i = pl.program_id(1)
    blk = MOBA_BLOCK
    all_rows = KV_GROUP * blk
    n_blk = k_ref.shape[0] // blk

    @pl.when(i == 0)
    def _stage_keys_values():
        lane = lax.broadcasted_iota(jnp.int32, (blk, LANES), 1)
        ones = jnp.ones((blk, LANES), BF16)

        def stage(n, carry):
            r = pl.multiple_of(n * blk, blk)
            kaug_ref[pl.ds(r, blk), 0:HEAD_DIM] = k_ref[pl.ds(r, blk), :].astype(BF16)
            kaug_ref[pl.ds(r, blk), HEAD_DIM:HEAD_DIM + LANES] = jnp.where(lane == n, 1.0, 0.0).astype(BF16)
            vaug_ref[pl.ds(r, blk), 0:HEAD_DIM] = v_ref[pl.ds(r, blk), :].astype(BF16)
            vaug_ref[pl.ds(r, blk), HEAD_DIM:HEAD_DIM + LANES] = ones
            return carry

        lax.fori_loop(0, n_blk, stage, 0)
        dmat_ref[...] = (lax.broadcasted_iota(jnp.int32, (blk, blk), 0)
                         - lax.broadcasted_iota(jnp.int32, (blk, blk), 1))

    m_ref[...] = jnp.full(m_ref.shape, NEG_INF, F32)
    l_ref[...] = jnp.zeros(l_ref.shape, F32)
    acc_ref[...] = jnp.zeros(acc_ref.shape, F32)
    q_all = jnp.concatenate(
        [q_ref[:, hh * HEAD_DIM:(hh + 1) * HEAD_DIM] for hh in range(KV_GROUP)], axis=0)
    gate_t = _dot_nt(km_ref[0:n_blk_pad, :], q_all, precision=lax.Precision.HIGHEST)
    blk_id = lax.broadcasted_iota(jnp.int32, (n_blk_pad, all_rows), 0)
    sel_t = _select_top_blocks(gate_t, blk_id, i, axis=0)
    mask_t = jnp.where(sel_t | (blk_id == i), 0.0, NEG_INF)
    if n_blk_pad < LANES:
        mask_t = jnp.concatenate([mask_t, jnp.zeros((LANES - n_blk_pad, all_rows), F32)], axis=0)
    qaug_ref[:, 0:HEAD_DIM] = (q_all * (ATTN_SCALE * LOG2_E)).astype(BF16)
    qaug_ref[:, HEAD_DIM:HEAD_DIM + LANES] = mask_t.T.astype(BF16)

    def sweep(groups):
        m_all, l_all, acc_all = m_ref[...], l_ref[...], acc_ref[...]
        chunks = [slice(c, c + SWEEP_ROWS) for c in range(0, all_rows, SWEEP_ROWS)]
        m_c = [m_all[rows] for rows in chunks]
        l_c = [l_all[rows] for rows in chunks]
        acc_c = [acc_all[rows] for rows in chunks]
        for n0, nb, bias in groups:
            r = pl.multiple_of(n0 * blk, blk)
            keys = kaug_ref[pl.ds(r, nb * blk), :]
            values = vaug_ref[pl.ds(r, nb * blk), :]
            for ci, rows in enumerate(chunks):
                sc = _dot_nt(qaug_ref[rows, :], keys) + bias[rows]
                m_new = jnp.maximum(m_c[ci], jnp.max(sc, axis=-1, keepdims=True))
                alpha = jnp.exp2(m_c[ci] - m_new)
                p = jnp.exp2(sc - jnp.concatenate([m_new] * (nb * blk // LANES), axis=1))
                pv = jnp.dot(p.astype(BF16), values, preferred_element_type=F32)
                l_c[ci] = alpha * l_c[ci] + pv[:, HEAD_DIM:]
                acc_c[ci] = alpha * acc_c[ci] + pv[:, :HEAD_DIM]
                m_c[ci] = m_new
        m_ref[...] = jnp.concatenate(m_c, axis=0)
        l_ref[...] = jnp.concatenate(l_c, axis=0)
        acc_ref[...] = jnp.concatenate(acc_c, axis=0)

    def far_bias(n0, nb):
        tiles = []
        for j in range(nb):
            delta = i - (n0 + j)
            high = dmat_ref[...] >= thr_ref[delta]
            base = delta * N_HEADS + g * KV_GROUP
            tiles.append(jnp.concatenate([jnp.where(high, bhi_ref[base + hh], blo_ref[base + hh])
                                          for hh in range(KV_GROUP)], axis=0))
        return jnp.concatenate(tiles, axis=1)

    n_far = jnp.maximum(i - (TABLE_DELTAS - 1), 0)
    per_iter = FAR_GROUPS * FAR_GROUP_BLOCKS

    def far_step(t, carry):
        first = t * per_iter
        sweep([(first + j * FAR_GROUP_BLOCKS, FAR_GROUP_BLOCKS,
                far_bias(first + j * FAR_GROUP_BLOCKS, FAR_GROUP_BLOCKS)) for j in range(FAR_GROUPS)])
        return carry

    lax.fori_loop(0, n_far // per_iter, far_step, 0)

    left = n_far % per_iter
    near = jnp.minimum(i + 1, TABLE_DELTAS)
    for count in range(1, TABLE_DELTAS + 1):
        for rem in range(per_iter if count == TABLE_DELTAS else 1):
            @pl.when((near == count) & (left == rem))
            def _tail(count=count, rem=rem):
                groups = []
                for first in range(0, rem, FAR_GROUP_BLOCKS):
                    nb = min(FAR_GROUP_BLOCKS, rem - first)
                    groups.append((n_far - rem + first, nb, far_bias(n_far - rem + first, nb)))
                groups.append((i - (count - 1), count, jnp.concatenate(
                    [tab_ref[count - 1 - j].reshape(all_rows, blk) for j in range(count)], axis=1)))
                sweep(groups)


    attn = acc_ref[...] / l_ref[...]
    for hh in range(KV_GROUP):
        cols = slice(hh * HEAD_DIM, (hh + 1) * HEAD_DIM)
        o_ref[:, cols] = (attn[hh * blk:(hh + 1) * blk, :] * _silu(ga_ref[:, cols])).astype(BF16)


def _prompt_bias_tables(rel_bias, n_blk):
    blk = MOBA_BLOCK
    r = jnp.arange(blk, dtype=jnp.int32)
    dist = (jnp.arange(TABLE_DELTAS, dtype=jnp.int32)[:, None, None] * blk
            + r[None, :, None] - r[None, None, :])
    tab = _bias_of_bucket(rel_bias, _rel_bucket(dist), 1)
    tab = jnp.where((dist >= 0)[:, None], tab * LOG2_E, NEG_INF)

    nd = max(n_blk, TABLE_DELTAS + 1)
    off = np.arange(-(blk - 1), blk, dtype=np.int32)
    dist_np = np.arange(nd, dtype=np.int32)[:, None] * blk + off[None, :]
    buckets_np = _rel_bucket_np(dist_np)
    span = buckets_np[TABLE_DELTAS:, -1] - buckets_np[TABLE_DELTAS:, 0]
    assert np.all(np.diff(buckets_np[TABLE_DELTAS:], axis=1) >= 0) and np.all(span <= 1), \
        "more than one bias bucket boundary inside a far block pair"
    buckets = _rel_bucket(jnp.asarray(dist_np))
    b_lo, b_hi = buckets[:, 0], buckets[:, -1]
    first_hi = jnp.argmax(buckets == b_hi[:, None], axis=1).astype(jnp.int32)
    thr = first_hi - (blk - 1)
    blo = (rel_bias.astype(F32)[b_lo] * LOG2_E).reshape(-1)
    bhi = (rel_bias.astype(F32)[b_hi] * LOG2_E).reshape(-1)
    return tab, thr, blo, bhi


def _prompt_attention(z, kmean_pad, rel_bias, seq, k_col0, v_col0, ga_col0):
    blk = MOBA_BLOCK
    n_blk = seq // blk
    assert n_blk <= LANES
    tab, thr, blo, bhi = _prompt_bias_tables(rel_bias, n_blk)
    kc, vc, gc = k_col0 // HEAD_DIM, v_col0 // HEAD_DIM, ga_col0 // GROUP_WIDTH
    smem = pl.BlockSpec(memory_space=pltpu.SMEM)
    grid_spec = pltpu.PrefetchScalarGridSpec(
        num_scalar_prefetch=1,
        grid=(N_KV_HEADS, n_blk),
        in_specs=[
            pl.BlockSpec((blk, GROUP_WIDTH), lambda g, i, th: (i, g)),
            pl.BlockSpec((seq, HEAD_DIM), lambda g, i, th: (0, kc + g)),
            pl.BlockSpec((seq, HEAD_DIM), lambda g, i, th: (0, vc + g)),
            pl.BlockSpec((LANES, HEAD_DIM), lambda g, i, th: (0, g)),
            pl.BlockSpec((blk, GROUP_WIDTH), lambda g, i, th: (i, gc + g)),
            pl.BlockSpec((TABLE_DELTAS, KV_GROUP, blk, blk), lambda g, i, th: (0, g, 0, 0)),
            smem, smem,
        ],
        out_specs=pl.BlockSpec((blk, GROUP_WIDTH), lambda g, i, th: (i, g)),
        scratch_shapes=[
            pltpu.VMEM((seq, HEAD_DIM + LANES), BF16),
            pltpu.VMEM((seq, HEAD_DIM + LANES), BF16),
            pltpu.VMEM((KV_GROUP * blk, HEAD_DIM + LANES), BF16),
            pltpu.VMEM((KV_GROUP * blk, LANES), F32),
            pltpu.VMEM((KV_GROUP * blk, LANES), F32),
            pltpu.VMEM((KV_GROUP * blk, HEAD_DIM), F32),
            pltpu.VMEM((blk, blk), jnp.int32),
        ],
    )
    return pl.pallas_call(
        functools.partial(_prompt_attn_kernel, n_blk_pad=-(-n_blk // 8) * 8),
        grid_spec=grid_spec,
        out_shape=jax.ShapeDtypeStruct((seq, ATTN_WIDTH), BF16),
        compiler_params=pltpu.CompilerParams(
            dimension_semantics=("arbitrary", "arbitrary"), vmem_limit_bytes=VMEM_LIMIT),
    )(thr, z, z, z, kmean_pad, z, tab, blo, bhi)


def _block_of_pages(page_refs):
    return jnp.concatenate(
        [jnp.concatenate([ref[0, pl.ds(gi, PAGE_SIZE, stride=N_KV_HEADS), :] for gi in range(N_KV_HEADS)],
                         axis=1) for ref in page_refs], axis=0)


def _sample_attn_kernel(pt_ref,q_ref, knew_ref, vnew_ref, ga_ref, *rest, n_past, dec_seq, blocks_per_step):
    n_pages = blocks_per_step * PAGES_PER_BLOCK
    k_pages, v_pages = rest[:n_pages], rest[n_pages:2 * n_pages]
    bias_ref, o_ref, qf_ref, qb_ref, gate_ref, mx_ref, l_ref, part_ref = rest[2 * n_pages:]
    t = pl.program_id(1)
    rows = N_HEADS * dec_seq
    grp_rows = KV_GROUP * dec_seq
    lane = lax.broadcasted_iota(jnp.int32, (rows, LANES), 1)

    @pl.when(t == 0)
    def _init():
        gate_ref[...] = jnp.zeros(gate_ref.shape, F32)
        mx_ref[...] = jnp.zeros(mx_ref.shape, F32)
        l_ref[...] = jnp.zeros(l_ref.shape, F32)
        col_grp = lax.broadcasted_iota(jnp.int32, (dec_seq, KV_WIDTH), 1) // HEAD_DIM
        for h in range(N_HEADS):
            qh = q_ref[:, h * HEAD_DIM:(h + 1) * HEAD_DIM]
            wide = jnp.concatenate([qh] * N_KV_HEADS, axis=1)
            wide = jnp.where(col_grp == h // KV_GROUP, wide, 0.0)
            qf_ref[h * dec_seq:(h + 1) * dec_seq, :] = wide
            qb_ref[h * dec_seq:(h + 1) * dec_seq, :] = (wide * ATTN_SCALE).astype(BF16)

    def block_partial(kblk, vblk, bias):
        sc = _dot_nt(qb_ref[...], kblk.astype(BF16)) + bias
        m_n = jnp.max(sc, axis=-1, keepdims=True)
        p = jnp.exp(sc - m_n)
        l_n = jnp.sum(p, axis=-1, keepdims=True)
        o_all = jnp.dot(p.astype(BF16), vblk.astype(BF16), preferred_element_type=F32)
        o_n = jnp.concatenate(
            [o_all[gi * grp_rows:(gi + 1) * grp_rows, gi * HEAD_DIM:(gi + 1) * HEAD_DIM]
             for gi in range(N_KV_HEADS)], axis=0)
        return m_n, l_n, o_n

    @pl.when(t < n_past // blocks_per_step)
    def _past():
        mx, ls, gt = mx_ref[...], l_ref[...], gate_ref[...]
        for jb in range(blocks_per_step):
            n = t * blocks_per_step + jb
            kblk = _block_of_pages(k_pages[jb * PAGES_PER_BLOCK:(jb + 1) * PAGES_PER_BLOCK])
            vblk = _block_of_pages(v_pages[jb * PAGES_PER_BLOCK:(jb + 1) * PAGES_PER_BLOCK])
            kmean = jnp.sum(kblk, axis=0, keepdims=True) * (1.0 / MOBA_BLOCK)
            gate = jnp.sum(qf_ref[...] * kmean, axis=-1, keepdims=True)
            m_n, l_n, o_n = block_partial(kblk, vblk, bias_ref[jb])
            part_ref[n] = o_n
            here = lane == n
            mx = jnp.where(here, m_n, mx)
            ls = jnp.where(here, l_n, ls)
            gt = jnp.where(here, gate, gt)
        mx_ref[...] = mx
        l_ref[...] = ls
        gate_ref[...] = gt

    @pl.when(t == n_past // blocks_per_step)
    def _own():
        pad = jnp.zeros((MOBA_BLOCK - dec_seq, KV_WIDTH), F32)
        m_n, l_n, o_n = block_partial(jnp.concatenate([knew_ref[...], pad], axis=0),
                                      jnp.concatenate([vnew_ref[...], pad], axis=0), bias_ref[0])
        own = lane == n_past
        mx = jnp.where(own, m_n, mx_ref[...])
        ls = jnp.where(own, l_n, l_ref[...])
        sel = _select_top_blocks(gate_ref[...], lane, n_past)
        include = sel | own
        m_tot = jnp.max(jnp.where(include, mx, GATE_MASKED), axis=-1, keepdims=True)
        e = jnp.where(include, jnp.exp(jnp.minimum(mx - m_tot, 0.0)), 0.0)
        denom = jnp.sum(e * ls, axis=-1, keepdims=True)
        num = e[:, n_past:n_past + 1] * o_n
        for j in range(n_past):
            num = num + e[:, j:j + 1] * part_ref[j]
        attn = num / denom
        for h in range(N_HEADS):
            cols = slice(h * HEAD_DIM, (h + 1) * HEAD_DIM)
            o_ref[0, :, cols] = (attn[h * dec_seq:(h + 1) * dec_seq, :] * _silu(ga_ref[:, cols])).astype(BF16)


def _sample_bias_table(rel_bias, past_len, dec_seq, n_blocks):
    kpos = jnp.arange(n_blocks * MOBA_BLOCK, dtype=jnp.int32).reshape(n_blocks, 1, MOBA_BLOCK)
    qpos = (past_len + jnp.arange(dec_seq, dtype=jnp.int32)).reshape(1, dec_seq, 1)
    dist = qpos - kpos
    vals = _bias_of_bucket(rel_bias, _rel_bucket(dist), 1)
    vals = jnp.where((dist >= 0)[:, None], vals, NEG_INF)
    return vals.reshape(n_blocks, N_HEADS * dec_seq, MOBA_BLOCK)


def _sample_attention(q_s, k_s, v_s, ga_s, cache_k, cache_v, page_table, rel_bias, past_len, dec_seq):
    bsz, n_pages = page_table.shape
    n_past = past_len // MOBA_BLOCK
    bps = 4
    assert past_len % MOBA_BLOCK == 0 and n_past + 1 <= LANES and n_pages == n_past * PAGES_PER_BLOCK
    assert n_past % bps == 0
    n_steps = n_past // bps + 1
    pages_per_step = bps * PAGES_PER_BLOCK
    bias = _sample_bias_table(rel_bias, past_len, dec_seq, n_steps * bps)
    rows = N_HEADS * dec_seq

    def page_spec(which):
        def index_map(b, t, pt):
            step = jnp.minimum(t, n_steps - 2)
            return (pt[b * n_pages + step * pages_per_step + which], 0, 0)
        return pl.BlockSpec((1, PAGE_SIZE * N_KV_HEADS, HEAD_DIM), index_map)

    row_spec = lambda width: pl.BlockSpec((dec_seq, width), lambda b, t, pt: (b, 0))
    grid_spec = pltpu.PrefetchScalarGridSpec(
        num_scalar_prefetch=1,
        grid=(bsz, n_steps),
        in_specs=([row_spec(ATTN_WIDTH), row_spec(KV_WIDTH), row_spec(KV_WIDTH), row_spec(ATTN_WIDTH)]
                  + [page_spec(j) for j in range(pages_per_step)]
                  + [page_spec(j) for j in range(pages_per_step)]
                  + [pl.BlockSpec((bps, rows, MOBA_BLOCK), lambda b, t, pt: (t, 0, 0))]),
        out_specs=pl.BlockSpec((1, dec_seq, ATTN_WIDTH), lambda b, t, pt: (b, 0, 0)),
        scratch_shapes=[
            pltpu.VMEM((rows, KV_WIDTH), F32),
            pltpu.VMEM((rows, KV_WIDTH), BF16),
            pltpu.VMEM((rows, LANES), F32),
            pltpu.VMEM((rows, LANES), F32),
            pltpu.VMEM((rows, LANES), F32),
            pltpu.VMEM((n_past, rows, HEAD_DIM), F32),
        ],
    )
    out = pl.pallas_call(
        functools.partial(_sample_attn_kernel, n_past=n_past, dec_seq=dec_seq, blocks_per_step=bps),
        grid_spec=grid_spec,
        out_shape=jax.ShapeDtypeStruct((bsz, dec_seq, ATTN_WIDTH), BF16),
        compiler_params=pltpu.CompilerParams(
            dimension_semantics=("arbitrary", "arbitrary"), vmem_limit_bytes=VMEM_LIMIT),
    )(page_table.reshape(-1), q_s, k_s, v_s, ga_s, *([cache_k] * pages_per_step),
      *([cache_v] * pages_per_step), bias)
    return out.reshape(bsz * dec_seq, ATTN_WIDTH)


def _window_mix(ext_ref, window, rows, cols, count, u, gp, w, scale):
    win = u
    for j in range(1, window):
        win = win + ext_ref[POOL_HALO - j:POOL_HALO - j + rows, cols]
    zed = win / count - u
    y = jnp.dot(zed.astype(BF16), w, preferred_element_type=F32)
    return (y * scale) * _silu(gp)


def _pool_prompt_kernel(prev_ref, u_ref, gp_ref, w_ref, scale_ref, o_ref, ext_ref, *, groups_per_step):
    i = pl.program_id(0)
    rows = u_ref.shape[0]
    gw = w_ref.shape[1]
    ext_ref[0:POOL_HALO, :] = jnp.where(i == 0, 0.0, prev_ref[...])
    ext_ref[POOL_HALO:POOL_HALO + rows, :] = u_ref[...]
    pos = i * rows + lax.broadcasted_iota(jnp.int32, (rows, 1), 0)
    for part in range(N_POOL_GROUPS // groups_per_step):
        @pl.when(pl.program_id(1) == part)
        def _(part=part):
            for gl in range(groups_per_step):
                window = POOL_WINDOWS[part * groups_per_step + gl]
                cols = slice(gl * gw, (gl + 1) * gw)
                count = jnp.minimum(window, pos + 1).astype(F32)
                y = _window_mix(ext_ref, window, rows, cols, count, u_ref[:, cols], gp_ref[:, cols],
                                w_ref[gl], scale_ref[:, cols])
                o_ref[:, cols] = y.astype(BF16)


def _pool_prompt(z, w_pool_bf16, pool_scale, seq, u_col0, gp_col0, tm):
    n_groups, gw, _ = w_pool_bf16.shape
    gps = 2
    bw = gps * gw
    assert n_groups == N_POOL_GROUPS and u_col0 % bw == 0 and gp_col0 % bw == 0 and tm % POOL_HALO == 0
    uc, gc = u_col0 // bw, gp_col0 // bw
    halo_blocks = tm // POOL_HALO
    return pl.pallas_call(
        functools.partial(_pool_prompt_kernel, groups_per_step=gps),
        grid=(seq // tm, n_groups // gps),
        in_specs=[pl.BlockSpec((POOL_HALO, bw), lambda i, c: (jnp.maximum(i * halo_blocks - 1, 0), uc + c)),
                  pl.BlockSpec((tm, bw), lambda i, c: (i, uc + c)),
                  pl.BlockSpec((tm, bw), lambda i, c: (i, gc + c)),
                  pl.BlockSpec((gps, gw, gw), lambda i, c: (c, 0, 0)),
                  pl.BlockSpec((1, bw), lambda i, c: (0, c))],
        out_specs=pl.BlockSpec((tm, bw), lambda i, c: (i, c)),
        out_shape=jax.ShapeDtypeStruct((seq, n_groups * gw), BF16),
        scratch_shapes=[pltpu.VMEM((POOL_HALO + tm, bw), F32)],
        compiler_params=pltpu.CompilerParams(
            dimension_semantics=("parallel", "arbitrary"), vmem_limit_bytes=VMEM_LIMIT),
    )(z, z, z, w_pool_bf16, pool_scale.reshape(1, n_groups * gw))


def _pool_sample_kernel(ext_in_ref, gp_ref, w_ref, scale_ref, o_ref, ext_ref, *, bsz, dec_seq):
    seg = POOL_HALO + dec_seq
    rows = bsz * seg
    gw = w_ref.shape[1]
    ext_ref[0:POOL_HALO, :] = jnp.zeros((POOL_HALO, ext_ref.shape[1]), F32)
    ext_ref[POOL_HALO:POOL_HALO + rows, :] = ext_in_ref[...]
    for gi, window in enumerate(POOL_WINDOWS):
        cols = slice(gi * gw, (gi + 1) * gw)
        u_all = ext_in_ref[:, cols]
        win = u_all
        for j in range(1, window):
            win = win + ext_ref[POOL_HALO - j:POOL_HALO - j + rows, cols]
        zed = (win / float(window) - u_all).reshape(bsz, seg, gw)[:, POOL_HALO:, :].reshape(bsz * dec_seq, gw)
        y = jnp.dot(zed.astype(BF16), w_ref[gi], preferred_element_type=F32)
        o_ref[:, cols] = ((y * scale_ref[:, cols]) * _silu(gp_ref[:, cols])).astype(BF16)


def _pool_sample(u_ext, gp_s, w_pool_bf16, pool_scale, bsz, dec_seq):
    pw = u_ext.shape[1]
    rows = u_ext.shape[0]
    vm = pl.BlockSpec(memory_space=pltpu.VMEM)
    return pl.pallas_call(
        functools.partial(_pool_sample_kernel, bsz=bsz, dec_seq=dec_seq),
        in_specs=[vm, vm, vm, vm],
        out_specs=vm,
        out_shape=jax.ShapeDtypeStruct((bsz * dec_seq, pw), BF16),
        scratch_shapes=[pltpu.VMEM((POOL_HALO + rows, pw), F32)],
        compiler_params=pltpu.CompilerParams(vmem_limit_bytes=VMEM_LIMIT),
    )(u_ext, gp_s, w_pool_bf16, pool_scale.reshape(1, pw))


def _outproj_kernel(ya_ref, yp_ref, w_ref, x_ref, gain_ref, o_ref, *, k_attn):
    k = pl.program_id(1)

    @pl.when(k == 0)
    def _():
        o_ref[...] = x_ref[...]

    @pl.when(k < k_attn)
    def _():
        o_ref[...] += jnp.dot(ya_ref[...], w_ref[...], preferred_element_type=F32)

    @pl.when(k >= k_attn)
    def _():
        o_ref[...] += jnp.dot(yp_ref[...], w_ref[...], preferred_element_type=F32)

    @pl.when(k == pl.num_programs(1) - 1)
    def _():
        h = o_ref[...]
        ms = jnp.mean(h * h, axis=-1, keepdims=True)
        o_ref[...] = h * lax.rsqrt(ms + RMS_EPS) * gain_ref[...]


def _out_projection(ya, yp, w_bf16, x, gain, tm, tk):
    m, d = x.shape
    ka, kp = ya.shape[1] // tk, yp.shape[1] // tk
    return pl.pallas_call(
        functools.partial(_outproj_kernel, k_attn=ka),
        grid=(m // tm, ka + kp),
        in_specs=[pl.BlockSpec((tm, tk), lambda i, k: (i, jnp.minimum(k, ka - 1))),
                  pl.BlockSpec((tm, tk), lambda i, k: (i, jnp.maximum(k - ka, 0))),
                  pl.BlockSpec((tk, d), lambda i, k: (k, 0)),
                  pl.BlockSpec((tm, d), lambda i, k: (i, 0)),
                  pl.BlockSpec((1, d), lambda i, k: (0, 0))],
        out_specs=pl.BlockSpec((tm, d), lambda i, k: (i, 0)),
        out_shape=jax.ShapeDtypeStruct((m, d), F32),
        compiler_params=pltpu.CompilerParams(
            dimension_semantics=("parallel", "arbitrary"), vmem_limit_bytes=VMEM_LIMIT),
    )(ya, yp, w_bf16, x, gain.reshape(1, d))


def kernel(x_prompt, x_sample, cache_k, cache_v, state_pool, page_table, norm_in, w_in, w_pool,
           pool_scale, w_out, rel_bias, norm_out):
    depth = norm_in.shape[0]
    assert depth == 1, "single layer step"
    n_batch, seq, d_model = x_prompt.shape
    bsz, dec_seq, _ = x_sample.shape
    assert n_batch == 1 and seq % (8 * MOBA_BLOCK) == 0 and dec_seq == 8
    past_len = page_table.shape[1] * PAGE_SIZE
    assert past_len >= POOL_HIST
    pool_width = pool_scale.shape[1]
    k0 = ATTN_WIDTH
    v0 = k0 + KV_WIDTH
    ga0 = v0 + KV_WIDTH
    u0 = ga0 + ATTN_WIDTH
    gp0 = u0 + pool_width
    assert w_in.shape[2] == gp0 + pool_width and w_out.shape[1] == ATTN_WIDTH + pool_width

    w_in_b = w_in[0].astype(BF16)
    w_out_b = w_out[0].astype(BF16)
    w_pool_b = w_pool[0].astype(BF16)
    xp = x_prompt[0]
    xs = x_sample.reshape(bsz * dec_seq, d_model)

    z = _in_projection(xp, norm_in[0], w_in_b, tm=512, tn=1024)
    n_blk = seq // MOBA_BLOCK
    kmean = _block_means(z, n_blk, k0)
    kmean_pad = jnp.pad(kmean, ((0, LANES - n_blk), (0, 0)))
    ya = _prompt_attention(z, kmean_pad, rel_bias, seq, k0, v0, ga0)
    yp = _pool_prompt(z, w_pool_b, pool_scale[0], seq, u0, gp0, tm=256)
    y_prompt = _out_projection(ya, yp, w_out_b, xp, norm_out, tm=512, tk=512)

    zs = _in_projection(xs, norm_in[0], w_in_b, tm=bsz * dec_seq, tn=1024)
    q_s, k_s, v_s = zs[:, :k0], zs[:, k0:v0], zs[:, v0:ga0]
    ga_s, u_s, gp_s = zs[:, ga0:u0], zs[:, u0:gp0], zs[:, gp0:]
    n_phys = cache_k.shape[1]
    page_rows = PAGE_SIZE * N_KV_HEADS
    ya_s = _sample_attention(q_s, k_s, v_s, ga_s, cache_k.reshape(n_phys, page_rows, HEAD_DIM),
                             cache_v.reshape(n_phys, page_rows, HEAD_DIM), page_table, rel_bias,
                             past_len, dec_seq)
    u_s3 = u_s.reshape(bsz, dec_seq, pool_width)
    u_ext = jnp.concatenate([jnp.zeros((bsz, 1, pool_width), F32), state_pool[0].astype(F32), u_s3], axis=1)
    yp_s = _pool_sample(u_ext.reshape(bsz * (POOL_HALO + dec_seq), pool_width), gp_s, w_pool_b,
                        pool_scale[0], bsz, dec_seq)
    y_sample = _out_projection(ya_s, yp_s, w_out_b, xs, norm_out, tm=bsz * dec_seq, tk=512)

    k_prompt = z[:, k0:v0].reshape(1, 1, seq, N_KV_HEADS, HEAD_DIM)
    v_prompt = z[:, v0:ga0].reshape(1, 1, seq, N_KV_HEADS, HEAD_DIM)
    pool_prompt = z[seq - POOL_HIST:, u0:gp0].reshape(1, 1, POOL_HIST, pool_width)
    k_sample = k_s.reshape(1, bsz, dec_seq, N_KV_HEADS, HEAD_DIM)
    v_sample = v_s.reshape(1, bsz, dec_seq, N_KV_HEADS, HEAD_DIM)
    pool_sample = u_ext[:, 1 + dec_seq:, :].reshape(1, bsz, POOL_HIST, pool_width).astype(state_pool.dtype)
    return (y_prompt.reshape(1, seq, d_model), y_sample.reshape(bsz, dec_seq, d_model),
            k_prompt, v_prompt, pool_prompt, k_sample, v_sample, pool_sample)
```

```python
import functools
import math

import numpy as np
import jax
import jax.numpy as jnp
from jax import lax
from jax.experimental import pallas as pl
from jax.experimental.pallas import tpu as pltpu

F32 = jnp.float32
BF16 = jnp.bfloat16

HEAD_DIM = 128
N_HEADS = 16
N_KV_HEADS = 4
KV_GROUP = N_HEADS // N_KV_HEADS
ATTN_WIDTH = N_HEADS * HEAD_DIM
KV_WIDTH = N_KV_HEADS * HEAD_DIM
GROUP_WIDTH = KV_GROUP * HEAD_DIM
ATTN_SCALE = HEAD_DIM ** -0.5
LOG2_E = math.log2(math.e)
MOBA_BLOCK = 256
MOBA_TOPK = 3
PAGE_SIZE = 128
PAGES_PER_BLOCK = MOBA_BLOCK // PAGE_SIZE
POOL_WINDOWS = (2, 4, 8, 16)
N_POOL_GROUPS = len(POOL_WINDOWS)
POOL_HIST = max(POOL_WINDOWS) - 1
POOL_HALO = POOL_HIST + 1
NUM_BUCKETS = 32
MAX_EXACT = NUM_BUCKETS // 2
MAX_DISTANCE = 4096
RMS_EPS = 1e-6
NEG_INF = -1e30
GATE_MASKED = -3.0e38
GATE_VALID_MIN = -1.0e38
LANES = 128
TABLE_DELTAS = 3
SWEEP_ROWS = 512
FAR_GROUP_BLOCKS = 2
FAR_GROUPS = 2

VMEM_LIMIT = 56 * 1024 * 1024


def _rel_bucket(dist):
    n = jnp.maximum(dist, 0)
    nf = jnp.maximum(n, 1).astype(F32)
    large = MAX_EXACT + (jnp.log(nf / MAX_EXACT) / math.log(MAX_DISTANCE / MAX_EXACT)
                         * (NUM_BUCKETS - MAX_EXACT)).astype(jnp.int32)
    large = jnp.minimum(large, NUM_BUCKETS - 1)
    return jnp.where(n < MAX_EXACT, n, large)


def _rel_bucket_np(dist):
    n = np.maximum(dist, 0)
    nf = np.maximum(n, 1).astype(np.float32)
    large = MAX_EXACT + (np.log(nf / np.float32(MAX_EXACT)) / np.float32(math.log(MAX_DISTANCE / MAX_EXACT))
                         * np.float32(NUM_BUCKETS - MAX_EXACT)).astype(np.int32)
    large = np.minimum(large, NUM_BUCKETS - 1)
    return np.where(n < MAX_EXACT, n, large)


def _bias_of_bucket(rel_bias, bucket, bucket_axis):
    onehot = (jnp.expand_dims(bucket, bucket_axis)
              == jnp.arange(NUM_BUCKETS, dtype=jnp.int32).reshape(
                  (NUM_BUCKETS,) + (1,) * (bucket.ndim - bucket_axis))).astype(F32)
    out = jnp.tensordot(rel_bias.astype(F32).T, onehot, axes=((1,), (bucket_axis,)),
                        precision=lax.Precision.HIGHEST)
    return jnp.moveaxis(out, 0, bucket_axis)


def _silu(x):
    return x * jax.nn.sigmoid(x)


def _dot_nt(a, b, precision=None):
    return lax.dot_general(a, b, (((1,), (1,)), ((), ())), precision=precision,
                           preferred_element_type=F32)


def _select_top_blocks(gate, block_id, n_past, axis=-1):
    gate = jnp.where(block_id < n_past, gate, GATE_MASKED)
    id_f = block_id.astype(F32)
    sel = jnp.zeros(gate.shape, jnp.bool_)
    for _ in range(MOBA_TOPK):
        mx = jnp.max(gate, axis=axis, keepdims=True)
        first = jnp.min(jnp.where(gate == mx, id_f, float(LANES)), axis=axis, keepdims=True)
        pick = (id_f == first) & (mx > GATE_VALID_MIN)
        sel = sel | pick
        gate = jnp.where(pick, GATE_MASKED, gate)
    return sel


def _inproj_kernel(x_ref, gain_ref, w_ref, o_ref, xn_ref, *, row_chunk):
    @pl.when(pl.program_id(1) == 0)
    def _():
        for r in range(0, x_ref.shape[0], row_chunk):
            x = x_ref[r:r + row_chunk, :]
            ms = jnp.mean(x * x, axis=-1, keepdims=True)
            xn_ref[r:r + row_chunk, :] = (x * lax.rsqrt(ms + RMS_EPS) * gain_ref[...]).astype(BF16)

    o_ref[...] = jnp.dot(xn_ref[...], w_ref[...], preferred_element_type=F32)


def _in_projection(x, gain, w_bf16, tm, tn):
    m, d = x.shape
    n = w_bf16.shape[1]
    return pl.pallas_call(
        functools.partial(_inproj_kernel, row_chunk=min(tm, 128)),
        grid=(m // tm, n // tn),
        in_specs=[pl.BlockSpec((tm, d), lambda i, j: (i, 0)),
                  pl.BlockSpec((1, d), lambda i, j: (0, 0)),
                  pl.BlockSpec((d, tn), lambda i, j: (0, j))],
        out_specs=pl.BlockSpec((tm, tn), lambda i, j: (i, j)),
        out_shape=jax.ShapeDtypeStruct((m, n), F32),
        scratch_shapes=[pltpu.VMEM((tm, d), BF16)],
        compiler_params=pltpu.CompilerParams(
            dimension_semantics=("parallel", "arbitrary"), vmem_limit_bytes=VMEM_LIMIT),
    )(x, gain.reshape(1, d), w_bf16)


def _kmean_kernel(k_ref, o_ref, *, blocks_per_step):
    k = k_ref[...].reshape(blocks_per_step, MOBA_BLOCK, KV_WIDTH)
    o_ref[...] = jnp.sum(k, axis=1) * (1.0 / MOBA_BLOCK)


def _block_means(z, n_blk, k_col0):
    bps = 8
    assert n_blk % bps == 0 and k_col0 % KV_WIDTH == 0
    return pl.pallas_call(
        functools.partial(_kmean_kernel, blocks_per_step=bps),
        grid=(n_blk // bps,),
        in_specs=[pl.BlockSpec((bps * MOBA_BLOCK, KV_WIDTH), lambda i: (i, k_col0 // KV_WIDTH))],
        out_specs=pl.BlockSpec((bps, KV_WIDTH), lambda i: (i, 0)),
        out_shape=jax.ShapeDtypeStruct((n_blk, KV_WIDTH), F32),
        compiler_params=pltpu.CompilerParams(vmem_limit_bytes=VMEM_LIMIT),
    )(z)


def _prompt_attn_kernel(thr_ref,
                        q_ref, k_ref, v_ref, km_ref, ga_ref, tab_ref, blo_ref, bhi_ref,
                        o_ref, kaug_ref, vaug_ref, qaug_ref, m_ref, l_ref, acc_ref, dmat_ref,
                        *, n_blk_pad):
    g = pl.program_id(0)
    i = pl.program_id(1)
    blk = MOBA_BLOCK
    all_rows = KV_GROUP * blk
    n_blk = k_ref.shape[0] // blk

    @pl.when(i == 0)
    def _stage_keys_values():
        lane = lax.broadcasted_iota(jnp.int32, (blk, LANES), 1)
        ones = jnp.ones((blk, LANES), BF16)

        def stage(n, carry):
            r = pl.multiple_of(n * blk, blk)
            kaug_ref[pl.ds(r, blk), 0:HEAD_DIM] = k_ref[pl.ds(r, blk), :].astype(BF16)
            kaug_ref[pl.ds(r, blk), HEAD_DIM:HEAD_DIM + LANES] = jnp.where(lane == n, 1.0, 0.0).astype(BF16)
            vaug_ref[pl.ds(r, blk), 0:HEAD_DIM] = v_ref[pl.ds(r, blk), :].astype(BF16)
            vaug_ref[pl.ds(r, blk), HEAD_DIM:HEAD_DIM + LANES] = ones
            return carry

        lax.fori_loop(0, n_blk, stage, 0)
        dmat_ref[...] = (lax.broadcasted_iota(jnp.int32, (blk, blk), 0)
                         - lax.broadcasted_iota(jnp.int32, (blk, blk), 1))

    m_ref[...] = jnp.full(m_ref.shape, NEG_INF, F32)
    l_ref[...] = jnp.zeros(l_ref.shape, F32)
    acc_ref[...] = jnp.zeros(acc_ref.shape, F32)
    q_all = jnp.concatenate(
        [q_ref[:, hh * HEAD_DIM:(hh + 1) * HEAD_DIM] for hh in range(KV_GROUP)], axis=0)
    gate_t = _dot_nt(km_ref[0:n_blk_pad, :], q_all, precision=lax.Precision.HIGHEST)
    blk_id = lax.broadcasted_iota(jnp.int32, (n_blk_pad, all_rows), 0)
    sel_t = _select_top_blocks(gate_t, blk_id, i, axis=0)
    mask_t = jnp.where(sel_t | (blk_id == i), 0.0, NEG_INF)
    if n_blk_pad < LANES:
        mask_t = jnp.concatenate([mask_t, jnp.zeros((LANES - n_blk_pad, all_rows), F32)], axis=0)
    qaug_ref[:, 0:HEAD_DIM] = (q_all * (ATTN_SCALE * LOG2_E)).astype(BF16)
    qaug_ref[:, HEAD_DIM:HEAD_DIM + LANES] = mask_t.T.astype(BF16)

    def sweep(groups):
        m_all, l_all, acc_all = m_ref[...], l_ref[...], acc_ref[...]
        chunks = [slice(c, c + SWEEP_ROWS) for c in range(0, all_rows, SWEEP_ROWS)]
        m_c = [m_all[rows] for rows in chunks]
        l_c = [l_all[rows] for rows in chunks]
        acc_c = [acc_all[rows] for rows in chunks]
        logits = []
        for n0, nb, bias in groups:
            r = pl.multiple_of(n0 * blk, blk)
            keys = kaug_ref[pl.ds(r, nb * blk), :]
            logits.append([_dot_nt(qaug_ref[rows, :], keys) + bias[rows] for rows in chunks])
        for (n0, nb, bias), group_logits in zip(groups, logits):
            r = pl.multiple_of(n0 * blk, blk)
            values = vaug_ref[pl.ds(r, nb * blk), :]
            for ci, rows in enumerate(chunks):
                sc = group_logits[ci]
                m_new = jnp.maximum(m_c[ci], jnp.max(sc, axis=-1, keepdims=True))
                alpha = jnp.exp2(m_c[ci] - m_new)
                p = jnp.exp2(sc - jnp.concatenate([m_new] * (nb * blk // LANES), axis=1))
                pv = jnp.dot(p.astype(BF16), values, preferred_element_type=F32)
                l_c[ci] = alpha * l_c[ci] + pv[:, HEAD_DIM:]
                acc_c[ci] = alpha * acc_c[ci] + pv[:, :HEAD_DIM]
                m_c[ci] = m_new
        m_ref[...] = jnp.concatenate(m_c, axis=0)
        l_ref[...] = jnp.concatenate(l_c, axis=0)
        acc_ref[...] = jnp.concatenate(acc_c, axis=0)

    def far_bias(n0, nb):
        tiles = []
        for j in range(nb):
            delta = i - (n0 + j)
            high = dmat_ref[...] >= thr_ref[delta]
            base = delta * N_HEADS + g * KV_GROUP
            tiles.append(jnp.concatenate([jnp.where(high, bhi_ref[base + hh], blo_ref[base + hh])
                                          for hh in range(KV_GROUP)], axis=0))
        return jnp.concatenate(tiles, axis=1)

    n_far = jnp.maximum(i - (TABLE_DELTAS - 1), 0)
    per_iter = FAR_GROUPS * FAR_GROUP_BLOCKS

    def far_step(t, carry):
        first = t * per_iter
        sweep([(first + j * FAR_GROUP_BLOCKS, FAR_GROUP_BLOCKS,
                far_bias(first + j * FAR_GROUP_BLOCKS, FAR_GROUP_BLOCKS)) for j in range(FAR_GROUPS)])
        return carry

    lax.fori_loop(0, n_far // per_iter, far_step, 0)

    left = n_far % per_iter
    near = jnp.minimum(i + 1, TABLE_DELTAS)
    for count in range(1, TABLE_DELTAS + 1):
        for rem in range(per_iter if count == TABLE_DELTAS else 1):
            @pl.when((near == count) & (left == rem))
            def _tail(count=count, rem=rem):
                groups = []
                for first in range(0, rem, FAR_GROUP_BLOCKS):
                    nb = min(FAR_GROUP_BLOCKS, rem - first)
                    groups.append((n_far - rem + first, nb, far_bias(n_far - rem + first, nb)))
                groups.append((i - (count - 1), count, jnp.concatenate(
                    [tab_ref[count - 1 - j].reshape(all_rows, blk) for j in range(count)], axis=1)))
                sweep(groups)


    attn = acc_ref[...] / l_ref[...]
    for hh in range(KV_GROUP):
        cols = slice(hh * HEAD_DIM, (hh + 1) * HEAD_DIM)
        o_ref[:, cols] = (attn[hh * blk:(hh + 1) * blk, :] * _silu(ga_ref[:, cols])).astype(BF16)


def _prompt_bias_tables(rel_bias, n_blk):
    blk = MOBA_BLOCK
    r = jnp.arange(blk, dtype=jnp.int32)
    dist = (jnp.arange(TABLE_DELTAS, dtype=jnp.int32)[:, None, None] * blk
            + r[None, :, None] - r[None, None, :])
    tab = _bias_of_bucket(rel_bias, _rel_bucket(dist), 1)
    tab = jnp.where((dist >= 0)[:, None], tab * LOG2_E, NEG_INF)

    nd = max(n_blk, TABLE_DELTAS + 1)
    off = np.arange(-(blk - 1), blk, dtype=np.int32)
    dist_np = np.arange(nd, dtype=np.int32)[:, None] * blk + off[None, :]
    buckets_np = _rel_bucket_np(dist_np)
    span = buckets_np[TABLE_DELTAS:, -1] - buckets_np[TABLE_DELTAS:, 0]
    assert np.all(np.diff(buckets_np[TABLE_DELTAS:], axis=1) >= 0) and np.all(span <= 1), \
        "more than one bias bucket boundary inside a far block pair"
    buckets = _rel_bucket(jnp.asarray(dist_np))
    b_lo, b_hi = buckets[:, 0], buckets[:, -1]
    first_hi = jnp.argmax(buckets == b_hi[:, None], axis=1).astype(jnp.int32)
    thr = first_hi - (blk - 1)
    blo = (rel_bias.astype(F32)[b_lo] * LOG2_E).reshape(-1)
    bhi = (rel_bias.astype(F32)[b_hi] * LOG2_E).reshape(-1)
    return tab, thr, blo, bhi


def _prompt_attention(z, kmean_pad, rel_bias, seq, k_col0, v_col0, ga_col0):
    blk = MOBA_BLOCK
    n_blk = seq // blk
    assert n_blk <= LANES
    tab, thr, blo, bhi = _prompt_bias_tables(rel_bias, n_blk)
    kc, vc, gc = k_col0 // HEAD_DIM, v_col0 // HEAD_DIM, ga_col0 // GROUP_WIDTH
    smem = pl.BlockSpec(memory_space=pltpu.SMEM)
    grid_spec = pltpu.PrefetchScalarGridSpec(
        num_scalar_prefetch=1,
        grid=(N_KV_HEADS, n_blk),
        in_specs=[
            pl.BlockSpec((blk, GROUP_WIDTH), lambda g, i, th: (i, g)),
            pl.BlockSpec((seq, HEAD_DIM), lambda g, i, th: (0, kc + g)),
            pl.BlockSpec((seq, HEAD_DIM), lambda g, i, th: (0, vc + g)),
            pl.BlockSpec((LANES, HEAD_DIM), lambda g, i, th: (0, g)),
            pl.BlockSpec((blk, GROUP_WIDTH), lambda g, i, th: (i, gc + g)),
            pl.BlockSpec((TABLE_DELTAS, KV_GROUP, blk, blk), lambda g, i, th: (0, g, 0, 0)),
            smem, smem,
        ],
        out_specs=pl.BlockSpec((blk, GROUP_WIDTH), lambda g, i, th: (i, g)),
        scratch_shapes=[
            pltpu.VMEM((seq, HEAD_DIM + LANES), BF16),
            pltpu.VMEM((seq, HEAD_DIM + LANES), BF16),
            pltpu.VMEM((KV_GROUP * blk, HEAD_DIM + LANES), BF16),
            pltpu.VMEM((KV_GROUP * blk, LANES), F32),
            pltpu.VMEM((KV_GROUP * blk, LANES), F32),
            pltpu.VMEM((KV_GROUP * blk, HEAD_DIM), F32),
            pltpu.VMEM((blk, blk), jnp.int32),
        ],
    )
    return pl.pallas_call(
        functools.partial(_prompt_attn_kernel, n_blk_pad=-(-n_blk // 8) * 8),
        grid_spec=grid_spec,
        out_shape=jax.ShapeDtypeStruct((seq, ATTN_WIDTH), BF16),
        compiler_params=pltpu.CompilerParams(
            dimension_semantics=("arbitrary", "arbitrary"), vmem_limit_bytes=VMEM_LIMIT),
    )(thr, z, z, z, kmean_pad, z, tab, blo, bhi)


def _block_of_pages(page_refs):
    return jnp.concatenate(
        [jnp.concatenate([ref[0, pl.ds(gi, PAGE_SIZE, stride=N_KV_HEADS), :] for gi in range(N_KV_HEADS)],
                         axis=1) for ref in page_refs], axis=0)


def _sample_attn_kernel(pt_ref,q_ref, knew_ref, vnew_ref, ga_ref, *rest, n_past, dec_seq, blocks_per_step):
    n_pages = blocks_per_step * PAGES_PER_BLOCK
    k_pages, v_pages = rest[:n_pages], rest[n_pages:2 * n_pages]
    bias_ref, o_ref, qf_ref, qb_ref, gate_ref, mx_ref, l_ref, part_ref = rest[2 * n_pages:]
    t = pl.program_id(1)
    rows = N_HEADS * dec_seq
    grp_rows = KV_GROUP * dec_seq
    lane = lax.broadcasted_iota(jnp.int32, (rows, LANES), 1)

    @pl.when(t == 0)
    def _init():
        gate_ref[...] = jnp.zeros(gate_ref.shape, F32)
        mx_ref[...] = jnp.zeros(mx_ref.shape, F32)
        l_ref[...] = jnp.zeros(l_ref.shape, F32)
        col_grp = lax.broadcasted_iota(jnp.int32, (dec_seq, KV_WIDTH), 1) // HEAD_DIM
        for h in range(N_HEADS):
            qh = q_ref[:, h * HEAD_DIM:(h + 1) * HEAD_DIM]
            wide = jnp.concatenate([qh] * N_KV_HEADS, axis=1)
            wide = jnp.where(col_grp == h // KV_GROUP, wide, 0.0)
            qf_ref[h * dec_seq:(h + 1) * dec_seq, :] = wide
            qb_ref[h * dec_seq:(h + 1) * dec_seq, :] = (wide * ATTN_SCALE).astype(BF16)

    def block_partials(kblks, vblks, biases):
        scs = [_dot_nt(qb_ref[...], kblk.astype(BF16)) + bias for kblk, bias in zip(kblks, biases)]
        ms = [jnp.max(sc, axis=-1, keepdims=True) for sc in scs]
        ps = [jnp.exp(sc - m_n) for sc, m_n in zip(scs, ms)]
        ls = [jnp.sum(p, axis=-1, keepdims=True) for p in ps]
        outs = []
        for p, vblk in zip(ps, vblks):
            o_all = jnp.dot(p.astype(BF16), vblk.astype(BF16), preferred_element_type=F32)
            outs.append(jnp.concatenate(
                [o_all[gi * grp_rows:(gi + 1) * grp_rows, gi * HEAD_DIM:(gi + 1) * HEAD_DIM]
                 for gi in range(N_KV_HEADS)], axis=0))
        return ms, ls, outs

    @pl.when(t < n_past // blocks_per_step)
    def _past():
        kblks = [_block_of_pages(k_pages[jb * PAGES_PER_BLOCK:(jb + 1) * PAGES_PER_BLOCK])
                 for jb in range(blocks_per_step)]
        vblks = [_block_of_pages(v_pages[jb * PAGES_PER_BLOCK:(jb + 1) * PAGES_PER_BLOCK])
                 for jb in range(blocks_per_step)]
        ms, lsums, outs = block_partials(kblks, vblks, [bias_ref[jb] for jb in range(blocks_per_step)])
        mx, ls, gt = mx_ref[...], l_ref[...], gate_ref[...]
        for jb in range(blocks_per_step):
            n = t * blocks_per_step + jb
            kmean = jnp.sum(kblks[jb], axis=0, keepdims=True) * (1.0 / MOBA_BLOCK)
            gate = jnp.sum(qf_ref[...] * kmean, axis=-1, keepdims=True)
            part_ref[n] = outs[jb]
            here = lane == n
            mx = jnp.where(here, ms[jb], mx)
            ls = jnp.where(here, lsums[jb], ls)
            gt = jnp.where(here, gate, gt)
        mx_ref[...] = mx
        l_ref[...] = ls
        gate_ref[...] = gt

    @pl.when(t == n_past // blocks_per_step)
    def _own():
        pad = jnp.zeros((MOBA_BLOCK - dec_seq, KV_WIDTH), F32)
        (m_n,), (l_n,), (o_n,) = block_partials([jnp.concatenate([knew_ref[...], pad], axis=0)],
                                                [jnp.concatenate([vnew_ref[...], pad], axis=0)], [bias_ref[0]])
        own = lane == n_past
        mx = jnp.where(own, m_n, mx_ref[...])
        ls = jnp.where(own, l_n, l_ref[...])
        sel = _select_top_blocks(gate_ref[...], lane, n_past)
        include = sel | own
        m_tot = jnp.max(jnp.where(include, mx, GATE_MASKED), axis=-1, keepdims=True)
        e = jnp.where(include, jnp.exp(jnp.minimum(mx - m_tot, 0.0)), 0.0)
        denom = jnp.sum(e * ls, axis=-1, keepdims=True)
        num = e[:, n_past:n_past + 1] * o_n
        for j in range(n_past):
            num = num + e[:, j:j + 1] * part_ref[j]
        attn = num / denom
        for h in range(N_HEADS):
            cols = slice(h * HEAD_DIM, (h + 1) * HEAD_DIM)
            o_ref[0, :, cols] = (attn[h * dec_seq:(h + 1) * dec_seq, :] * _silu(ga_ref[:, cols])).astype(BF16)


def _sample_bias_table(rel_bias, past_len, dec_seq, n_blocks):
    kpos = jnp.arange(n_blocks * MOBA_BLOCK, dtype=jnp.int32).reshape(n_blocks, 1, MOBA_BLOCK)
    qpos = (past_len + jnp.arange(dec_seq, dtype=jnp.int32)).reshape(1, dec_seq, 1)
    dist = qpos - kpos
    vals = _bias_of_bucket(rel_bias, _rel_bucket(dist), 1)
    vals = jnp.where((dist >= 0)[:, None], vals, NEG_INF)
    return vals.reshape(n_blocks, N_HEADS * dec_seq, MOBA_BLOCK)


def _sample_attention(q_s, k_s, v_s, ga_s, cache_k, cache_v, page_table, rel_bias, past_len, dec_seq):
    bsz, n_pages = page_table.shape
    n_past = past_len // MOBA_BLOCK
    bps = 4
    assert past_len % MOBA_BLOCK == 0 and n_past + 1 <= LANES and n_pages == n_past * PAGES_PER_BLOCK
    assert n_past % bps == 0
    n_steps = n_past // bps + 1
    pages_per_step = bps * PAGES_PER_BLOCK
    bias = _sample_bias_table(rel_bias, past_len, dec_seq, n_steps * bps)
    rows = N_HEADS * dec_seq

    def page_spec(which):
        def index_map(b, t, pt):
            step = jnp.minimum(t, n_steps - 2)
            return (pt[b * n_pages + step * pages_per_step + which], 0, 0)
        return pl.BlockSpec((1, PAGE_SIZE * N_KV_HEADS, HEAD_DIM), index_map)

    row_spec = lambda width: pl.BlockSpec((dec_seq, width), lambda b, t, pt: (b, 0))
    grid_spec = pltpu.PrefetchScalarGridSpec(
        num_scalar_prefetch=1,
        grid=(bsz, n_steps),
        in_specs=([row_spec(ATTN_WIDTH), row_spec(KV_WIDTH), row_spec(KV_WIDTH), row_spec(ATTN_WIDTH)]
                  + [page_spec(j) for j in range(pages_per_step)]
                  + [page_spec(j) for j in range(pages_per_step)]
                  + [pl.BlockSpec((bps, rows, MOBA_BLOCK), lambda b, t, pt: (t, 0, 0))]),
        out_specs=pl.BlockSpec((1, dec_seq, ATTN_WIDTH), lambda b, t, pt: (b, 0, 0)),
        scratch_shapes=[
            pltpu.VMEM((rows, KV_WIDTH), F32),
            pltpu.VMEM((rows, KV_WIDTH), BF16),
            pltpu.VMEM((rows, LANES), F32),
            pltpu.VMEM((rows, LANES), F32),
            pltpu.VMEM((rows, LANES), F32),
            pltpu.VMEM((n_past, rows, HEAD_DIM), F32),
        ],
    )
    out = pl.pallas_call(
        functools.partial(_sample_attn_kernel, n_past=n_past, dec_seq=dec_seq, blocks_per_step=bps),
        grid_spec=grid_spec,
        out_shape=jax.ShapeDtypeStruct((bsz, dec_seq, ATTN_WIDTH), BF16),
        compiler_params=pltpu.CompilerParams(
            dimension_semantics=("arbitrary", "arbitrary"), vmem_limit_bytes=VMEM_LIMIT),
    )(page_table.reshape(-1), q_s, k_s, v_s, ga_s, *([cache_k] * pages_per_step),
      *([cache_v] * pages_per_step), bias)
    return out.reshape(bsz * dec_seq, ATTN_WIDTH)


def _window_mix(ext_ref, window, rows, cols, count, u, gp, w, scale):
    win = u
    for j in range(1, window):
        win = win + ext_ref[POOL_HALO - j:POOL_HALO - j + rows, cols]
    zed = win / count - u
    y = jnp.dot(zed.astype(BF16), w, preferred_element_type=F32)
    return (y * scale) * _silu(gp)


def _pool_prompt_kernel(prev_ref, u_ref, gp_ref, w_ref, scale_ref, o_ref, ext_ref, *, groups_per_step):
    i = pl.program_id(0)
    rows = u_ref.shape[0]
    gw = w_ref.shape[1]
    ext_ref[0:POOL_HALO, :] = jnp.where(i == 0, 0.0, prev_ref[...])
    ext_ref[POOL_HALO:POOL_HALO + rows, :] = u_ref[...]
    pos = i * rows + lax.broadcasted_iota(jnp.int32, (rows, 1), 0)
    for part in range(N_POOL_GROUPS // groups_per_step):
        @pl.when(pl.program_id(1) == part)
        def _(part=part):
            for gl in range(groups_per_step):
                window = POOL_WINDOWS[part * groups_per_step + gl]
                cols = slice(gl * gw, (gl + 1) * gw)
                count = jnp.minimum(window, pos + 1).astype(F32)
                y = _window_mix(ext_ref, window, rows, cols, count, u_ref[:, cols], gp_ref[:, cols],
                                w_ref[gl], scale_ref[:, cols])
                o_ref[:, cols] = y.astype(BF16)


def _pool_prompt(z, w_pool_bf16, pool_scale, seq, u_col0, gp_col0, tm):
    n_groups, gw, _ = w_pool_bf16.shape
    gps = 2
    bw = gps * gw
    assert n_groups == N_POOL_GROUPS and u_col0 % bw == 0 and gp_col0 % bw == 0 and tm % POOL_HALO == 0
    uc, gc = u_col0 // bw, gp_col0 // bw
    halo_blocks = tm // POOL_HALO
    return pl.pallas_call(
        functools.partial(_pool_prompt_kernel, groups_per_step=gps),
        grid=(seq // tm, n_groups // gps),
        in_specs=[pl.BlockSpec((POOL_HALO, bw), lambda i, c: (jnp.maximum(i * halo_blocks - 1, 0), uc + c)),
                  pl.BlockSpec((tm, bw), lambda i, c: (i, uc + c)),
                  pl.BlockSpec((tm, bw), lambda i, c: (i, gc + c)),
                  pl.BlockSpec((gps, gw, gw), lambda i, c: (c, 0, 0)),
                  pl.BlockSpec((1, bw), lambda i, c: (0, c))],
        out_specs=pl.BlockSpec((tm, bw), lambda i, c: (i, c)),
        out_shape=jax.ShapeDtypeStruct((seq, n_groups * gw), BF16),
        scratch_shapes=[pltpu.VMEM((POOL_HALO + tm, bw), F32)],
        compiler_params=pltpu.CompilerParams(
            dimension_semantics=("parallel", "arbitrary"), vmem_limit_bytes=VMEM_LIMIT),
    )(z, z, z, w_pool_bf16, pool_scale.reshape(1, n_groups * gw))


def _pool_sample_kernel(ext_in_ref, gp_ref, w_ref, scale_ref, o_ref, ext_ref, *, bsz, dec_seq):
    seg = POOL_HALO + dec_seq
    rows = bsz * seg
    gw = w_ref.shape[1]
    ext_ref[0:POOL_HALO, :] = jnp.zeros((POOL_HALO, ext_ref.shape[1]), F32)
    ext_ref[POOL_HALO:POOL_HALO + rows, :] = ext_in_ref[...]
    for gi, window in enumerate(POOL_WINDOWS):
        cols = slice(gi * gw, (gi + 1) * gw)
        u_all = ext_in_ref[:, cols]
        win = u_all
        for j in range(1, window):
            win = win + ext_ref[POOL_HALO - j:POOL_HALO - j + rows, cols]
        zed = (win / float(window) - u_all).reshape(bsz, seg, gw)[:, POOL_HALO:, :].reshape(bsz * dec_seq, gw)
        y = jnp.dot(zed.astype(BF16), w_ref[gi], preferred_element_type=F32)
        o_ref[:, cols] = ((y * scale_ref[:, cols]) * _silu(gp_ref[:, cols])).astype(BF16)


def _pool_sample(u_ext, gp_s, w_pool_bf16, pool_scale, bsz, dec_seq):
    pw = u_ext.shape[1]
    rows = u_ext.shape[0]
    vm = pl.BlockSpec(memory_space=pltpu.VMEM)
    return pl.pallas_call(
        functools.partial(_pool_sample_kernel, bsz=bsz, dec_seq=dec_seq),
        in_specs=[vm, vm, vm, vm],
        out_specs=vm,
        out_shape=jax.ShapeDtypeStruct((bsz * dec_seq, pw), BF16),
        scratch_shapes=[pltpu.VMEM((POOL_HALO + rows, pw), F32)],
        compiler_params=pltpu.CompilerParams(vmem_limit_bytes=VMEM_LIMIT),
    )(u_ext, gp_s, w_pool_bf16, pool_scale.reshape(1, pw))


def _outproj_kernel(ya_ref, yp_ref, w_ref, x_ref, gain_ref, o_ref, *, k_attn):
    k = pl.program_id(1)

    @pl.when(k == 0)
    def _():
        o_ref[...] = x_ref[...]

    @pl.when(k < k_attn)
    def _():
        o_ref[...] += jnp.dot(ya_ref[...], w_ref[...], preferred_element_type=F32)

    @pl.when(k >= k_attn)
    def _():
        o_ref[...] += jnp.dot(yp_ref[...], w_ref[...], preferred_element_type=F32)

    @pl.when(k == pl.num_programs(1) - 1)
    def _():
        h = o_ref[...]
        ms = jnp.mean(h * h, axis=-1, keepdims=True)
        o_ref[...] = h * lax.rsqrt(ms + RMS_EPS) * gain_ref[...]


def _out_projection(ya, yp, w_bf16, x, gain, tm, tk):
    m, d = x.shape
    ka, kp = ya.shape[1] // tk, yp.shape[1] // tk
    return pl.pallas_call(
        functools.partial(_outproj_kernel, k_attn=ka),
        grid=(m // tm, ka + kp),
        in_specs=[pl.BlockSpec((tm, tk), lambda i, k: (i, jnp.minimum(k, ka - 1))),
                  pl.BlockSpec((tm, tk), lambda i, k: (i, jnp.maximum(k - ka, 0))),
                  pl.BlockSpec((tk, d), lambda i, k: (k, 0)),
                  pl.BlockSpec((tm, d), lambda i, k: (i, 0)),
                  pl.BlockSpec((1, d), lambda i, k: (0, 0))],
        out_specs=pl.BlockSpec((tm, d), lambda i, k: (i, 0)),
        out_shape=jax.ShapeDtypeStruct((m, d), F32),
        compiler_params=pltpu.CompilerParams(
            dimension_semantics=("parallel", "arbitrary"), vmem_limit_bytes=VMEM_LIMIT),
    )(ya, yp, w_bf16, x, gain.reshape(1, d))


def kernel(x_prompt, x_sample, cache_k, cache_v, state_pool, page_table, norm_in, w_in, w_pool,
           pool_scale, w_out, rel_bias, norm_out):
    depth = norm_in.shape[0]
    assert depth == 1, "single layer step"
    n_batch, seq, d_model = x_prompt.shape
    bsz, dec_seq, _ = x_sample.shape
    assert n_batch == 1 and seq % (8 * MOBA_BLOCK) == 0 and dec_seq == 8
    past_len = page_table.shape[1] * PAGE_SIZE
    assert past_len >= POOL_HIST
    pool_width = pool_scale.shape[1]
    k0 = ATTN_WIDTH
    v0 = k0 + KV_WIDTH
    ga0 = v0 + KV_WIDTH
    u0 = ga0 + ATTN_WIDTH
    gp0 = u0 + pool_width
    assert w_in.shape[2] == gp0 + pool_width and w_out.shape[1] == ATTN_WIDTH + pool_width

    w_in_b = w_in[0].astype(BF16)
    w_out_b = w_out[0].astype(BF16)
    w_pool_b = w_pool[0].astype(BF16)
    xp = x_prompt[0]
    xs = x_sample.reshape(bsz * dec_seq, d_model)

    z = _in_projection(xp, norm_in[0], w_in_b, tm=512, tn=1024)
    n_blk = seq // MOBA_BLOCK
    kmean = _block_means(z, n_blk, k0)
    kmean_pad = jnp.pad(kmean, ((0, LANES - n_blk), (0, 0)))
    ya = _prompt_attention(z, kmean_pad, rel_bias, seq, k0, v0, ga0)
    yp = _pool_prompt(z, w_pool_b, pool_scale[0], seq, u0, gp0, tm=256)
    y_prompt = _out_projection(ya, yp, w_out_b, xp, norm_out, tm=512, tk=512)

    zs = _in_projection(xs, norm_in[0], w_in_b, tm=bsz * dec_seq, tn=1024)
    q_s, k_s, v_s = zs[:, :k0], zs[:, k0:v0], zs[:, v0:ga0]
    ga_s, u_s, gp_s = zs[:, ga0:u0], zs[:, u0:gp0], zs[:, gp0:]
    n_phys = cache_k.shape[1]
    page_rows = PAGE_SIZE * N_KV_HEADS
    ya_s = _sample_attention(q_s, k_s, v_s, ga_s, cache_k.reshape(n_phys, page_rows, HEAD_DIM),
                             cache_v.reshape(n_phys, page_rows, HEAD_DIM), page_table, rel_bias,
                             past_len, dec_seq)
    u_s3 = u_s.reshape(bsz, dec_seq, pool_width)
    u_ext = jnp.concatenate([jnp.zeros((bsz, 1, pool_width), F32), state_pool[0].astype(F32), u_s3], axis=1)
    yp_s = _pool_sample(u_ext.reshape(bsz * (POOL_HALO + dec_seq), pool_width), gp_s, w_pool_b,
                        pool_scale[0], bsz, dec_seq)
    y_sample = _out_projection(ya_s, yp_s, w_out_b, xs, norm_out, tm=bsz * dec_seq, tk=512)

    k_prompt = z[:, k0:v0].reshape(1, 1, seq, N_KV_HEADS, HEAD_DIM)
    v_prompt = z[:, v0:ga0].reshape(1, 1, seq, N_KV_HEADS, HEAD_DIM)
    pool_prompt = z[seq - POOL_HIST:, u0:gp0].reshape(1, 1, POOL_HIST, pool_width)
    k_sample = k_s.reshape(1, bsz, dec_seq, N_KV_HEADS, HEAD_DIM)
    v_sample = v_s.reshape(1, bsz, dec_seq, N_KV_HEADS, HEAD_DIM)
    pool_sample = u_ext[:, 1 + dec_seq:, :].reshape(1, bsz, POOL_HIST, pool_width).astype(state_pool.dtype)
    return (y_prompt.reshape(1, seq, d_model), y_sample.reshape(bsz, dec_seq, d_model),
            k_prompt, v_prompt, pool_prompt, k_sample, v_sample, pool_sample)
```

```python
import functools
import math

import numpy as np
import jax
import jax.numpy as jnp
from jax import lax
from jax.experimental import pallas as pl
from jax.experimental.pallas import tpu as pltpu

F32 = jnp.float32
BF16 = jnp.bfloat16

HEAD_DIM = 128
N_HEADS = 16
N_KV_HEADS = 4
KV_GROUP = N_HEADS // N_KV_HEADS
ATTN_WIDTH = N_HEADS * HEAD_DIM
KV_WIDTH = N_KV_HEADS * HEAD_DIM
GROUP_WIDTH = KV_GROUP * HEAD_DIM
ATTN_SCALE = HEAD_DIM ** -0.5
LOG2_E = math.log2(math.e)
MOBA_BLOCK = 256
MOBA_TOPK = 3
PAGE_SIZE = 128
PAGES_PER_BLOCK = MOBA_BLOCK // PAGE_SIZE
POOL_WINDOWS = (2, 4, 8, 16)
N_POOL_GROUPS = len(POOL_WINDOWS)
POOL_HIST = max(POOL_WINDOWS) - 1
POOL_HALO = POOL_HIST + 1
NUM_BUCKETS = 32
MAX_EXACT = NUM_BUCKETS // 2
MAX_DISTANCE = 4096
RMS_EPS = 1e-6
NEG_INF = -1e30
GATE_MASKED = -3.0e38
GATE_VALID_MIN = -1.0e38
LANES = 128
TABLE_DELTAS = 3
SWEEP_ROWS = 512
FAR_GROUP_BLOCKS = 2
FAR_GROUPS = 2

VMEM_LIMIT = 56 * 1024 * 1024


def _rel_bucket(dist):
    n = jnp.maximum(dist, 0)
    nf = jnp.maximum(n, 1).astype(F32)
    large = MAX_EXACT + (jnp.log(nf / MAX_EXACT) / math.log(MAX_DISTANCE / MAX_EXACT)
                         * (NUM_BUCKETS - MAX_EXACT)).astype(jnp.int32)
    large = jnp.minimum(large, NUM_BUCKETS - 1)
    return jnp.where(n < MAX_EXACT, n, large)


def _rel_bucket_np(dist):
    n = np.maximum(dist, 0)
    nf = np.maximum(n, 1).astype(np.float32)
    large = MAX_EXACT + (np.log(nf / np.float32(MAX_EXACT)) / np.float32(math.log(MAX_DISTANCE / MAX_EXACT))
                         * np.float32(NUM_BUCKETS - MAX_EXACT)).astype(np.int32)
    large = np.minimum(large, NUM_BUCKETS - 1)
    return np.where(n < MAX_EXACT, n, large)


def _bias_of_bucket(rel_bias, bucket, bucket_axis):
    onehot = (jnp.expand_dims(bucket, bucket_axis)
              == jnp.arange(NUM_BUCKETS, dtype=jnp.int32).reshape(
                  (NUM_BUCKETS,) + (1,) * (bucket.ndim - bucket_axis))).astype(F32)
    out = jnp.tensordot(rel_bias.astype(F32).T, onehot, axes=((1,), (bucket_axis,)),
                        precision=lax.Precision.HIGHEST)
    return jnp.moveaxis(out, 0, bucket_axis)


def _silu(x):
    return x * jax.nn.sigmoid(x)


def _dot_nt(a, b, precision=None):
    return lax.dot_general(a, b, (((1,), (1,)), ((), ())), precision=precision,
                           preferred_element_type=F32)


def _select_top_blocks(gate, block_id, n_past, axis=-1):
    gate = jnp.where(block_id < n_past, gate, GATE_MASKED)
    id_f = block_id.astype(F32)
    sel = jnp.zeros(gate.shape, jnp.bool_)
    for _ in range(MOBA_TOPK):
        mx = jnp.max(gate, axis=axis, keepdims=True)
        first = jnp.min(jnp.where(gate == mx, id_f, float(LANES)), axis=axis, keepdims=True)
        pick = (id_f == first) & (mx > GATE_VALID_MIN)
        sel = sel | pick
        gate = jnp.where(pick, GATE_MASKED, gate)
    return sel


def _inproj_kernel(x_ref, gain_ref, w_ref, o_ref, xn_ref, *, row_chunk):
    @pl.when(pl.program_id(1) == 0)
    def _():
        for r in range(0, x_ref.shape[0], row_chunk):
            x = x_ref[r:r + row_chunk, :]
            ms = jnp.mean(x * x, axis=-1, keepdims=True)
            xn_ref[r:r + row_chunk, :] = (x * lax.rsqrt(ms + RMS_EPS) * gain_ref[...]).astype(BF16)

    o_ref[...] = jnp.dot(xn_ref[...], w_ref[...], preferred_element_type=F32)


def _in_projection(x, gain, w_bf16, tm, tn):
    m, d = x.shape
    n = w_bf16.shape[1]
    return pl.pallas_call(
        functools.partial(_inproj_kernel, row_chunk=min(tm, 128)),
        grid=(m // tm, n // tn),
        in_specs=[pl.BlockSpec((tm, d), lambda i, j: (i, 0)),
                  pl.BlockSpec((1, d), lambda i, j: (0, 0)),
                  pl.BlockSpec((d, tn), lambda i, j: (0, j))],
        out_specs=pl.BlockSpec((tm, tn), lambda i, j: (i, j)),
        out_shape=jax.ShapeDtypeStruct((m, n), F32),
        scratch_shapes=[pltpu.VMEM((tm, d), BF16)],
        compiler_params=pltpu.CompilerParams(
            dimension_semantics=("parallel", "arbitrary"), vmem_limit_bytes=VMEM_LIMIT),
    )(x, gain.reshape(1, d), w_bf16)


def _kmean_kernel(k_ref, o_ref, *, blocks_per_step):
    k = k_ref[...].reshape(blocks_per_step, MOBA_BLOCK, KV_WIDTH)
    o_ref[...] = jnp.sum(k, axis=1) * (1.0 / MOBA_BLOCK)


def _block_means(z, n_blk, k_col0):
    bps = 8
    assert n_blk % bps == 0 and k_col0 % KV_WIDTH == 0
    return pl.pallas_call(
        functools.partial(_kmean_kernel, blocks_per_step=bps),
        grid=(n_blk // bps,),
        in_specs=[pl.BlockSpec((bps * MOBA_BLOCK, KV_WIDTH), lambda i: (i, k_col0 // KV_WIDTH))],
        out_specs=pl.BlockSpec((bps, KV_WIDTH), lambda i: (i, 0)),
        out_shape=jax.ShapeDtypeStruct((n_blk, KV_WIDTH), F32),
        compiler_params=pltpu.CompilerParams(vmem_limit_bytes=VMEM_LIMIT),
    )(z)


def _prompt_attn_kernel(thr_ref,
                        q_ref, k_ref, v_ref, km_ref, ga_ref, tab_ref, blo_ref, bhi_ref,
                        o_ref, kaug_ref, vaug_ref, qaug_ref, m_ref, l_ref, acc_ref, dmat_ref,
                        *, n_blk_pad):
    g = pl.program_id(0)
    i = pl.program_id(1)
    blk = MOBA_BLOCK
    all_rows = KV_GROUP * blk
    n_blk = k_ref.shape[0] // blk

    @pl.when(i == 0)
    def _stage_keys_values():
        lane = lax.broadcasted_iota(jnp.int32, (blk, LANES), 1)
        ones = jnp.ones((blk, LANES), BF16)

        def stage(n, carry):
            r = pl.multiple_of(n * blk, blk)
            kaug_ref[pl.ds(r, blk), 0:HEAD_DIM] = k_ref[pl.ds(r, blk), :].astype(BF16)
            kaug_ref[pl.ds(r, blk), HEAD_DIM:HEAD_DIM + LANES] = jnp.where(lane == n, 1.0, 0.0).astype(BF16)
            vaug_ref[pl.ds(r, blk), 0:HEAD_DIM] = v_ref[pl.ds(r, blk), :].astype(BF16)
            vaug_ref[pl.ds(r, blk), HEAD_DIM:HEAD_DIM + LANES] = ones
            return carry

        lax.fori_loop(0, n_blk, stage, 0)
        dmat_ref[...] = (lax.broadcasted_iota(jnp.int32, (blk, blk), 0)
                         - lax.broadcasted_iota(jnp.int32, (blk, blk), 1))

    m_ref[...] = jnp.full(m_ref.shape, NEG_INF, F32)
    l_ref[...] = jnp.zeros(l_ref.shape, F32)
    acc_ref[...] = jnp.zeros(acc_ref.shape, F32)
    q_all = jnp.concatenate(
        [q_ref[:, hh * HEAD_DIM:(hh + 1) * HEAD_DIM] for hh in range(KV_GROUP)], axis=0)
    gate_t = _dot_nt(km_ref[0:n_blk_pad, :], q_all, precision=lax.Precision.HIGHEST)
    blk_id = lax.broadcasted_iota(jnp.int32, (n_blk_pad, all_rows), 0)
    sel_t = _select_top_blocks(gate_t, blk_id, i, axis=0)
    mask_t = jnp.where(sel_t | (blk_id == i), 0.0, NEG_INF)
    if n_blk_pad < LANES:
        mask_t = jnp.concatenate([mask_t, jnp.zeros((LANES - n_blk_pad, all_rows), F32)], axis=0)
    qaug_ref[:, 0:HEAD_DIM] = (q_all * (ATTN_SCALE * LOG2_E)).astype(BF16)
    qaug_ref[:, HEAD_DIM:HEAD_DIM + LANES] = mask_t.T.astype(BF16)

    def sweep(groups):
        m_all, l_all, acc_all = m_ref[...], l_ref[...], acc_ref[...]
        chunks = [slice(c, c + SWEEP_ROWS) for c in range(0, all_rows, SWEEP_ROWS)]
        m_c = [m_all[rows] for rows in chunks]
        l_c = [l_all[rows] for rows in chunks]
        acc_c = [acc_all[rows] for rows in chunks]
        logits = []
        for n0, nb, bias in groups:
            r = pl.multiple_of(n0 * blk, blk)
            keys = kaug_ref[pl.ds(r, nb * blk), :]
            logits.append([_dot_nt(qaug_ref[rows, :], keys) + bias[rows] for rows in chunks])
        for (n0, nb, bias), group_logits in zip(groups, logits):
            r = pl.multiple_of(n0 * blk, blk)
            values = vaug_ref[pl.ds(r, nb * blk), :]
            for ci, rows in enumerate(chunks):
                sc = group_logits[ci]
                m_new = jnp.maximum(m_c[ci], jnp.max(sc, axis=-1, keepdims=True))
                alpha = jnp.exp2(m_c[ci] - m_new)
                p = jnp.exp2(sc - jnp.concatenate([m_new] * (nb * blk // LANES), axis=1))
                pv = jnp.dot(p.astype(BF16), values, preferred_element_type=F32)
                l_c[ci] = alpha * l_c[ci] + pv[:, HEAD_DIM:]
                acc_c[ci] = alpha * acc_c[ci] + pv[:, :HEAD_DIM]
                m_c[ci] = m_new
        m_ref[...] = jnp.concatenate(m_c, axis=0)
        l_ref[...] = jnp.concatenate(l_c, axis=0)
        acc_ref[...] = jnp.concatenate(acc_c, axis=0)

    def far_bias(n0, nb):
        tiles = []
        for j in range(nb):
            delta = i - (n0 + j)
            high = dmat_ref[...] >= thr_ref[delta]
            base = delta * N_HEADS + g * KV_GROUP
            tiles.append(jnp.concatenate([jnp.where(high, bhi_ref[base + hh], blo_ref[base + hh])
                                          for hh in range(KV_GROUP)], axis=0))
        return jnp.concatenate(tiles, axis=1)

    n_far = jnp.maximum(i - (TABLE_DELTAS - 1), 0)
    per_iter = FAR_GROUPS * FAR_GROUP_BLOCKS

    def far_step(t, carry):
        first = t * per_iter
        sweep([(first + j * FAR_GROUP_BLOCKS, FAR_GROUP_BLOCKS,
                far_bias(first + j * FAR_GROUP_BLOCKS, FAR_GROUP_BLOCKS)) for j in range(FAR_GROUPS)])
        return carry

    lax.fori_loop(0, n_far // per_iter, far_step, 0)

    left = n_far % per_iter
    near = jnp.minimum(i + 1, TABLE_DELTAS)
    for count in range(1, TABLE_DELTAS + 1):
        for rem in range(per_iter if count == TABLE_DELTAS else 1):
            @pl.when((near == count) & (left == rem))
            def _tail(count=count, rem=rem):
                groups = []
                for first in range(0, rem, FAR_GROUP_BLOCKS):
                    nb = min(FAR_GROUP_BLOCKS, rem - first)
                    groups.append((n_far - rem + first, nb, far_bias(n_far - rem + first, nb)))
                groups.append((i - (count - 1), count, jnp.concatenate(
                    [tab_ref[count - 1 - j].reshape(all_rows, blk) for j in range(count)], axis=1)))
                sweep(groups)


    attn = acc_ref[...] / l_ref[...]
    for hh in range(KV_GROUP):
        cols = slice(hh * HEAD_DIM, (hh + 1) * HEAD_DIM)
        o_ref[:, cols] = (attn[hh * blk:(hh + 1) * blk, :] * _silu(ga_ref[:, cols])).astype(BF16)


def _prompt_bias_tables(rel_bias, n_blk):
    blk = MOBA_BLOCK
    r = jnp.arange(blk, dtype=jnp.int32)
    dist = (jnp.arange(TABLE_DELTAS, dtype=jnp.int32)[:, None, None] * blk
            + r[None, :, None] - r[None, None, :])
    tab = _bias_of_bucket(rel_bias, _rel_bucket(dist), 1)
    tab = jnp.where((dist >= 0)[:, None], tab * LOG2_E, NEG_INF)

    nd = max(n_blk, TABLE_DELTAS + 1)
    off = np.arange(-(blk - 1), blk, dtype=np.int32)
    dist_np = np.arange(nd, dtype=np.int32)[:, None] * blk + off[None, :]
    buckets_np = _rel_bucket_np(dist_np)
    span = buckets_np[TABLE_DELTAS:, -1] - buckets_np[TABLE_DELTAS:, 0]
    assert np.all(np.diff(buckets_np[TABLE_DELTAS:], axis=1) >= 0) and np.all(span <= 1), \
        "more than one bias bucket boundary inside a far block pair"
    buckets = _rel_bucket(jnp.asarray(dist_np))
    b_lo, b_hi = buckets[:, 0], buckets[:, -1]
    first_hi = jnp.argmax(buckets == b_hi[:, None], axis=1).astype(jnp.int32)
    thr = first_hi - (blk - 1)
    blo = (rel_bias.astype(F32)[b_lo] * LOG2_E).reshape(-1)
    bhi = (rel_bias.astype(F32)[b_hi] * LOG2_E).reshape(-1)
    return tab, thr, blo, bhi


def _prompt_attention(z, kmean_pad, rel_bias, seq, k_col0, v_col0, ga_col0):
    blk = MOBA_BLOCK
    n_blk = seq // blk
    assert n_blk <= LANES
    tab, thr, blo, bhi = _prompt_bias_tables(rel_bias, n_blk)
    kc, vc, gc = k_col0 // HEAD_DIM, v_col0 // HEAD_DIM, ga_col0 // GROUP_WIDTH
    smem = pl.BlockSpec(memory_space=pltpu.SMEM)
    grid_spec = pltpu.PrefetchScalarGridSpec(
        num_scalar_prefetch=1,
        grid=(N_KV_HEADS, n_blk),
        in_specs=[
            pl.BlockSpec((blk, GROUP_WIDTH), lambda g, i, th: (i, g)),
            pl.BlockSpec((seq, HEAD_DIM), lambda g, i, th: (0, kc + g)),
            pl.BlockSpec((seq, HEAD_DIM), lambda g, i, th: (0, vc + g)),
            pl.BlockSpec((LANES, HEAD_DIM), lambda g, i, th: (0, g)),
            pl.BlockSpec((blk, GROUP_WIDTH), lambda g, i, th: (i, gc + g)),
            pl.BlockSpec((TABLE_DELTAS, KV_GROUP, blk, blk), lambda g, i, th: (0, g, 0, 0)),
            smem, smem,
        ],
        out_specs=pl.BlockSpec((blk, GROUP_WIDTH), lambda g, i, th: (i, g)),
        scratch_shapes=[
            pltpu.VMEM((seq, HEAD_DIM + LANES), BF16),
            pltpu.VMEM((seq, HEAD_DIM + LANES), BF16),
            pltpu.VMEM((KV_GROUP * blk, HEAD_DIM + LANES), BF16),
            pltpu.VMEM((KV_GROUP * blk, LANES), F32),
            pltpu.VMEM((KV_GROUP * blk, LANES), F32),
            pltpu.VMEM((KV_GROUP * blk, HEAD_DIM), F32),
            pltpu.VMEM((blk, blk), jnp.int32),
        ],
    )
    return pl.pallas_call(
        functools.partial(_prompt_attn_kernel, n_blk_pad=-(-n_blk // 8) * 8),
        grid_spec=grid_spec,
        out_shape=jax.ShapeDtypeStruct((seq, ATTN_WIDTH), BF16),
        compiler_params=pltpu.CompilerParams(
            dimension_semantics=("arbitrary", "arbitrary"), vmem_limit_bytes=VMEM_LIMIT),
    )(thr, z, z, z, kmean_pad, z, tab, blo, bhi)


def _block_of_pages(page_refs):
    return jnp.concatenate(
        [jnp.concatenate([ref[0, pl.ds(gi, PAGE_SIZE, stride=N_KV_HEADS), :] for gi in range(N_KV_HEADS)],
                         axis=1) for ref in page_refs], axis=0)


def _sample_attn_kernel(pt_ref,q_ref, knew_ref, vnew_ref, ga_ref, *rest, n_past, dec_seq, blocks_per_step):
    n_pages = blocks_per_step * PAGES_PER_BLOCK
    k_pages, v_pages = rest[:n_pages], rest[n_pages:2 * n_pages]
    bias_ref, o_ref, qf_ref, qb_ref, gate_ref, mx_ref, l_ref, part_ref = rest[2 * n_pages:]
    t = pl.program_id(1)
    rows = N_HEADS * dec_seq
    grp_rows = KV_GROUP * dec_seq
    lane = lax.broadcasted_iota(jnp.int32, (rows, LANES), 1)

    @pl.when(t == 0)
    def _init():
        gate_ref[...] = jnp.zeros(gate_ref.shape, F32)
        mx_ref[...] = jnp.zeros(mx_ref.shape, F32)
        l_ref[...] = jnp.zeros(l_ref.shape, F32)
        col_grp = lax.broadcasted_iota(jnp.int32, (dec_seq, KV_WIDTH), 1) // HEAD_DIM
        for h in range(N_HEADS):
            qh = q_ref[:, h * HEAD_DIM:(h + 1) * HEAD_DIM]
            wide = jnp.concatenate([qh] * N_KV_HEADS, axis=1)
            wide = jnp.where(col_grp == h // KV_GROUP, wide, 0.0)
            qf_ref[h * dec_seq:(h + 1) * dec_seq, :] = wide
            qb_ref[h * dec_seq:(h + 1) * dec_seq, :] = (wide * ATTN_SCALE).astype(BF16)

    def block_partials(kblks, vblks, biases):
        scs = [_dot_nt(qb_ref[...], kblk.astype(BF16)) + bias for kblk, bias in zip(kblks, biases)]
        ms = [jnp.max(sc, axis=-1, keepdims=True) for sc in scs]
        ps = [jnp.exp(sc - m_n) for sc, m_n in zip(scs, ms)]
        ls = [jnp.sum(p, axis=-1, keepdims=True) for p in ps]
        outs = []
        for p, vblk in zip(ps, vblks):
            o_all = jnp.dot(p.astype(BF16), vblk.astype(BF16), preferred_element_type=F32)
            outs.append(jnp.concatenate(
                [o_all[gi * grp_rows:(gi + 1) * grp_rows, gi * HEAD_DIM:(gi + 1) * HEAD_DIM]
                 for gi in range(N_KV_HEADS)], axis=0))
        return ms, ls, outs

    @pl.when(t < n_past // blocks_per_step)
    def _past():
        kblks = [_block_of_pages(k_pages[jb * PAGES_PER_BLOCK:(jb + 1) * PAGES_PER_BLOCK])
                 for jb in range(blocks_per_step)]
        vblks = [_block_of_pages(v_pages[jb * PAGES_PER_BLOCK:(jb + 1) * PAGES_PER_BLOCK])
                 for jb in range(blocks_per_step)]
        ms, lsums, outs = block_partials(kblks, vblks, [bias_ref[jb] for jb in range(blocks_per_step)])
        mx, ls, gt = mx_ref[...], l_ref[...], gate_ref[...]
        for jb in range(blocks_per_step):
            n = t * blocks_per_step + jb
            kmean = jnp.sum(kblks[jb], axis=0, keepdims=True) * (1.0 / MOBA_BLOCK)
            gate = jnp.sum(qf_ref[...] * kmean, axis=-1, keepdims=True)
            part_ref[n] = outs[jb]
            here = lane == n
            mx = jnp.where(here, ms[jb], mx)
            ls = jnp.where(here, lsums[jb], ls)
            gt = jnp.where(here, gate, gt)
        mx_ref[...] = mx
        l_ref[...] = ls
        gate_ref[...] = gt

    @pl.when(t == n_past // blocks_per_step)
    def _own():
        pad = jnp.zeros((MOBA_BLOCK - dec_seq, KV_WIDTH), F32)
        (m_n,), (l_n,), (o_n,) = block_partials([jnp.concatenate([knew_ref[...], pad], axis=0)],
                                                [jnp.concatenate([vnew_ref[...], pad], axis=0)], [bias_ref[0]])
        own = lane == n_past
        mx = jnp.where(own, m_n, mx_ref[...])
        ls = jnp.where(own, l_n, l_ref[...])
        sel = _select_top_blocks(gate_ref[...], lane, n_past)
        include = sel | own
        m_tot = jnp.max(jnp.where(include, mx, GATE_MASKED), axis=-1, keepdims=True)
        e = jnp.where(include, jnp.exp(jnp.minimum(mx - m_tot, 0.0)), 0.0)
        denom = jnp.sum(e * ls, axis=-1, keepdims=True)
        num = e[:, n_past:n_past + 1] * o_n
        for j in range(n_past):
            num = num + e[:, j:j + 1] * part_ref[j]
        attn = num / denom
        for h in range(N_HEADS):
            cols = slice(h * HEAD_DIM, (h + 1) * HEAD_DIM)
            o_ref[0, :, cols] = (attn[h * dec_seq:(h + 1) * dec_seq, :] * _silu(ga_ref[:, cols])).astype(BF16)


def _sample_bias_table(rel_bias, past_len, dec_seq, n_blocks):
    kpos = jnp.arange(n_blocks * MOBA_BLOCK, dtype=jnp.int32).reshape(n_blocks, 1, MOBA_BLOCK)
    qpos = (past_len + jnp.arange(dec_seq, dtype=jnp.int32)).reshape(1, dec_seq, 1)
    dist = qpos - kpos
    vals = _bias_of_bucket(rel_bias, _rel_bucket(dist), 1)
    vals = jnp.where((dist >= 0)[:, None], vals, NEG_INF)
    return vals.reshape(n_blocks, N_HEADS * dec_seq, MOBA_BLOCK)


def _sample_attention(q_s, k_s, v_s, ga_s, cache_k, cache_v, page_table, rel_bias, past_len, dec_seq):
    bsz, n_pages = page_table.shape
    n_past = past_len // MOBA_BLOCK
    bps = 8
    assert past_len % MOBA_BLOCK == 0 and n_past + 1 <= LANES and n_pages == n_past * PAGES_PER_BLOCK
    assert n_past % bps == 0
    n_steps = n_past // bps + 1
    pages_per_step = bps * PAGES_PER_BLOCK
    bias = _sample_bias_table(rel_bias, past_len, dec_seq, n_steps * bps)
    rows = N_HEADS * dec_seq

    def page_spec(which):
        def index_map(b, t, pt):
            step = jnp.minimum(t, n_steps - 2)
            return (pt[b * n_pages + step * pages_per_step + which], 0, 0)
        return pl.BlockSpec((1, PAGE_SIZE * N_KV_HEADS, HEAD_DIM), index_map)

    row_spec = lambda width: pl.BlockSpec((dec_seq, width), lambda b, t, pt: (b, 0))
    grid_spec = pltpu.PrefetchScalarGridSpec(
        num_scalar_prefetch=1,
        grid=(bsz, n_steps),
        in_specs=([row_spec(ATTN_WIDTH), row_spec(KV_WIDTH), row_spec(KV_WIDTH), row_spec(ATTN_WIDTH)]
                  + [page_spec(j) for j in range(pages_per_step)]
                  + [page_spec(j) for j in range(pages_per_step)]
                  + [pl.BlockSpec((bps, rows, MOBA_BLOCK), lambda b, t, pt: (t, 0, 0))]),
        out_specs=pl.BlockSpec((1, dec_seq, ATTN_WIDTH), lambda b, t, pt: (b, 0, 0)),
        scratch_shapes=[
            pltpu.VMEM((rows, KV_WIDTH), F32),
            pltpu.VMEM((rows, KV_WIDTH), BF16),
            pltpu.VMEM((rows, LANES), F32),
            pltpu.VMEM((rows, LANES), F32),
            pltpu.VMEM((rows, LANES), F32),
            pltpu.VMEM((n_past, rows, HEAD_DIM), F32),
        ],
    )
    out = pl.pallas_call(
        functools.partial(_sample_attn_kernel, n_past=n_past, dec_seq=dec_seq, blocks_per_step=bps),
        grid_spec=grid_spec,
        out_shape=jax.ShapeDtypeStruct((bsz, dec_seq, ATTN_WIDTH), BF16),
        compiler_params=pltpu.CompilerParams(
            dimension_semantics=("arbitrary", "arbitrary"), vmem_limit_bytes=VMEM_LIMIT),
    )(page_table.reshape(-1), q_s, k_s, v_s, ga_s, *([cache_k] * pages_per_step),
      *([cache_v] * pages_per_step), bias)
    return out.reshape(bsz * dec_seq, ATTN_WIDTH)


def _window_mix(ext_ref, window, rows, cols, count, u, gp, w, scale):
    win = u
    for j in range(1, window):
        win = win + ext_ref[POOL_HALO - j:POOL_HALO - j + rows, cols]
    zed = win / count - u
    y = jnp.dot(zed.astype(BF16), w, preferred_element_type=F32)
    return (y * scale) * _silu(gp)


def _pool_prompt_kernel(prev_ref, u_ref, gp_ref, w_ref, scale_ref, o_ref, ext_ref, *, groups_per_step):
    i = pl.program_id(0)
    rows = u_ref.shape[0]
    gw = w_ref.shape[1]
    ext_ref[0:POOL_HALO, :] = jnp.where(i == 0, 0.0, prev_ref[...])
    ext_ref[POOL_HALO:POOL_HALO + rows, :] = u_ref[...]
    pos = i * rows + lax.broadcasted_iota(jnp.int32, (rows, 1), 0)
    for part in range(N_POOL_GROUPS // groups_per_step):
        @pl.when(pl.program_id(1) == part)
        def _(part=part):
            for gl in range(groups_per_step):
                window = POOL_WINDOWS[part * groups_per_step + gl]
                cols = slice(gl * gw, (gl + 1) * gw)
                count = jnp.minimum(window, pos + 1).astype(F32)
                y = _window_mix(ext_ref, window, rows, cols, count, u_ref[:, cols], gp_ref[:, cols],
                                w_ref[gl], scale_ref[:, cols])
                o_ref[:, cols] = y.astype(BF16)


def _pool_prompt(z, w_pool_bf16, pool_scale, seq, u_col0, gp_col0, tm):
    n_groups, gw, _ = w_pool_bf16.shape
    gps = 2
    bw = gps * gw
    assert n_groups == N_POOL_GROUPS and u_col0 % bw == 0 and gp_col0 % bw == 0 and tm % POOL_HALO == 0
    uc, gc = u_col0 // bw, gp_col0 // bw
    halo_blocks = tm // POOL_HALO
    return pl.pallas_call(
        functools.partial(_pool_prompt_kernel, groups_per_step=gps),
        grid=(seq // tm, n_groups // gps),
        in_specs=[pl.BlockSpec((POOL_HALO, bw), lambda i, c: (jnp.maximum(i * halo_blocks - 1, 0), uc + c)),
                  pl.BlockSpec((tm, bw), lambda i, c: (i, uc + c)),
                  pl.BlockSpec((tm, bw), lambda i, c: (i, gc + c)),
                  pl.BlockSpec((gps, gw, gw), lambda i, c: (c, 0, 0)),
                  pl.BlockSpec((1, bw), lambda i, c: (0, c))],
        out_specs=pl.BlockSpec((tm, bw), lambda i, c: (i, c)),
        out_shape=jax.ShapeDtypeStruct((seq, n_groups * gw), BF16),
        scratch_shapes=[pltpu.VMEM((POOL_HALO + tm, bw), F32)],
        compiler_params=pltpu.CompilerParams(
            dimension_semantics=("parallel", "arbitrary"), vmem_limit_bytes=VMEM_LIMIT),
    )(z, z, z, w_pool_bf16, pool_scale.reshape(1, n_groups * gw))


def _pool_sample_kernel(ext_in_ref, gp_ref, w_ref, scale_ref, o_ref, ext_ref, *, bsz, dec_seq):
    seg = POOL_HALO + dec_seq
    rows = bsz * seg
    gw = w_ref.shape[1]
    ext_ref[0:POOL_HALO, :] = jnp.zeros((POOL_HALO, ext_ref.shape[1]), F32)
    ext_ref[POOL_HALO:POOL_HALO + rows, :] = ext_in_ref[...]
    for gi, window in enumerate(POOL_WINDOWS):
        cols = slice(gi * gw, (gi + 1) * gw)
        u_all = ext_in_ref[:, cols]
        win = u_all
        for j in range(1, window):
            win = win + ext_ref[POOL_HALO - j:POOL_HALO - j + rows, cols]
        zed = (win / float(window) - u_all).reshape(bsz, seg, gw)[:, POOL_HALO:, :].reshape(bsz * dec_seq, gw)
        y = jnp.dot(zed.astype(BF16), w_ref[gi], preferred_element_type=F32)
        o_ref[:, cols] = ((y * scale_ref[:, cols]) * _silu(gp_ref[:, cols])).astype(BF16)


def _pool_sample(u_ext, gp_s, w_pool_bf16, pool_scale, bsz, dec_seq):
    pw = u_ext.shape[1]
    rows = u_ext.shape[0]
    vm = pl.BlockSpec(memory_space=pltpu.VMEM)
    return pl.pallas_call(
        functools.partial(_pool_sample_kernel, bsz=bsz, dec_seq=dec_seq),
        in_specs=[vm, vm, vm, vm],
        out_specs=vm,
        out_shape=jax.ShapeDtypeStruct((bsz * dec_seq, pw), BF16),
        scratch_shapes=[pltpu.VMEM((POOL_HALO + rows, pw), F32)],
        compiler_params=pltpu.CompilerParams(vmem_limit_bytes=VMEM_LIMIT),
    )(u_ext, gp_s, w_pool_bf16, pool_scale.reshape(1, pw))


def _outproj_kernel(ya_ref, yp_ref, w_ref, x_ref, gain_ref, o_ref, *, k_attn):
    k = pl.program_id(1)

    @pl.when(k == 0)
    def _():
        o_ref[...] = x_ref[...]

    @pl.when(k < k_attn)
    def _():
        o_ref[...] += jnp.dot(ya_ref[...], w_ref[...], preferred_element_type=F32)

    @pl.when(k >= k_attn)
    def _():
        o_ref[...] += jnp.dot(yp_ref[...], w_ref[...], preferred_element_type=F32)

    @pl.when(k == pl.num_programs(1) - 1)
    def _():
        h = o_ref[...]
        ms = jnp.mean(h * h, axis=-1, keepdims=True)
        o_ref[...] = h * lax.rsqrt(ms + RMS_EPS) * gain_ref[...]


def _out_projection(ya, yp, w_bf16, x, gain, tm, tk):
    m, d = x.shape
    ka, kp = ya.shape[1] // tk, yp.shape[1] // tk
    return pl.pallas_call(
        functools.partial(_outproj_kernel, k_attn=ka),
        grid=(m // tm, ka + kp),
        in_specs=[pl.BlockSpec((tm, tk), lambda i, k: (i, jnp.minimum(k, ka - 1))),
                  pl.BlockSpec((tm, tk), lambda i, k: (i, jnp.maximum(k - ka, 0))),
                  pl.BlockSpec((tk, d), lambda i, k: (k, 0)),
                  pl.BlockSpec((tm, d), lambda i, k: (i, 0)),
                  pl.BlockSpec((1, d), lambda i, k: (0, 0))],
        out_specs=pl.BlockSpec((tm, d), lambda i, k: (i, 0)),
        out_shape=jax.ShapeDtypeStruct((m, d), F32),
        compiler_params=pltpu.CompilerParams(
            dimension_semantics=("parallel", "arbitrary"), vmem_limit_bytes=VMEM_LIMIT),
    )(ya, yp, w_bf16, x, gain.reshape(1, d))


def kernel(x_prompt, x_sample, cache_k, cache_v, state_pool, page_table, norm_in, w_in, w_pool,
           pool_scale, w_out, rel_bias, norm_out):
    depth = norm_in.shape[0]
    assert depth == 1, "single layer step"
    n_batch, seq, d_model = x_prompt.shape
    bsz, dec_seq, _ = x_sample.shape
    assert n_batch == 1 and seq % (8 * MOBA_BLOCK) == 0 and dec_seq == 8
    past_len = page_table.shape[1] * PAGE_SIZE
    assert past_len >= POOL_HIST
    pool_width = pool_scale.shape[1]
    k0 = ATTN_WIDTH
    v0 = k0 + KV_WIDTH
    ga0 = v0 + KV_WIDTH
    u0 = ga0 + ATTN_WIDTH
    gp0 = u0 + pool_width
    assert w_in.shape[2] == gp0 + pool_width and w_out.shape[1] == ATTN_WIDTH + pool_width

    w_in_b = w_in[0].astype(BF16)
    w_out_b = w_out[0].astype(BF16)
    w_pool_b = w_pool[0].astype(BF16)
    xp = x_prompt[0]
    xs = x_sample.reshape(bsz * dec_seq, d_model)

    z = _in_projection(xp, norm_in[0], w_in_b, tm=512, tn=1024)
    n_blk = seq // MOBA_BLOCK
    kmean = _block_means(z, n_blk, k0)
    kmean_pad = jnp.pad(kmean, ((0, LANES - n_blk), (0, 0)))
    ya = _prompt_attention(z, kmean_pad, rel_bias, seq, k0, v0, ga0)
    yp = _pool_prompt(z, w_pool_b, pool_scale[0], seq, u0, gp0, tm=256)
    y_prompt = _out_projection(ya, yp, w_out_b, xp, norm_out, tm=512, tk=512)

    zs = _in_projection(xs, norm_in[0], w_in_b, tm=bsz * dec_seq, tn=1024)
    q_s, k_s, v_s = zs[:, :k0], zs[:, k0:v0], zs[:, v0:ga0]
    ga_s, u_s, gp_s = zs[:, ga0:u0], zs[:, u0:gp0], zs[:, gp0:]
    n_phys = cache_k.shape[1]
    page_rows = PAGE_SIZE * N_KV_HEADS
    ya_s = _sample_attention(q_s, k_s, v_s, ga_s, cache_k.reshape(n_phys, page_rows, HEAD_DIM),
                             cache_v.reshape(n_phys, page_rows, HEAD_DIM), page_table, rel_bias,
                             past_len, dec_seq)
    u_s3 = u_s.reshape(bsz, dec_seq, pool_width)
    u_ext = jnp.concatenate([jnp.zeros((bsz, 1, pool_width), F32), state_pool[0].astype(F32), u_s3], axis=1)
    yp_s = _pool_sample(u_ext.reshape(bsz * (POOL_HALO + dec_seq), pool_width), gp_s, w_pool_b,
                        pool_scale[0], bsz, dec_seq)
    y_sample = _out_projection(ya_s, yp_s, w_out_b, xs, norm_out, tm=bsz * dec_seq, tk=512)

    k_prompt = z[:, k0:v0].reshape(1, 1, seq, N_KV_HEADS, HEAD_DIM)
    v_prompt = z[:, v0:ga0].reshape(1, 1, seq, N_KV_HEADS, HEAD_DIM)
    pool_prompt = z[seq - POOL_HIST:, u0:gp0].reshape(1, 1, POOL_HIST, pool_width)
    k_sample = k_s.reshape(1, bsz, dec_seq, N_KV_HEADS, HEAD_DIM)
    v_sample = v_s.reshape(1, bsz, dec_seq, N_KV_HEADS, HEAD_DIM)
    pool_sample = u_ext[:, 1 + dec_seq:, :].reshape(1, bsz, POOL_HIST, pool_width).astype(state_pool.dtype)
    return (y_prompt.reshape(1, seq, d_model), y_sample.reshape(bsz, dec_seq, d_model),
            k_prompt, v_prompt, pool_prompt, k_sample, v_sample, pool_sample)
```

```python
import functools
import math
from typing import NamedTuple

import numpy as np
import jax
import jax.numpy as jnp
from jax import lax
from jax.experimental import pallas as pl
from jax.experimental.pallas import tpu as pltpu

F32 = jnp.float32
BF16 = jnp.bfloat16

HEAD_DIM = 128
N_HEADS = 16
N_KV_HEADS = 4
KV_GROUP = N_HEADS // N_KV_HEADS
ATTN_WIDTH = N_HEADS * HEAD_DIM
KV_WIDTH = N_KV_HEADS * HEAD_DIM
GROUP_WIDTH = KV_GROUP * HEAD_DIM
ATTN_SCALE = HEAD_DIM ** -0.5
LOG2_E = math.log2(math.e)
MOBA_BLOCK = 256
MOBA_TOPK = 3
PAGE_SIZE = 128
PAGES_PER_BLOCK = MOBA_BLOCK // PAGE_SIZE
POOL_WINDOWS = (2, 4, 8, 16)
N_POOL_GROUPS = len(POOL_WINDOWS)
POOL_HIST = max(POOL_WINDOWS) - 1
POOL_HALO = POOL_HIST + 1
NUM_BUCKETS = 32
MAX_EXACT = NUM_BUCKETS // 2
MAX_DISTANCE = 4096
RMS_EPS = 1e-6
NEG_INF = -1e30
GATE_MASKED = -3.0e38
GATE_VALID_MIN = -1.0e38
LANES = 128
TABLE_DELTAS = 3
SWEEP_ROWS = 512
FAR_GROUP_BLOCKS = 2
FAR_GROUPS = 2
SAMPLE_BLOCKS_PER_STEP = 8

VMEM_LIMIT = 56 * 1024 * 1024


def _rel_bucket(dist):
    n = jnp.maximum(dist, 0)
    nf = jnp.maximum(n, 1).astype(F32)
    large = MAX_EXACT + (jnp.log(nf / MAX_EXACT) / math.log(MAX_DISTANCE / MAX_EXACT)
                         * (NUM_BUCKETS - MAX_EXACT)).astype(jnp.int32)
    large = jnp.minimum(large, NUM_BUCKETS - 1)
    return jnp.where(n < MAX_EXACT, n, large)


def _rel_bucket_np(dist):
    n = np.maximum(dist, 0)
    nf = np.maximum(n, 1).astype(np.float32)
    large = MAX_EXACT + (np.log(nf / np.float32(MAX_EXACT)) / np.float32(math.log(MAX_DISTANCE / MAX_EXACT))
                         * np.float32(NUM_BUCKETS - MAX_EXACT)).astype(np.int32)
    large = np.minimum(large, NUM_BUCKETS - 1)
    return np.where(n < MAX_EXACT, n, large)


def _bias_of_bucket(rel_bias, bucket, bucket_axis):
    onehot = (jnp.expand_dims(bucket, bucket_axis)
              == jnp.arange(NUM_BUCKETS, dtype=jnp.int32).reshape(
                  (NUM_BUCKETS,) + (1,) * (bucket.ndim - bucket_axis))).astype(F32)
    out = jnp.tensordot(rel_bias.astype(F32).T, onehot, axes=((1,), (bucket_axis,)),
                        precision=lax.Precision.HIGHEST)
    return jnp.moveaxis(out, 0, bucket_axis)


def _silu(x):
    return x * jax.nn.sigmoid(x)


def _dot_nt(a, b, precision=None):
    return lax.dot_general(a, b, (((1,), (1,)), ((), ())), precision=precision,
                           preferred_element_type=F32)


def _select_top_blocks(gate, block_id, n_past, axis=-1):
    gate = jnp.where(block_id < n_past, gate, GATE_MASKED)
    id_f = block_id.astype(F32)
    sel = jnp.zeros(gate.shape, jnp.bool_)
    for _ in range(MOBA_TOPK):
        mx = jnp.max(gate, axis=axis, keepdims=True)
        first = jnp.min(jnp.where(gate == mx, id_f, float(LANES)), axis=axis, keepdims=True)
        pick = (id_f == first) & (mx > GATE_VALID_MIN)
        sel = sel | pick
        gate = jnp.where(pick, GATE_MASKED, gate)
    return sel


def _inproj_kernel(x_ref, gain_ref, w_ref, o_ref, xn_ref, *, row_chunk):
    @pl.when(pl.program_id(1) == 0)
    def _():
        for r in range(0, x_ref.shape[0], row_chunk):
            x = x_ref[r:r + row_chunk, :]
            ms = jnp.mean(x * x, axis=-1, keepdims=True)
            xn_ref[r:r + row_chunk, :] = (x * lax.rsqrt(ms + RMS_EPS) * gain_ref[...]).astype(BF16)

    o_ref[...] = jnp.dot(xn_ref[...], w_ref[...], preferred_element_type=F32)


def _in_projection(x, gain, w_bf16, tm, tn):
    m, d = x.shape
    n = w_bf16.shape[1]
    return pl.pallas_call(
        functools.partial(_inproj_kernel, row_chunk=min(tm, 128)),
        grid=(m // tm, n // tn),
        in_specs=[pl.BlockSpec((tm, d), lambda i, j: (i, 0)),
                  pl.BlockSpec((1, d), lambda i, j: (0, 0)),
                  pl.BlockSpec((d, tn), lambda i, j: (0, j))],
        out_specs=pl.BlockSpec((tm, tn), lambda i, j: (i, j)),
        out_shape=jax.ShapeDtypeStruct((m, n), F32),
        scratch_shapes=[pltpu.VMEM((tm, d), BF16)],
        compiler_params=pltpu.CompilerParams(
            dimension_semantics=("parallel", "arbitrary"), vmem_limit_bytes=VMEM_LIMIT),
    )(x, gain.reshape(1, d), w_bf16)


def _stage_kv_kernel(k_ref, v_ref, km_ref, kaug_ref, vaug_ref, *, blocks_per_step):
    rows = blocks_per_step * MOBA_BLOCK
    k = k_ref[...]
    km_ref[...] = jnp.sum(k.reshape(blocks_per_step, MOBA_BLOCK, KV_WIDTH), axis=1) * (1.0 / MOBA_BLOCK)
    block_of_row = (pl.program_id(0) * blocks_per_step
                    + lax.broadcasted_iota(jnp.int32, (rows, LANES), 0) // MOBA_BLOCK)
    onehot = jnp.where(lax.broadcasted_iota(jnp.int32, (rows, LANES), 1) == block_of_row, 1.0, 0.0).astype(BF16)
    ones = jnp.ones((rows, LANES), BF16)
    for gi in range(N_KV_HEADS):
        cols = slice(gi * HEAD_DIM, (gi + 1) * HEAD_DIM)
        kaug_ref[gi, :, 0:HEAD_DIM] = k[:, cols].astype(BF16)
        kaug_ref[gi, :, HEAD_DIM:HEAD_DIM + LANES] = onehot
        vaug_ref[gi, :, 0:HEAD_DIM] = v_ref[:, cols].astype(BF16)
        vaug_ref[gi, :, HEAD_DIM:HEAD_DIM + LANES] = ones


def _stage_kv(z, seq, k_col0, v_col0):
    bps = 8
    n_blk = seq // MOBA_BLOCK
    assert n_blk % bps == 0 and k_col0 % KV_WIDTH == 0 and v_col0 % KV_WIDTH == 0
    rows = bps * MOBA_BLOCK
    aug = jax.ShapeDtypeStruct((N_KV_HEADS, seq, HEAD_DIM + LANES), BF16)
    aug_spec = pl.BlockSpec((N_KV_HEADS, rows, HEAD_DIM + LANES), lambda i: (0, i, 0))
    return pl.pallas_call(
        functools.partial(_stage_kv_kernel, blocks_per_step=bps),
        grid=(n_blk // bps,),
        in_specs=[pl.BlockSpec((rows, KV_WIDTH), lambda i: (i, k_col0 // KV_WIDTH)),
                  pl.BlockSpec((rows, KV_WIDTH), lambda i: (i, v_col0 // KV_WIDTH))],
        out_specs=[pl.BlockSpec((bps, KV_WIDTH), lambda i: (i, 0)), aug_spec, aug_spec],
        out_shape=[jax.ShapeDtypeStruct((n_blk, KV_WIDTH), F32), aug, aug],
        compiler_params=pltpu.CompilerParams(vmem_limit_bytes=VMEM_LIMIT),
    )(z, z)


class _SampleShape(NamedTuple):
    n_seq: int
    dec_seq: int
    n_past: int
    blocks_per_step: int
    steps_per_seq: int


def _attn_kernel(thr_ref, pt_ref,
                 q_ref, kaug_ref, vaug_ref, km_ref, ga_ref, tab_ref, blo_ref, bhi_ref,
                 qs_ref, knew_ref, vnew_ref, gas_ref, *rest, n_blk_pad, sample):
    n_cache_pages = sample.blocks_per_step * PAGES_PER_BLOCK
    k_pages, v_pages = rest[:n_cache_pages], rest[n_cache_pages:2 * n_cache_pages]
    (sbias_ref, sbias_own_ref, o_ref, so_ref, qaug_ref, m_ref, l_ref, acc_ref, dmat_ref,
     qf_ref, qb_ref, gate_ref, mx_ref, sl_ref, part_ref) = rest[2 * n_cache_pages:]
    g = pl.program_id(0)
    i = pl.program_id(1)
    blk = MOBA_BLOCK
    all_rows = KV_GROUP * blk

    step = g * pl.num_programs(1) + i

    @pl.when(step < sample.n_seq * sample.steps_per_seq)
    def _sample_chunk():
        _sample_step(step % sample.steps_per_seq, sample, qs_ref, knew_ref, vnew_ref, gas_ref, k_pages, v_pages,
                     sbias_ref, sbias_own_ref, so_ref, qf_ref, qb_ref, gate_ref, mx_ref, sl_ref, part_ref)

    @pl.when(i == 0)
    def _():
        dmat_ref[...] = (lax.broadcasted_iota(jnp.int32, (blk, blk), 0)
                         - lax.broadcasted_iota(jnp.int32, (blk, blk), 1))

    m_ref[...] = jnp.full(m_ref.shape, NEG_INF, F32)
    l_ref[...] = jnp.zeros(l_ref.shape, F32)
    acc_ref[...] = jnp.zeros(acc_ref.shape, F32)
    q_all = jnp.concatenate(
        [q_ref[:, hh * HEAD_DIM:(hh + 1) * HEAD_DIM] for hh in range(KV_GROUP)], axis=0)
    gate_t = _dot_nt(km_ref[0:n_blk_pad, :], q_all, precision=lax.Precision.HIGHEST)
    blk_id = lax.broadcasted_iota(jnp.int32, (n_blk_pad, all_rows), 0)
    sel_t = _select_top_blocks(gate_t, blk_id, i, axis=0)
    mask_t = jnp.where(sel_t | (blk_id == i), 0.0, NEG_INF)
    if n_blk_pad < LANES:
        mask_t = jnp.concatenate([mask_t, jnp.zeros((LANES - n_blk_pad, all_rows), F32)], axis=0)
    qaug_ref[:, 0:HEAD_DIM] = (q_all * (ATTN_SCALE * LOG2_E)).astype(BF16)
    qaug_ref[:, HEAD_DIM:HEAD_DIM + LANES] = mask_t.T.astype(BF16)

    def sweep(groups):
        m_all, l_all, acc_all = m_ref[...], l_ref[...], acc_ref[...]
        chunks = [slice(c, c + SWEEP_ROWS) for c in range(0, all_rows, SWEEP_ROWS)]
        m_c = [m_all[rows] for rows in chunks]
        l_c = [l_all[rows] for rows in chunks]
        acc_c = [acc_all[rows] for rows in chunks]
        logits = []
        for n0, nb, bias in groups:
            r = pl.multiple_of(n0 * blk, blk)
            keys = kaug_ref[0, pl.ds(r, nb * blk), :]
            logits.append([_dot_nt(qaug_ref[rows, :], keys) + bias[rows] for rows in chunks])
        for (n0, nb, bias), group_logits in zip(groups, logits):
            r = pl.multiple_of(n0 * blk, blk)
            values = vaug_ref[0, pl.ds(r, nb * blk), :]
            for ci, rows in enumerate(chunks):
                sc = group_logits[ci]
                m_new = jnp.maximum(m_c[ci], jnp.max(sc, axis=-1, keepdims=True))
                alpha = jnp.exp2(m_c[ci] - m_new)
                p = jnp.exp2(sc - jnp.concatenate([m_new] * (nb * blk // LANES), axis=1))
                pv = jnp.dot(p.astype(BF16), values, preferred_element_type=F32)
                l_c[ci] = alpha * l_c[ci] + pv[:, HEAD_DIM:]
                acc_c[ci] = alpha * acc_c[ci] + pv[:, :HEAD_DIM]
                m_c[ci] = m_new
        m_ref[...] = jnp.concatenate(m_c, axis=0)
        l_ref[...] = jnp.concatenate(l_c, axis=0)
        acc_ref[...] = jnp.concatenate(acc_c, axis=0)

    def far_bias(n0, nb):
        tiles = []
        for j in range(nb):
            delta = i - (n0 + j)
            high = dmat_ref[...] >= thr_ref[delta]
            base = delta * N_HEADS + g * KV_GROUP
            tiles.append(jnp.concatenate([jnp.where(high, bhi_ref[base + hh], blo_ref[base + hh])
                                          for hh in range(KV_GROUP)], axis=0))
        return jnp.concatenate(tiles, axis=1)

    n_far = jnp.maximum(i - (TABLE_DELTAS - 1), 0)
    per_iter = FAR_GROUPS * FAR_GROUP_BLOCKS

    def far_step(t, carry):
        first = t * per_iter
        sweep([(first + j * FAR_GROUP_BLOCKS, FAR_GROUP_BLOCKS,
                far_bias(first + j * FAR_GROUP_BLOCKS, FAR_GROUP_BLOCKS)) for j in range(FAR_GROUPS)])
        return carry

    lax.fori_loop(0, n_far // per_iter, far_step, 0)

    left = n_far % per_iter
    near = jnp.minimum(i + 1, TABLE_DELTAS)
    for count in range(1, TABLE_DELTAS + 1):
        for rem in range(per_iter if count == TABLE_DELTAS else 1):
            @pl.when((near == count) & (left == rem))
            def _tail(count=count, rem=rem):
                groups = []
                for first in range(0, rem, FAR_GROUP_BLOCKS):
                    nb = min(FAR_GROUP_BLOCKS, rem - first)
                    groups.append((n_far - rem + first, nb, far_bias(n_far - rem + first, nb)))
                groups.append((i - (count - 1), count, jnp.concatenate(
                    [tab_ref[count - 1 - j].reshape(all_rows, blk) for j in range(count)], axis=1)))
                sweep(groups)

    attn = acc_ref[...] / l_ref[...]
    for hh in range(KV_GROUP):
        cols = slice(hh * HEAD_DIM, (hh + 1) * HEAD_DIM)
        o_ref[:, cols] = (attn[hh * blk:(hh + 1) * blk, :] * _silu(ga_ref[:, cols])).astype(BF16)


def _block_of_pages(page_refs):
    return jnp.concatenate(
        [jnp.concatenate([ref[0, pl.ds(gi, PAGE_SIZE, stride=N_KV_HEADS), :] for gi in range(N_KV_HEADS)],
                         axis=1) for ref in page_refs], axis=0)


def _sample_step(part, sample, q_ref, knew_ref, vnew_ref, ga_ref, k_pages, v_pages, bias_ref, bias_own_ref,
                 o_ref, qf_ref, qb_ref, gate_ref, mx_ref, l_ref, part_ref):
    dec_seq, n_past, blocks_per_step = sample.dec_seq, sample.n_past, sample.blocks_per_step
    rows = N_HEADS * dec_seq
    grp_rows = KV_GROUP * dec_seq
    lane = lax.broadcasted_iota(jnp.int32, (rows, LANES), 1)

    @pl.when(part == 0)
    def _init():
        gate_ref[...] = jnp.zeros(gate_ref.shape, F32)
        mx_ref[...] = jnp.zeros(mx_ref.shape, F32)
        l_ref[...] = jnp.zeros(l_ref.shape, F32)
        col_grp = lax.broadcasted_iota(jnp.int32, (dec_seq, KV_WIDTH), 1) // HEAD_DIM
        for h in range(N_HEADS):
            qh = q_ref[:, h * HEAD_DIM:(h + 1) * HEAD_DIM]
            wide = jnp.concatenate([qh] * N_KV_HEADS, axis=1)
            wide = jnp.where(col_grp == h // KV_GROUP, wide, 0.0)
            qf_ref[h * dec_seq:(h + 1) * dec_seq, :] = wide
            qb_ref[h * dec_seq:(h + 1) * dec_seq, :] = (wide * ATTN_SCALE).astype(BF16)

    def block_partials(kblks, vblks, biases):
        scs = [_dot_nt(qb_ref[...], kblk.astype(BF16)) + bias for kblk, bias in zip(kblks, biases)]
        ms = [jnp.max(sc, axis=-1, keepdims=True) for sc in scs]
        ps = [jnp.exp(sc - m_n) for sc, m_n in zip(scs, ms)]
        ls = [jnp.sum(p, axis=-1, keepdims=True) for p in ps]
        outs = []
        for p, vblk in zip(ps, vblks):
            o_all = jnp.dot(p.astype(BF16), vblk.astype(BF16), preferred_element_type=F32)
            outs.append(jnp.concatenate(
                [o_all[gi * grp_rows:(gi + 1) * grp_rows, gi * HEAD_DIM:(gi + 1) * HEAD_DIM]
                 for gi in range(N_KV_HEADS)], axis=0))
        return ms, ls, outs

    kblks = [_block_of_pages(k_pages[jb * PAGES_PER_BLOCK:(jb + 1) * PAGES_PER_BLOCK])
             for jb in range(blocks_per_step)]
    vblks = [_block_of_pages(v_pages[jb * PAGES_PER_BLOCK:(jb + 1) * PAGES_PER_BLOCK])
             for jb in range(blocks_per_step)]
    ms, lsums, outs = block_partials(kblks, vblks, [bias_ref[jb] for jb in range(blocks_per_step)])
    mx, ls, gt = mx_ref[...], l_ref[...], gate_ref[...]
    for jb in range(blocks_per_step):
        n = part * blocks_per_step + jb
        kmean = jnp.sum(kblks[jb], axis=0, keepdims=True) * (1.0 / MOBA_BLOCK)
        gate = jnp.sum(qf_ref[...] * kmean, axis=-1, keepdims=True)
        part_ref[n] = outs[jb]
        here = lane == n
        mx = jnp.where(here, ms[jb], mx)
        ls = jnp.where(here, lsums[jb], ls)
        gt = jnp.where(here, gate, gt)
    mx_ref[...] = mx
    l_ref[...] = ls
    gate_ref[...] = gt

    @pl.when(part == sample.steps_per_seq - 1)
    def _own():
        pad = jnp.zeros((MOBA_BLOCK - dec_seq, KV_WIDTH), F32)
        (m_n,), (l_n,), (o_n,) = block_partials([jnp.concatenate([knew_ref[...], pad], axis=0)],
                                                [jnp.concatenate([vnew_ref[...], pad], axis=0)],
                                                [bias_own_ref[0]])
        own = lane == n_past
        mx = jnp.where(own, m_n, mx_ref[...])
        ls = jnp.where(own, l_n, l_ref[...])
        sel = _select_top_blocks(gate_ref[...], lane, n_past)
        include = sel | own
        m_tot = jnp.max(jnp.where(include, mx, GATE_MASKED), axis=-1, keepdims=True)
        e = jnp.where(include, jnp.exp(jnp.minimum(mx - m_tot, 0.0)), 0.0)
        denom = jnp.sum(e * ls, axis=-1, keepdims=True)
        num = e[:, n_past:n_past + 1] * o_n
        for j in range(n_past):
            num = num + e[:, j:j + 1] * part_ref[j]
        attn = num / denom
        for h in range(N_HEADS):
            cols = slice(h * HEAD_DIM, (h + 1) * HEAD_DIM)
            o_ref[0, :, cols] = (attn[h * dec_seq:(h + 1) * dec_seq, :] * _silu(ga_ref[:, cols])).astype(BF16)


def _prompt_bias_tables(rel_bias, n_blk):
    blk = MOBA_BLOCK
    r = jnp.arange(blk, dtype=jnp.int32)
    dist = (jnp.arange(TABLE_DELTAS, dtype=jnp.int32)[:, None, None] * blk
            + r[None, :, None] - r[None, None, :])
    tab = _bias_of_bucket(rel_bias, _rel_bucket(dist), 1)
    tab = jnp.where((dist >= 0)[:, None], tab * LOG2_E, NEG_INF)

    nd = max(n_blk, TABLE_DELTAS + 1)
    off = np.arange(-(blk - 1), blk, dtype=np.int32)
    dist_np = np.arange(nd, dtype=np.int32)[:, None] * blk + off[None, :]
    buckets_np = _rel_bucket_np(dist_np)
    span = buckets_np[TABLE_DELTAS:, -1] - buckets_np[TABLE_DELTAS:, 0]
    assert np.all(np.diff(buckets_np[TABLE_DELTAS:], axis=1) >= 0) and np.all(span <= 1), \
        "more than one bias bucket boundary inside a far block pair"
    buckets = _rel_bucket(jnp.asarray(dist_np))
    b_lo, b_hi = buckets[:, 0], buckets[:, -1]
    first_hi = jnp.argmax(buckets == b_hi[:, None], axis=1).astype(jnp.int32)
    thr = first_hi - (blk - 1)
    blo = (rel_bias.astype(F32)[b_lo] * LOG2_E).reshape(-1)
    bhi = (rel_bias.astype(F32)[b_hi] * LOG2_E).reshape(-1)
    return tab, thr, blo, bhi


def _sample_bias_table(rel_bias, past_len, dec_seq, n_blocks):
    kpos = jnp.arange(n_blocks * MOBA_BLOCK, dtype=jnp.int32).reshape(n_blocks, 1, MOBA_BLOCK)
    qpos = (past_len + jnp.arange(dec_seq, dtype=jnp.int32)).reshape(1, dec_seq, 1)
    dist = qpos - kpos
    vals = _bias_of_bucket(rel_bias, _rel_bucket(dist), 1)
    vals = jnp.where((dist >= 0)[:, None], vals, NEG_INF)
    return vals.reshape(n_blocks, N_HEADS * dec_seq, MOBA_BLOCK)


def _attention(z, kmean_pad, kaug, vaug, rel_bias, seq, ga_col0,
               q_s, k_s, v_s, ga_s, cache_k, cache_v, page_table, past_len):
    blk = MOBA_BLOCK
    n_blk = seq // blk
    assert n_blk <= LANES
    tab, thr, blo, bhi = _prompt_bias_tables(rel_bias, n_blk)
    gc = ga_col0 // GROUP_WIDTH

    bsz, n_pages = page_table.shape
    dec_seq = q_s.shape[0] // bsz
    n_past = past_len // blk
    bps = math.gcd(SAMPLE_BLOCKS_PER_STEP, n_past)
    assert past_len % blk == 0 and n_past + 1 <= LANES and n_pages == n_past * PAGES_PER_BLOCK
    sample = _SampleShape(bsz, dec_seq, n_past, bps, n_past // bps)
    sample_steps = sample.n_seq * sample.steps_per_seq
    assert sample_steps <= N_KV_HEADS * n_blk, "sample attention chunks must fit in the prompt grid"
    pages_per_step = bps * PAGES_PER_BLOCK
    sbias = _sample_bias_table(rel_bias, past_len, dec_seq, n_past + 1)
    rows = N_HEADS * dec_seq

    def sample_pos(g, i):
        s = jnp.minimum(g * n_blk + i, sample_steps - 1)
        return s // sample.steps_per_seq, s % sample.steps_per_seq

    def page_spec(which):
        def index_map(g, i, th, pt):
            b, part = sample_pos(g, i)
            return (pt[b * n_pages + part * pages_per_step + which], 0, 0)
        return pl.BlockSpec((1, PAGE_SIZE * N_KV_HEADS, HEAD_DIM), index_map)

    row_spec = lambda width: pl.BlockSpec((dec_seq, width), lambda g, i, th, pt: (sample_pos(g, i)[0], 0))
    resident = dict(pipeline_mode=pl.Buffered(1))
    smem = pl.BlockSpec(memory_space=pltpu.SMEM)
    grid_spec = pltpu.PrefetchScalarGridSpec(
        num_scalar_prefetch=2,
        grid=(N_KV_HEADS, n_blk),
        in_specs=([
            pl.BlockSpec((blk, GROUP_WIDTH), lambda g, i, th, pt: (i, g)),
            pl.BlockSpec((1, seq, HEAD_DIM + LANES), lambda g, i, th, pt: (g, 0, 0), **resident),
            pl.BlockSpec((1, seq, HEAD_DIM + LANES), lambda g, i, th, pt: (g, 0, 0), **resident),
            pl.BlockSpec((LANES, HEAD_DIM), lambda g, i, th, pt: (0, g)),
            pl.BlockSpec((blk, GROUP_WIDTH), lambda g, i, th, pt: (i, gc + g)),
            pl.BlockSpec((TABLE_DELTAS, KV_GROUP, blk, blk), lambda g, i, th, pt: (0, g, 0, 0), **resident),
            smem, smem,
            row_spec(ATTN_WIDTH), row_spec(KV_WIDTH), row_spec(KV_WIDTH), row_spec(ATTN_WIDTH)]
            + [page_spec(j) for j in range(pages_per_step)]
            + [page_spec(j) for j in range(pages_per_step)]
            + [pl.BlockSpec((bps, rows, blk), lambda g, i, th, pt: (sample_pos(g, i)[1], 0, 0)),
               pl.BlockSpec((1, rows, blk), lambda g, i, th, pt: (n_past, 0, 0))]),
        out_specs=[pl.BlockSpec((blk, GROUP_WIDTH), lambda g, i, th, pt: (i, g)),
                   pl.BlockSpec((1, dec_seq, ATTN_WIDTH), lambda g, i, th, pt: (sample_pos(g, i)[0], 0, 0))],
        scratch_shapes=[
            pltpu.VMEM((KV_GROUP * blk, HEAD_DIM + LANES), BF16),
            pltpu.VMEM((KV_GROUP * blk, LANES), F32),
            pltpu.VMEM((KV_GROUP * blk, LANES), F32),
            pltpu.VMEM((KV_GROUP * blk, HEAD_DIM), F32),
            pltpu.VMEM((blk, blk), jnp.int32),
            pltpu.VMEM((rows, KV_WIDTH), F32),
            pltpu.VMEM((rows, KV_WIDTH), BF16),
            pltpu.VMEM((rows, LANES), F32),
            pltpu.VMEM((rows, LANES), F32),
            pltpu.VMEM((rows, LANES), F32),
            pltpu.VMEM((n_past, rows, HEAD_DIM), F32),
        ],
    )
    ya, ya_s = pl.pallas_call(
        functools.partial(_attn_kernel, n_blk_pad=-(-n_blk // 8) * 8, sample=sample),
        grid_spec=grid_spec,
        out_shape=[jax.ShapeDtypeStruct((seq, ATTN_WIDTH), BF16),
                   jax.ShapeDtypeStruct((bsz, dec_seq, ATTN_WIDTH), BF16)],
        compiler_params=pltpu.CompilerParams(
            dimension_semantics=("arbitrary", "arbitrary"), vmem_limit_bytes=VMEM_LIMIT),
    )(thr, page_table.reshape(-1), z, kaug, vaug, kmean_pad, z, tab, blo, bhi, q_s, k_s, v_s, ga_s,
      *([cache_k] * pages_per_step), *([cache_v] * pages_per_step), sbias, sbias)
    return ya, ya_s.reshape(bsz * dec_seq, ATTN_WIDTH)


def _window_mix(ext_ref, window, rows, cols, count, u, gp, w, scale):
    win = u
    for j in range(1, window):
        win = win + ext_ref[POOL_HALO - j:POOL_HALO - j + rows, cols]
    zed = win / count - u
    y = jnp.dot(zed.astype(BF16), w, preferred_element_type=F32)
    return (y * scale) * _silu(gp)


def _pool_prompt_kernel(prev_ref, u_ref, gp_ref, w_ref, scale_ref, o_ref, ext_ref, *, groups_per_step):
    i = pl.program_id(0)
    rows = u_ref.shape[0]
    gw = w_ref.shape[1]
    ext_ref[0:POOL_HALO, :] = jnp.where(i == 0, 0.0, prev_ref[...])
    ext_ref[POOL_HALO:POOL_HALO + rows, :] = u_ref[...]
    pos = i * rows + lax.broadcasted_iota(jnp.int32, (rows, 1), 0)
    for part in range(N_POOL_GROUPS // groups_per_step):
        @pl.when(pl.program_id(1) == part)
        def _(part=part):
            for gl in range(groups_per_step):
                window = POOL_WINDOWS[part * groups_per_step + gl]
                cols = slice(gl * gw, (gl + 1) * gw)
                count = jnp.minimum(window, pos + 1).astype(F32)
                y = _window_mix(ext_ref, window, rows, cols, count, u_ref[:, cols], gp_ref[:, cols],
                                w_ref[gl], scale_ref[:, cols])
                o_ref[:, cols] = y.astype(BF16)


def _pool_prompt(z, w_pool_bf16, pool_scale, seq, u_col0, gp_col0, tm):
    n_groups, gw, _ = w_pool_bf16.shape
    gps = 2
    bw = gps * gw
    assert n_groups == N_POOL_GROUPS and u_col0 % bw == 0 and gp_col0 % bw == 0 and tm % POOL_HALO == 0
    uc, gc = u_col0 // bw, gp_col0 // bw
    halo_blocks = tm // POOL_HALO
    return pl.pallas_call(
        functools.partial(_pool_prompt_kernel, groups_per_step=gps),
        grid=(seq // tm, n_groups // gps),
        in_specs=[pl.BlockSpec((POOL_HALO, bw), lambda i, c: (jnp.maximum(i * halo_blocks - 1, 0), uc + c)),
                  pl.BlockSpec((tm, bw), lambda i, c: (i, uc + c)),
                  pl.BlockSpec((tm, bw), lambda i, c: (i, gc + c)),
                  pl.BlockSpec((gps, gw, gw), lambda i, c: (c, 0, 0)),
                  pl.BlockSpec((1, bw), lambda i, c: (0, c))],
        out_specs=pl.BlockSpec((tm, bw), lambda i, c: (i, c)),
        out_shape=jax.ShapeDtypeStruct((seq, n_groups * gw), BF16),
        scratch_shapes=[pltpu.VMEM((POOL_HALO + tm, bw), F32)],
        compiler_params=pltpu.CompilerParams(
            dimension_semantics=("parallel", "arbitrary"), vmem_limit_bytes=VMEM_LIMIT),
    )(z, z, z, w_pool_bf16, pool_scale.reshape(1, n_groups * gw))


def _pool_sample_kernel(ext_in_ref, gp_ref, w_ref, scale_ref, o_ref, ext_ref, *, bsz, dec_seq):
    seg = POOL_HALO + dec_seq
    rows = bsz * seg
    gw = w_ref.shape[1]
    ext_ref[0:POOL_HALO, :] = jnp.zeros((POOL_HALO, ext_ref.shape[1]), F32)
    ext_ref[POOL_HALO:POOL_HALO + rows, :] = ext_in_ref[...]
    for gi, window in enumerate(POOL_WINDOWS):
        cols = slice(gi * gw, (gi + 1) * gw)
        u_all = ext_in_ref[:, cols]
        win = u_all
        for j in range(1, window):
            win = win + ext_ref[POOL_HALO - j:POOL_HALO - j + rows, cols]
        zed = (win / float(window) - u_all).reshape(bsz, seg, gw)[:, POOL_HALO:, :].reshape(bsz * dec_seq, gw)
        y = jnp.dot(zed.astype(BF16), w_ref[gi], preferred_element_type=F32)
        o_ref[:, cols] = ((y * scale_ref[:, cols]) * _silu(gp_ref[:, cols])).astype(BF16)


def _pool_sample(u_ext, gp_s, w_pool_bf16, pool_scale, bsz, dec_seq):
    pw = u_ext.shape[1]
    rows = u_ext.shape[0]
    vm = pl.BlockSpec(memory_space=pltpu.VMEM)
    return pl.pallas_call(
        functools.partial(_pool_sample_kernel, bsz=bsz, dec_seq=dec_seq),
        in_specs=[vm, vm, vm, vm],
        out_specs=vm,
        out_shape=jax.ShapeDtypeStruct((bsz * dec_seq, pw), BF16),
        scratch_shapes=[pltpu.VMEM((POOL_HALO + rows, pw), F32)],
        compiler_params=pltpu.CompilerParams(vmem_limit_bytes=VMEM_LIMIT),
    )(u_ext, gp_s, w_pool_bf16, pool_scale.reshape(1, pw))


def _outproj_kernel(ya_ref, yp_ref, w_ref, x_ref, gain_ref, o_ref, *, k_attn):
    k = pl.program_id(1)

    @pl.when(k == 0)
    def _():
        o_ref[...] = x_ref[...]

    @pl.when(k < k_attn)
    def _():
        o_ref[...] += jnp.dot(ya_ref[...], w_ref[...], preferred_element_type=F32)

    @pl.when(k >= k_attn)
    def _():
        o_ref[...] += jnp.dot(yp_ref[...], w_ref[...], preferred_element_type=F32)

    @pl.when(k == pl.num_programs(1) - 1)
    def _():
        h = o_ref[...]
        ms = jnp.mean(h * h, axis=-1, keepdims=True)
        o_ref[...] = h * lax.rsqrt(ms + RMS_EPS) * gain_ref[...]


def _out_projection(ya, yp, w_bf16, x, gain, tm, tk):
    m, d = x.shape
    ka, kp = ya.shape[1] // tk, yp.shape[1] // tk
    return pl.pallas_call(
        functools.partial(_outproj_kernel, k_attn=ka),
        grid=(m // tm, ka + kp),
        in_specs=[pl.BlockSpec((tm, tk), lambda i, k: (i, jnp.minimum(k, ka - 1))),
                  pl.BlockSpec((tm, tk), lambda i, k: (i, jnp.maximum(k - ka, 0))),
                  pl.BlockSpec((tk, d), lambda i, k: (k, 0)),
                  pl.BlockSpec((tm, d), lambda i, k: (i, 0)),
                  pl.BlockSpec((1, d), lambda i, k: (0, 0))],
        out_specs=pl.BlockSpec((tm, d), lambda i, k: (i, 0)),
        out_shape=jax.ShapeDtypeStruct((m, d), F32),
        compiler_params=pltpu.CompilerParams(
            dimension_semantics=("parallel", "arbitrary"), vmem_limit_bytes=VMEM_LIMIT),
    )(ya, yp, w_bf16, x, gain.reshape(1, d))


def kernel(x_prompt, x_sample, cache_k, cache_v, state_pool, page_table, norm_in, w_in, w_pool,
           pool_scale, w_out, rel_bias, norm_out):
    depth = norm_in.shape[0]
    assert depth == 1, "single layer step"
    n_batch, seq, d_model = x_prompt.shape
    bsz, dec_seq, _ = x_sample.shape
    assert n_batch == 1 and seq % (8 * MOBA_BLOCK) == 0 and dec_seq == 8
    past_len = page_table.shape[1] * PAGE_SIZE
    assert past_len >= POOL_HIST
    pool_width = pool_scale.shape[1]
    k0 = ATTN_WIDTH
    v0 = k0 + KV_WIDTH
    ga0 = v0 + KV_WIDTH
    u0 = ga0 + ATTN_WIDTH
    gp0 = u0 + pool_width
    assert w_in.shape[2] == gp0 + pool_width and w_out.shape[1] == ATTN_WIDTH + pool_width

    w_in_b = w_in[0].astype(BF16)
    w_out_b = w_out[0].astype(BF16)
    w_pool_b = w_pool[0].astype(BF16)
    xp = x_prompt[0]
    xs = x_sample.reshape(bsz * dec_seq, d_model)

    z = _in_projection(xp, norm_in[0], w_in_b, tm=512, tn=1024)
    zs = _in_projection(xs, norm_in[0], w_in_b, tm=bsz * dec_seq, tn=1024)
    q_s, k_s, v_s = zs[:, :k0], zs[:, k0:v0], zs[:, v0:ga0]
    ga_s, u_s, gp_s = zs[:, ga0:u0], zs[:, u0:gp0], zs[:, gp0:]

    n_blk = seq // MOBA_BLOCK
    kmean, kaug, vaug = _stage_kv(z, seq, k0, v0)
    kmean_pad = jnp.pad(kmean, ((0, LANES - n_blk), (0, 0)))
    n_phys = cache_k.shape[1]
    page_rows = PAGE_SIZE * N_KV_HEADS
    ya, ya_s = _attention(z, kmean_pad, kaug, vaug, rel_bias, seq, ga0, q_s, k_s, v_s, ga_s,
                          cache_k.reshape(n_phys, page_rows, HEAD_DIM),
                          cache_v.reshape(n_phys, page_rows, HEAD_DIM), page_table, past_len)
    yp = _pool_prompt(z, w_pool_b, pool_scale[0], seq, u0, gp0, tm=256)
    y_prompt = _out_projection(ya, yp, w_out_b, xp, norm_out, tm=512, tk=512)

    u_s3 = u_s.reshape(bsz, dec_seq, pool_width)
    u_ext = jnp.concatenate([jnp.zeros((bsz, 1, pool_width), F32), state_pool[0].astype(F32), u_s3], axis=1)
    yp_s = _pool_sample(u_ext.reshape(bsz * (POOL_HALO + dec_seq), pool_width), gp_s, w_pool_b,
                        pool_scale[0], bsz, dec_seq)
    y_sample = _out_projection(ya_s, yp_s, w_out_b, xs, norm_out, tm=bsz * dec_seq, tk=512)

    k_prompt = z[:, k0:v0].reshape(1, 1, seq, N_KV_HEADS, HEAD_DIM)
    v_prompt = z[:, v0:ga0].reshape(1, 1, seq, N_KV_HEADS, HEAD_DIM)
    pool_prompt = z[seq - POOL_HIST:, u0:gp0].reshape(1, 1, POOL_HIST, pool_width)
    k_sample = k_s.reshape(1, bsz, dec_seq, N_KV_HEADS, HEAD_DIM)
    v_sample = v_s.reshape(1, bsz, dec_seq, N_KV_HEADS, HEAD_DIM)
    pool_sample = u_ext[:, 1 + dec_seq:, :].reshape(1, bsz, POOL_HIST, pool_width).astype(state_pool.dtype)
    return (y_prompt.reshape(1, seq, d_model), y_sample.reshape(bsz, dec_seq, d_model),
            k_prompt, v_prompt, pool_prompt, k_sample, v_sample, pool_sample)
```

```python
import functools
import math
from typing import NamedTuple

import numpy as np
import jax
import jax.numpy as jnp
from jax import lax
from jax.experimental import pallas as pl
from jax.experimental.pallas import tpu as pltpu

F32 = jnp.float32
BF16 = jnp.bfloat16

HEAD_DIM = 128
N_HEADS = 16
N_KV_HEADS = 4
KV_GROUP = N_HEADS // N_KV_HEADS
ATTN_WIDTH = N_HEADS * HEAD_DIM
KV_WIDTH = N_KV_HEADS * HEAD_DIM
GROUP_WIDTH = KV_GROUP * HEAD_DIM
ATTN_SCALE = HEAD_DIM ** -0.5
LOG2_E = math.log2(math.e)
MOBA_BLOCK = 256
MOBA_TOPK = 3
PAGE_SIZE = 128
PAGES_PER_BLOCK = MOBA_BLOCK // PAGE_SIZE
POOL_WINDOWS = (2, 4, 8, 16)
N_POOL_GROUPS = len(POOL_WINDOWS)
POOL_HIST = max(POOL_WINDOWS) - 1
POOL_HALO = POOL_HIST + 1
NUM_BUCKETS = 32
MAX_EXACT = NUM_BUCKETS // 2
MAX_DISTANCE = 4096
RMS_EPS = 1e-6
NEG_INF = -1e30
GATE_MASKED = -3.0e38
GATE_VALID_MIN = -1.0e38
LANES = 128
TABLE_DELTAS = 3
SWEEP_ROWS = 512
FAR_GROUP_BLOCKS = 2
FAR_GROUPS = 2
SAMPLE_BLOCKS_PER_STEP = 8

VMEM_LIMIT = 56 * 1024 * 1024


def _rel_bucket(dist):
    n = jnp.maximum(dist, 0)
    nf = jnp.maximum(n, 1).astype(F32)
    large = MAX_EXACT + (jnp.log(nf / MAX_EXACT) / math.log(MAX_DISTANCE / MAX_EXACT)
                         * (NUM_BUCKETS - MAX_EXACT)).astype(jnp.int32)
    large = jnp.minimum(large, NUM_BUCKETS - 1)
    return jnp.where(n < MAX_EXACT, n, large)


def _rel_bucket_np(dist):
    n = np.maximum(dist, 0)
    nf = np.maximum(n, 1).astype(np.float32)
    large = MAX_EXACT + (np.log(nf / np.float32(MAX_EXACT)) / np.float32(math.log(MAX_DISTANCE / MAX_EXACT))
                         * np.float32(NUM_BUCKETS - MAX_EXACT)).astype(np.int32)
    large = np.minimum(large, NUM_BUCKETS - 1)
    return np.where(n < MAX_EXACT, n, large)


def _bias_of_bucket(rel_bias, bucket, bucket_axis):
    onehot = (jnp.expand_dims(bucket, bucket_axis)
              == jnp.arange(NUM_BUCKETS, dtype=jnp.int32).reshape(
                  (NUM_BUCKETS,) + (1,) * (bucket.ndim - bucket_axis))).astype(F32)
    out = jnp.tensordot(rel_bias.astype(F32).T, onehot, axes=((1,), (bucket_axis,)),
                        precision=lax.Precision.HIGHEST)
    return jnp.moveaxis(out, 0, bucket_axis)


def _silu(x):
    return x * jax.nn.sigmoid(x)


def _dot_nt(a, b, precision=None):
    return lax.dot_general(a, b, (((1,), (1,)), ((), ())), precision=precision,
                           preferred_element_type=F32)


def _select_top_blocks(gate, block_id, n_past, axis=-1):
    gate = jnp.where(block_id < n_past, gate, GATE_MASKED)
    id_f = block_id.astype(F32)
    sel = jnp.zeros(gate.shape, jnp.bool_)
    for _ in range(MOBA_TOPK):
        mx = jnp.max(gate, axis=axis, keepdims=True)
        first = jnp.min(jnp.where(gate == mx, id_f, float(LANES)), axis=axis, keepdims=True)
        pick = (id_f == first) & (mx > GATE_VALID_MIN)
        sel = sel | pick
        gate = jnp.where(pick, GATE_MASKED, gate)
    return sel


def _inproj_kernel(x_ref, gain_ref, w_ref, o_ref, xn_ref, *, row_chunk):
    @pl.when(pl.program_id(1) == 0)
    def _():
        for r in range(0, x_ref.shape[0], row_chunk):
            x = x_ref[r:r + row_chunk, :]
            ms = jnp.mean(x * x, axis=-1, keepdims=True)
            xn_ref[r:r + row_chunk, :] = (x * lax.rsqrt(ms + RMS_EPS) * gain_ref[...]).astype(BF16)

    o_ref[...] = jnp.dot(xn_ref[...], w_ref[...], preferred_element_type=F32)


def _in_projection(x, gain, w_bf16, tm, tn):
    m, d = x.shape
    n = w_bf16.shape[1]
    return pl.pallas_call(
        functools.partial(_inproj_kernel, row_chunk=min(tm, 128)),
        grid=(m // tm, n // tn),
        in_specs=[pl.BlockSpec((tm, d), lambda i, j: (i, 0)),
                  pl.BlockSpec((1, d), lambda i, j: (0, 0)),
                  pl.BlockSpec((d, tn), lambda i, j: (0, j))],
        out_specs=pl.BlockSpec((tm, tn), lambda i, j: (i, j)),
        out_shape=jax.ShapeDtypeStruct((m, n), F32),
        scratch_shapes=[pltpu.VMEM((tm, d), BF16)],
        compiler_params=pltpu.CompilerParams(
            dimension_semantics=("parallel", "arbitrary"), vmem_limit_bytes=VMEM_LIMIT),
    )(x, gain.reshape(1, d), w_bf16)


def _stage_kv_kernel(k_ref, v_ref, km_ref, kaug_ref, vaug_ref, *, blocks_per_step):
    rows = blocks_per_step * MOBA_BLOCK
    k = k_ref[...]
    km_ref[...] = jnp.sum(k.reshape(blocks_per_step, MOBA_BLOCK, KV_WIDTH), axis=1) * (1.0 / MOBA_BLOCK)
    block_of_row = (pl.program_id(0) * blocks_per_step
                    + lax.broadcasted_iota(jnp.int32, (rows, LANES), 0) // MOBA_BLOCK)
    onehot = jnp.where(lax.broadcasted_iota(jnp.int32, (rows, LANES), 1) == block_of_row, 1.0, 0.0).astype(BF16)
    ones = jnp.ones((rows, LANES), BF16)
    for gi in range(N_KV_HEADS):
        cols = slice(gi * HEAD_DIM, (gi + 1) * HEAD_DIM)
        kaug_ref[gi, :, 0:HEAD_DIM] = k[:, cols].astype(BF16)
        kaug_ref[gi, :, HEAD_DIM:HEAD_DIM + LANES] = onehot
        vaug_ref[gi, :, 0:HEAD_DIM] = v_ref[:, cols].astype(BF16)
        vaug_ref[gi, :, HEAD_DIM:HEAD_DIM + LANES] = ones


def _stage_kv(z, seq, k_col0, v_col0):
    bps = 8
    n_blk = seq // MOBA_BLOCK
    assert n_blk % bps == 0 and k_col0 % KV_WIDTH == 0 and v_col0 % KV_WIDTH == 0
    rows = bps * MOBA_BLOCK
    aug = jax.ShapeDtypeStruct((N_KV_HEADS, seq, HEAD_DIM + LANES), BF16)
    aug_spec = pl.BlockSpec((N_KV_HEADS, rows, HEAD_DIM + LANES), lambda i: (0, i, 0))
    return pl.pallas_call(
        functools.partial(_stage_kv_kernel, blocks_per_step=bps),
        grid=(n_blk // bps,),
        in_specs=[pl.BlockSpec((rows, KV_WIDTH), lambda i: (i, k_col0 // KV_WIDTH)),
                  pl.BlockSpec((rows, KV_WIDTH), lambda i: (i, v_col0 // KV_WIDTH))],
        out_specs=[pl.BlockSpec((bps, KV_WIDTH), lambda i: (i, 0)), aug_spec, aug_spec],
        out_shape=[jax.ShapeDtypeStruct((n_blk, KV_WIDTH), F32), aug, aug],
        compiler_params=pltpu.CompilerParams(vmem_limit_bytes=VMEM_LIMIT),
    )(z, z)


class _SampleShape(NamedTuple):
    n_seq: int
    dec_seq: int
    n_past: int
    blocks_per_step: int
    steps_per_seq: int


def _attn_kernel(thr_ref, pt_ref,
                 q_ref, kaug_ref, vaug_ref, km_ref, ga_ref, tab_ref, blo_ref, bhi_ref,
                 qs_ref, knew_ref, vnew_ref, gas_ref, *rest, n_blk_pad, sample):
    n_cache_pages = sample.blocks_per_step * PAGES_PER_BLOCK
    k_pages, v_pages = rest[:n_cache_pages], rest[n_cache_pages:2 * n_cache_pages]
    (sbias_ref, sbias_own_ref, o_ref, so_ref, qaug_ref, m_ref, l_ref, acc_ref, dmat_ref,
     qf_ref, qb_ref, gate_ref, mx_ref, sl_ref, part_ref) = rest[2 * n_cache_pages:]
    g = pl.program_id(0)
    i = pl.program_id(1)
    blk = MOBA_BLOCK
    all_rows = KV_GROUP * blk

    def select_blocks():
        dmat_ref[...] = (lax.broadcasted_iota(jnp.int32, (blk, blk), 0)
                         - lax.broadcasted_iota(jnp.int32, (blk, blk), 1))
        m_ref[...] = jnp.full(m_ref.shape, NEG_INF, F32)
        l_ref[...] = jnp.zeros(l_ref.shape, F32)
        acc_ref[...] = jnp.zeros(acc_ref.shape, F32)
        q_all = jnp.concatenate(
            [q_ref[:, hh * HEAD_DIM:(hh + 1) * HEAD_DIM] for hh in range(KV_GROUP)], axis=0)
        gate_t = _dot_nt(km_ref[0:n_blk_pad, :], q_all, precision=lax.Precision.HIGHEST)
        blk_id = lax.broadcasted_iota(jnp.int32, (n_blk_pad, all_rows), 0)
        sel_t = _select_top_blocks(gate_t, blk_id, i, axis=0)
        mask_t = jnp.where(sel_t | (blk_id == i), 0.0, NEG_INF)
        if n_blk_pad < LANES:
            mask_t = jnp.concatenate([mask_t, jnp.zeros((LANES - n_blk_pad, all_rows), F32)], axis=0)
        qaug_ref[:, 0:HEAD_DIM] = (q_all * (ATTN_SCALE * LOG2_E)).astype(BF16)
        qaug_ref[:, HEAD_DIM:HEAD_DIM + LANES] = mask_t.T.astype(BF16)

    step = g * pl.num_programs(1) + i
    part = step % sample.steps_per_seq
    sample_init, sample_past, sample_own = _sample_phases(
        part, sample, qs_ref, knew_ref, vnew_ref, gas_ref, k_pages, v_pages, sbias_ref, sbias_own_ref,
        so_ref, qf_ref, qb_ref, gate_ref, mx_ref, sl_ref, part_ref)
    sample_steps = sample.n_seq * sample.steps_per_seq
    if sample_steps == N_KV_HEADS * (kaug_ref.shape[1] // blk):
        pl.when(part == 0)(sample_init)
        select_blocks()
        sample_past()
        pl.when(part == sample.steps_per_seq - 1)(sample_own)
    else:
        @pl.when(step < sample_steps)
        def _sample_chunk():
            pl.when(part == 0)(sample_init)
            sample_past()
            pl.when(part == sample.steps_per_seq - 1)(sample_own)

        select_blocks()

    def sweep(groups):
        m_all, l_all, acc_all = m_ref[...], l_ref[...], acc_ref[...]
        chunks = [slice(c, c + SWEEP_ROWS) for c in range(0, all_rows, SWEEP_ROWS)]
        m_c = [m_all[rows] for rows in chunks]
        l_c = [l_all[rows] for rows in chunks]
        acc_c = [acc_all[rows] for rows in chunks]
        logits = []
        for n0, nb, bias in groups:
            r = pl.multiple_of(n0 * blk, blk)
            keys = kaug_ref[0, pl.ds(r, nb * blk), :]
            logits.append([_dot_nt(qaug_ref[rows, :], keys) + bias[rows] for rows in chunks])
        for (n0, nb, bias), group_logits in zip(groups, logits):
            r = pl.multiple_of(n0 * blk, blk)
            values = vaug_ref[0, pl.ds(r, nb * blk), :]
            for ci, rows in enumerate(chunks):
                sc = group_logits[ci]
                m_new = jnp.maximum(m_c[ci], jnp.max(sc, axis=-1, keepdims=True))
                alpha = jnp.exp2(m_c[ci] - m_new)
                p = jnp.exp2(sc - jnp.concatenate([m_new] * (nb * blk // LANES), axis=1))
                pv = jnp.dot(p.astype(BF16), values, preferred_element_type=F32)
                l_c[ci] = alpha * l_c[ci] + pv[:, HEAD_DIM:]
                acc_c[ci] = alpha * acc_c[ci] + pv[:, :HEAD_DIM]
                m_c[ci] = m_new
        m_ref[...] = jnp.concatenate(m_c, axis=0)
        l_ref[...] = jnp.concatenate(l_c, axis=0)
        acc_ref[...] = jnp.concatenate(acc_c, axis=0)

    def far_bias(n0, nb):
        tiles = []
        for j in range(nb):
            delta = i - (n0 + j)
            high = dmat_ref[...] >= thr_ref[delta]
            base = delta * N_HEADS + g * KV_GROUP
            tiles.append(jnp.concatenate([jnp.where(high, bhi_ref[base + hh], blo_ref[base + hh])
                                          for hh in range(KV_GROUP)], axis=0))
        return jnp.concatenate(tiles, axis=1)

    n_far = jnp.maximum(i - (TABLE_DELTAS - 1), 0)
    per_iter = FAR_GROUPS * FAR_GROUP_BLOCKS

    def far_step(t, carry):
        first = t * per_iter
        sweep([(first + j * FAR_GROUP_BLOCKS, FAR_GROUP_BLOCKS,
                far_bias(first + j * FAR_GROUP_BLOCKS, FAR_GROUP_BLOCKS)) for j in range(FAR_GROUPS)])
        return carry

    lax.fori_loop(0, n_far // per_iter, far_step, 0)

    left = n_far % per_iter
    near = jnp.minimum(i + 1, TABLE_DELTAS)
    for count in range(1, TABLE_DELTAS + 1):
        for rem in range(per_iter if count == TABLE_DELTAS else 1):
            @pl.when((near == count) & (left == rem))
            def _tail(count=count, rem=rem):
                groups = []
                for first in range(0, rem, FAR_GROUP_BLOCKS):
                    nb = min(FAR_GROUP_BLOCKS, rem - first)
                    groups.append((n_far - rem + first, nb, far_bias(n_far - rem + first, nb)))
                groups.append((i - (count - 1), count, jnp.concatenate(
                    [tab_ref[count - 1 - j].reshape(all_rows, blk) for j in range(count)], axis=1)))
                sweep(groups)

    attn = acc_ref[...] / l_ref[...]
    for hh in range(KV_GROUP):
        cols = slice(hh * HEAD_DIM, (hh + 1) * HEAD_DIM)
        o_ref[:, cols] = (attn[hh * blk:(hh + 1) * blk, :] * _silu(ga_ref[:, cols])).astype(BF16)


def _block_of_pages(page_refs):
    return jnp.concatenate(
        [jnp.concatenate([ref[0, pl.ds(gi, PAGE_SIZE, stride=N_KV_HEADS), :] for gi in range(N_KV_HEADS)],
                         axis=1) for ref in page_refs], axis=0)


def _sample_phases(part, sample, q_ref, knew_ref, vnew_ref, ga_ref, k_pages, v_pages, bias_ref, bias_own_ref,
                   o_ref, qf_ref, qb_ref, gate_ref, mx_ref, l_ref, part_ref):
    dec_seq, n_past, blocks_per_step = sample.dec_seq, sample.n_past, sample.blocks_per_step
    rows = N_HEADS * dec_seq
    grp_rows = KV_GROUP * dec_seq

    def _init():
        gate_ref[...] = jnp.zeros(gate_ref.shape, F32)
        mx_ref[...] = jnp.zeros(mx_ref.shape, F32)
        l_ref[...] = jnp.zeros(l_ref.shape, F32)
        col_grp = lax.broadcasted_iota(jnp.int32, (dec_seq, KV_WIDTH), 1) // HEAD_DIM
        for h in range(N_HEADS):
            qh = q_ref[:, h * HEAD_DIM:(h + 1) * HEAD_DIM]
            wide = jnp.concatenate([qh] * N_KV_HEADS, axis=1)
            wide = jnp.where(col_grp == h // KV_GROUP, wide, 0.0)
            qf_ref[h * dec_seq:(h + 1) * dec_seq, :] = wide
            qb_ref[h * dec_seq:(h + 1) * dec_seq, :] = (wide * ATTN_SCALE).astype(BF16)

    def block_partials(kblks, vblks, biases):
        scs = [_dot_nt(qb_ref[...], kblk.astype(BF16)) + bias for kblk, bias in zip(kblks, biases)]
        ms = [jnp.max(sc, axis=-1, keepdims=True) for sc in scs]
        ps = [jnp.exp(sc - m_n) for sc, m_n in zip(scs, ms)]
        ls = [jnp.sum(p, axis=-1, keepdims=True) for p in ps]
        outs = []
        for p, vblk in zip(ps, vblks):
            o_all = jnp.dot(p.astype(BF16), vblk.astype(BF16), preferred_element_type=F32)
            outs.append(jnp.concatenate(
                [o_all[gi * grp_rows:(gi + 1) * grp_rows, gi * HEAD_DIM:(gi + 1) * HEAD_DIM]
                 for gi in range(N_KV_HEADS)], axis=0))
        return ms, ls, outs

    def _past():
        lane = lax.broadcasted_iota(jnp.int32, (rows, LANES), 1)
        kblks = [_block_of_pages(k_pages[jb * PAGES_PER_BLOCK:(jb + 1) * PAGES_PER_BLOCK])
                 for jb in range(blocks_per_step)]
        vblks = [_block_of_pages(v_pages[jb * PAGES_PER_BLOCK:(jb + 1) * PAGES_PER_BLOCK])
                 for jb in range(blocks_per_step)]
        ms, lsums, outs = block_partials(kblks, vblks, [bias_ref[jb] for jb in range(blocks_per_step)])
        mx, ls, gt = mx_ref[...], l_ref[...], gate_ref[...]
        for jb in range(blocks_per_step):
            n = part * blocks_per_step + jb
            kmean = jnp.sum(kblks[jb], axis=0, keepdims=True) * (1.0 / MOBA_BLOCK)
            gate = jnp.sum(qf_ref[...] * kmean, axis=-1, keepdims=True)
            part_ref[n] = outs[jb]
            here = lane == n
            mx = jnp.where(here, ms[jb], mx)
            ls = jnp.where(here, lsums[jb], ls)
            gt = jnp.where(here, gate, gt)
        mx_ref[...] = mx
        l_ref[...] = ls
        gate_ref[...] = gt

    def _own():
        lane = lax.broadcasted_iota(jnp.int32, (rows, LANES), 1)
        pad = jnp.zeros((MOBA_BLOCK - dec_seq, KV_WIDTH), F32)
        (m_n,), (l_n,), (o_n,) = block_partials([jnp.concatenate([knew_ref[...], pad], axis=0)],
                                                [jnp.concatenate([vnew_ref[...], pad], axis=0)],
                                                [bias_own_ref[0]])
        own = lane == n_past
        mx = jnp.where(own, m_n, mx_ref[...])
        ls = jnp.where(own, l_n, l_ref[...])
        sel = _select_top_blocks(gate_ref[...], lane, n_past)
        include = sel | own
        m_tot = jnp.max(jnp.where(include, mx, GATE_MASKED), axis=-1, keepdims=True)
        e = jnp.where(include, jnp.exp(jnp.minimum(mx - m_tot, 0.0)), 0.0)
        denom = jnp.sum(e * ls, axis=-1, keepdims=True)
        num = e[:, n_past:n_past + 1] * o_n
        for j in range(n_past):
            num = num + e[:, j:j + 1] * part_ref[j]
        attn = num / denom
        for h in range(N_HEADS):
            cols = slice(h * HEAD_DIM, (h + 1) * HEAD_DIM)
            o_ref[0, :, cols] = (attn[h * dec_seq:(h + 1) * dec_seq, :] * _silu(ga_ref[:, cols])).astype(BF16)

    return _init, _past, _own


def _prompt_bias_tables(rel_bias, n_blk):
    blk = MOBA_BLOCK
    r = jnp.arange(blk, dtype=jnp.int32)
    dist = (jnp.arange(TABLE_DELTAS, dtype=jnp.int32)[:, None, None] * blk
            + r[None, :, None] - r[None, None, :])
    tab = _bias_of_bucket(rel_bias, _rel_bucket(dist), 1)
    tab = jnp.where((dist >= 0)[:, None], tab * LOG2_E, NEG_INF)

    nd = max(n_blk, TABLE_DELTAS + 1)
    off = np.arange(-(blk - 1), blk, dtype=np.int32)
    dist_np = np.arange(nd, dtype=np.int32)[:, None] * blk + off[None, :]
    buckets_np = _rel_bucket_np(dist_np)
    span = buckets_np[TABLE_DELTAS:, -1] - buckets_np[TABLE_DELTAS:, 0]
    assert np.all(np.diff(buckets_np[TABLE_DELTAS:], axis=1) >= 0) and np.all(span <= 1), \
        "more than one bias bucket boundary inside a far block pair"
    buckets = _rel_bucket(jnp.asarray(dist_np))
    b_lo, b_hi = buckets[:, 0], buckets[:, -1]
    first_hi = jnp.argmax(buckets == b_hi[:, None], axis=1).astype(jnp.int32)
    thr = first_hi - (blk - 1)
    blo = (rel_bias.astype(F32)[b_lo] * LOG2_E).reshape(-1)
    bhi = (rel_bias.astype(F32)[b_hi] * LOG2_E).reshape(-1)
    return tab, thr, blo, bhi


def _sample_bias_table(rel_bias, past_len, dec_seq, n_blocks):
    kpos = jnp.arange(n_blocks * MOBA_BLOCK, dtype=jnp.int32).reshape(n_blocks, 1, MOBA_BLOCK)
    qpos = (past_len + jnp.arange(dec_seq, dtype=jnp.int32)).reshape(1, dec_seq, 1)
    dist = qpos - kpos
    vals = _bias_of_bucket(rel_bias, _rel_bucket(dist), 1)
    vals = jnp.where((dist >= 0)[:, None], vals, NEG_INF)
    return vals.reshape(n_blocks, N_HEADS * dec_seq, MOBA_BLOCK)


def _attention(z, kmean_pad, kaug, vaug, rel_bias, seq, ga_col0,
               q_s, k_s, v_s, ga_s, cache_k, cache_v, page_table, past_len):
    blk = MOBA_BLOCK
    n_blk = seq // blk
    assert n_blk <= LANES
    tab, thr, blo, bhi = _prompt_bias_tables(rel_bias, n_blk)
    gc = ga_col0 // GROUP_WIDTH

    bsz, n_pages = page_table.shape
    dec_seq = q_s.shape[0] // bsz
    n_past = past_len // blk
    bps = math.gcd(SAMPLE_BLOCKS_PER_STEP, n_past)
    assert past_len % blk == 0 and n_past + 1 <= LANES and n_pages == n_past * PAGES_PER_BLOCK
    sample = _SampleShape(bsz, dec_seq, n_past, bps, n_past // bps)
    sample_steps = sample.n_seq * sample.steps_per_seq
    assert sample_steps <= N_KV_HEADS * n_blk, "sample attention chunks must fit in the prompt grid"
    pages_per_step = bps * PAGES_PER_BLOCK
    sbias = _sample_bias_table(rel_bias, past_len, dec_seq, n_past + 1)
    rows = N_HEADS * dec_seq

    def sample_pos(g, i):
        s = jnp.minimum(g * n_blk + i, sample_steps - 1)
        return s // sample.steps_per_seq, s % sample.steps_per_seq

    def page_spec(which):
        def index_map(g, i, th, pt):
            b, part = sample_pos(g, i)
            return (pt[b * n_pages + part * pages_per_step + which], 0, 0)
        return pl.BlockSpec((1, PAGE_SIZE * N_KV_HEADS, HEAD_DIM), index_map)

    row_spec = lambda width: pl.BlockSpec((dec_seq, width), lambda g, i, th, pt: (sample_pos(g, i)[0], 0))
    resident = dict(pipeline_mode=pl.Buffered(1))
    smem = pl.BlockSpec(memory_space=pltpu.SMEM)
    grid_spec = pltpu.PrefetchScalarGridSpec(
        num_scalar_prefetch=2,
        grid=(N_KV_HEADS, n_blk),
        in_specs=([
            pl.BlockSpec((blk, GROUP_WIDTH), lambda g, i, th, pt: (i, g)),
            pl.BlockSpec((1, seq, HEAD_DIM + LANES), lambda g, i, th, pt: (g, 0, 0), **resident),
            pl.BlockSpec((1, seq, HEAD_DIM + LANES), lambda g, i, th, pt: (g, 0, 0), **resident),
            pl.BlockSpec((LANES, HEAD_DIM), lambda g, i, th, pt: (0, g)),
            pl.BlockSpec((blk, GROUP_WIDTH), lambda g, i, th, pt: (i, gc + g)),
            pl.BlockSpec((TABLE_DELTAS, KV_GROUP, blk, blk), lambda g, i, th, pt: (0, g, 0, 0), **resident),
            smem, smem,
            row_spec(ATTN_WIDTH), row_spec(KV_WIDTH), row_spec(KV_WIDTH), row_spec(ATTN_WIDTH)]
            + [page_spec(j) for j in range(pages_per_step)]
            + [page_spec(j) for j in range(pages_per_step)]
            + [pl.BlockSpec((bps, rows, blk), lambda g, i, th, pt: (sample_pos(g, i)[1], 0, 0)),
               pl.BlockSpec((1, rows, blk), lambda g, i, th, pt: (n_past, 0, 0))]),
        out_specs=[pl.BlockSpec((blk, GROUP_WIDTH), lambda g, i, th, pt: (i, g)),
                   pl.BlockSpec((1, dec_seq, ATTN_WIDTH), lambda g, i, th, pt: (sample_pos(g, i)[0], 0, 0))],
        scratch_shapes=[
            pltpu.VMEM((KV_GROUP * blk, HEAD_DIM + LANES), BF16),
            pltpu.VMEM((KV_GROUP * blk, LANES), F32),
            pltpu.VMEM((KV_GROUP * blk, LANES), F32),
            pltpu.VMEM((KV_GROUP * blk, HEAD_DIM), F32),
            pltpu.VMEM((blk, blk), jnp.int32),
            pltpu.VMEM((rows, KV_WIDTH), F32),
            pltpu.VMEM((rows, KV_WIDTH), BF16),
            pltpu.VMEM((rows, LANES), F32),
            pltpu.VMEM((rows, LANES), F32),
            pltpu.VMEM((rows, LANES), F32),
            pltpu.VMEM((n_past, rows, HEAD_DIM), F32),
        ],
    )
    ya, ya_s = pl.pallas_call(
        functools.partial(_attn_kernel, n_blk_pad=-(-n_blk // 8) * 8, sample=sample),
        grid_spec=grid_spec,
        out_shape=[jax.ShapeDtypeStruct((seq, ATTN_WIDTH), BF16),
                   jax.ShapeDtypeStruct((bsz, dec_seq, ATTN_WIDTH), BF16)],
        compiler_params=pltpu.CompilerParams(
            dimension_semantics=("arbitrary", "arbitrary"), vmem_limit_bytes=VMEM_LIMIT),
    )(thr, page_table.reshape(-1), z, kaug, vaug, kmean_pad, z, tab, blo, bhi, q_s, k_s, v_s, ga_s,
      *([cache_k] * pages_per_step), *([cache_v] * pages_per_step), sbias, sbias)
    return ya, ya_s.reshape(bsz * dec_seq, ATTN_WIDTH)


def _window_mix(ext_ref, window, rows, cols, count, u, gp, w, scale):
    win = u
    for j in range(1, window):
        win = win + ext_ref[POOL_HALO - j:POOL_HALO - j + rows, cols]
    zed = win / count - u
    y = jnp.dot(zed.astype(BF16), w, preferred_element_type=F32)
    return (y * scale) * _silu(gp)


def _pool_prompt_kernel(prev_ref, u_ref, gp_ref, w_ref, scale_ref, o_ref, ext_ref, *, groups_per_step):
    i = pl.program_id(0)
    rows = u_ref.shape[0]
    gw = w_ref.shape[1]
    ext_ref[0:POOL_HALO, :] = jnp.where(i == 0, 0.0, prev_ref[...])
    ext_ref[POOL_HALO:POOL_HALO + rows, :] = u_ref[...]
    pos = i * rows + lax.broadcasted_iota(jnp.int32, (rows, 1), 0)
    for part in range(N_POOL_GROUPS // groups_per_step):
        @pl.when(pl.program_id(1) == part)
        def _(part=part):
            for gl in range(groups_per_step):
                window = POOL_WINDOWS[part * groups_per_step + gl]
                cols = slice(gl * gw, (gl + 1) * gw)
                count = jnp.minimum(window, pos + 1).astype(F32)
                y = _window_mix(ext_ref, window, rows, cols, count, u_ref[:, cols], gp_ref[:, cols],
                                w_ref[gl], scale_ref[:, cols])
                o_ref[:, cols] = y.astype(BF16)


def _pool_prompt(z, w_pool_bf16, pool_scale, seq, u_col0, gp_col0, tm):
    n_groups, gw, _ = w_pool_bf16.shape
    gps = 2
    bw = gps * gw
    assert n_groups == N_POOL_GROUPS and u_col0 % bw == 0 and gp_col0 % bw == 0 and tm % POOL_HALO == 0
    uc, gc = u_col0 // bw, gp_col0 // bw
    halo_blocks = tm // POOL_HALO
    return pl.pallas_call(
        functools.partial(_pool_prompt_kernel, groups_per_step=gps),
        grid=(seq // tm, n_groups // gps),
        in_specs=[pl.BlockSpec((POOL_HALO, bw), lambda i, c: (jnp.maximum(i * halo_blocks - 1, 0), uc + c)),
                  pl.BlockSpec((tm, bw), lambda i, c: (i, uc + c)),
                  pl.BlockSpec((tm, bw), lambda i, c: (i, gc + c)),
                  pl.BlockSpec((gps, gw, gw), lambda i, c: (c, 0, 0)),
                  pl.BlockSpec((1, bw), lambda i, c: (0, c))],
        out_specs=pl.BlockSpec((tm, bw), lambda i, c: (i, c)),
        out_shape=jax.ShapeDtypeStruct((seq, n_groups * gw), BF16),
        scratch_shapes=[pltpu.VMEM((POOL_HALO + tm, bw), F32)],
        compiler_params=pltpu.CompilerParams(
            dimension_semantics=("parallel", "arbitrary"), vmem_limit_bytes=VMEM_LIMIT),
    )(z, z, z, w_pool_bf16, pool_scale.reshape(1, n_groups * gw))


def _pool_sample_kernel(ext_in_ref, gp_ref, w_ref, scale_ref, o_ref, ext_ref, *, bsz, dec_seq):
    seg = POOL_HALO + dec_seq
    rows = bsz * seg
    gw = w_ref.shape[1]
    ext_ref[0:POOL_HALO, :] = jnp.zeros((POOL_HALO, ext_ref.shape[1]), F32)
    ext_ref[POOL_HALO:POOL_HALO + rows, :] = ext_in_ref[...]
    for gi, window in enumerate(POOL_WINDOWS):
        cols = slice(gi * gw, (gi + 1) * gw)
        u_all = ext_in_ref[:, cols]
        win = u_all
        for j in range(1, window):
            win = win + ext_ref[POOL_HALO - j:POOL_HALO - j + rows, cols]
        zed = (win / float(window) - u_all).reshape(bsz, seg, gw)[:, POOL_HALO:, :].reshape(bsz * dec_seq, gw)
        y = jnp.dot(zed.astype(BF16), w_ref[gi], preferred_element_type=F32)
        o_ref[:, cols] = ((y * scale_ref[:, cols]) * _silu(gp_ref[:, cols])).astype(BF16)


def _pool_sample(u_ext, gp_s, w_pool_bf16, pool_scale, bsz, dec_seq):
    pw = u_ext.shape[1]
    rows = u_ext.shape[0]
    vm = pl.BlockSpec(memory_space=pltpu.VMEM)
    return pl.pallas_call(
        functools.partial(_pool_sample_kernel, bsz=bsz, dec_seq=dec_seq),
        in_specs=[vm, vm, vm, vm],
        out_specs=vm,
        out_shape=jax.ShapeDtypeStruct((bsz * dec_seq, pw), BF16),
        scratch_shapes=[pltpu.VMEM((POOL_HALO + rows, pw), F32)],
        compiler_params=pltpu.CompilerParams(vmem_limit_bytes=VMEM_LIMIT),
    )(u_ext, gp_s, w_pool_bf16, pool_scale.reshape(1, pw))


def _outproj_kernel(ya_ref, yp_ref, w_ref, x_ref, gain_ref, o_ref, *, k_attn):
    k = pl.program_id(1)

    @pl.when(k == 0)
    def _():
        o_ref[...] = x_ref[...] + jnp.dot(ya_ref[...], w_ref[...], preferred_element_type=F32)

    @pl.when((k > 0) & (k < k_attn))
    def _():
        o_ref[...] += jnp.dot(ya_ref[...], w_ref[...], preferred_element_type=F32)

    @pl.when(k >= k_attn)
    def _():
        o_ref[...] += jnp.dot(yp_ref[...], w_ref[...], preferred_element_type=F32)

    @pl.when(k == pl.num_programs(1) - 1)
    def _():
        h = o_ref[...]
        ms = jnp.mean(h * h, axis=-1, keepdims=True)
        o_ref[...] = h * lax.rsqrt(ms + RMS_EPS) * gain_ref[...]


def _out_projection(ya, yp, w_bf16, x, gain, tm, tk):
    m, d = x.shape
    ka, kp = ya.shape[1] // tk, yp.shape[1] // tk
    return pl.pallas_call(
        functools.partial(_outproj_kernel, k_attn=ka),
        grid=(m // tm, ka + kp),
        in_specs=[pl.BlockSpec((tm, tk), lambda i, k: (i, jnp.minimum(k, ka - 1))),
                  pl.BlockSpec((tm, tk), lambda i, k: (i, jnp.maximum(k - ka, 0))),
                  pl.BlockSpec((tk, d), lambda i, k: (k, 0)),
                  pl.BlockSpec((tm, d), lambda i, k: (i, 0)),
                  pl.BlockSpec((1, d), lambda i, k: (0, 0))],
        out_specs=pl.BlockSpec((tm, d), lambda i, k: (i, 0)),
        out_shape=jax.ShapeDtypeStruct((m, d), F32),
        compiler_params=pltpu.CompilerParams(
            dimension_semantics=("parallel", "arbitrary"), vmem_limit_bytes=VMEM_LIMIT),
    )(ya, yp, w_bf16, x, gain.reshape(1, d))


def kernel(x_prompt, x_sample, cache_k, cache_v, state_pool, page_table, norm_in, w_in, w_pool,
           pool_scale, w_out, rel_bias, norm_out):
    depth = norm_in.shape[0]
    assert depth == 1, "single layer step"
    n_batch, seq, d_model = x_prompt.shape
    bsz, dec_seq, _ = x_sample.shape
    assert n_batch == 1 and seq % (8 * MOBA_BLOCK) == 0 and dec_seq == 8
    past_len = page_table.shape[1] * PAGE_SIZE
    assert past_len >= POOL_HIST
    pool_width = pool_scale.shape[1]
    k0 = ATTN_WIDTH
    v0 = k0 + KV_WIDTH
    ga0 = v0 + KV_WIDTH
    u0 = ga0 + ATTN_WIDTH
    gp0 = u0 + pool_width
    assert w_in.shape[2] == gp0 + pool_width and w_out.shape[1] == ATTN_WIDTH + pool_width

    w_in_b = w_in[0].astype(BF16)
    w_out_b = w_out[0].astype(BF16)
    w_pool_b = w_pool[0].astype(BF16)
    xp = x_prompt[0]
    xs = x_sample.reshape(bsz * dec_seq, d_model)

    z = _in_projection(xp, norm_in[0], w_in_b, tm=512, tn=1024)
    zs = _in_projection(xs, norm_in[0], w_in_b, tm=bsz * dec_seq, tn=1024)
    q_s, k_s, v_s = zs[:, :k0], zs[:, k0:v0], zs[:, v0:ga0]
    ga_s, u_s, gp_s = zs[:, ga0:u0], zs[:, u0:gp0], zs[:, gp0:]

    n_blk = seq // MOBA_BLOCK
    kmean, kaug, vaug = _stage_kv(z, seq, k0, v0)
    kmean_pad = jnp.pad(kmean, ((0, LANES - n_blk), (0, 0)))
    n_phys = cache_k.shape[1]
    page_rows = PAGE_SIZE * N_KV_HEADS
    ya, ya_s = _attention(z, kmean_pad, kaug, vaug, rel_bias, seq, ga0, q_s, k_s, v_s, ga_s,
                          cache_k.reshape(n_phys, page_rows, HEAD_DIM),
                          cache_v.reshape(n_phys, page_rows, HEAD_DIM), page_table, past_len)
    yp = _pool_prompt(z, w_pool_b, pool_scale[0], seq, u0, gp0, tm=512)
    y_prompt = _out_projection(ya, yp, w_out_b, xp, norm_out, tm=512, tk=512)

    u_s3 = u_s.reshape(bsz, dec_seq, pool_width)
    u_ext = jnp.concatenate([jnp.zeros((bsz, 1, pool_width), F32), state_pool[0].astype(F32), u_s3], axis=1)
    yp_s = _pool_sample(u_ext.reshape(bsz * (POOL_HALO + dec_seq), pool_width), gp_s, w_pool_b,
                        pool_scale[0], bsz, dec_seq)
    y_sample = _out_projection(ya_s, yp_s, w_out_b, xs, norm_out, tm=bsz * dec_seq, tk=512)

    k_prompt = z[:, k0:v0].reshape(1, 1, seq, N_KV_HEADS, HEAD_DIM)
    v_prompt = z[:, v0:ga0].reshape(1, 1, seq, N_KV_HEADS, HEAD_DIM)
    pool_prompt = z[seq - POOL_HIST:, u0:gp0].reshape(1, 1, POOL_HIST, pool_width)
    k_sample = k_s.reshape(1, bsz, dec_seq, N_KV_HEADS, HEAD_DIM)
    v_sample = v_s.reshape(1, bsz, dec_seq, N_KV_HEADS, HEAD_DIM)
    pool_sample = u_ext[:, 1 + dec_seq:, :].reshape(1, bsz, POOL_HIST, pool_width).astype(state_pool.dtype)
    return (y_prompt.reshape(1, seq, d_model), y_sample.reshape(bsz, dec_seq, d_model),
            k_prompt, v_prompt, pool_prompt, k_sample, v_sample, pool_sample)
```

```python
import functools
import math
from typing import NamedTuple

import numpy as np
import jax
import jax.numpy as jnp
from jax import lax
from jax.experimental import pallas as pl
from jax.experimental.pallas import tpu as pltpu

F32 = jnp.float32
BF16 = jnp.bfloat16

HEAD_DIM = 128
N_HEADS = 16
N_KV_HEADS = 4
KV_GROUP = N_HEADS // N_KV_HEADS
ATTN_WIDTH = N_HEADS * HEAD_DIM
KV_WIDTH = N_KV_HEADS * HEAD_DIM
GROUP_WIDTH = KV_GROUP * HEAD_DIM
ATTN_SCALE = HEAD_DIM ** -0.5
LOG2_E = math.log2(math.e)
MOBA_BLOCK = 256
MOBA_TOPK = 3
PAGE_SIZE = 128
PAGES_PER_BLOCK = MOBA_BLOCK // PAGE_SIZE
POOL_WINDOWS = (2, 4, 8, 16)
N_POOL_GROUPS = len(POOL_WINDOWS)
POOL_HIST = max(POOL_WINDOWS) - 1
POOL_HALO = POOL_HIST + 1
NUM_BUCKETS = 32
MAX_EXACT = NUM_BUCKETS // 2
MAX_DISTANCE = 4096
RMS_EPS = 1e-6
NEG_INF = -1e30
GATE_MASKED = -3.0e38
GATE_VALID_MIN = -1.0e38
LANES = 128
TABLE_DELTAS = 3
SWEEP_ROWS = 256
FAR_GROUP_BLOCKS = 8
FAR_GROUPS = 1
SAMPLE_BLOCKS_PER_STEP = 8

VMEM_LIMIT = 56 * 1024 * 1024


def _rel_bucket(dist):
    n = jnp.maximum(dist, 0)
    nf = jnp.maximum(n, 1).astype(F32)
    large = MAX_EXACT + (jnp.log(nf / MAX_EXACT) / math.log(MAX_DISTANCE / MAX_EXACT)
                         * (NUM_BUCKETS - MAX_EXACT)).astype(jnp.int32)
    large = jnp.minimum(large, NUM_BUCKETS - 1)
    return jnp.where(n < MAX_EXACT, n, large)


def _rel_bucket_np(dist):
    n = np.maximum(dist, 0)
    nf = np.maximum(n, 1).astype(np.float32)
    large = MAX_EXACT + (np.log(nf / np.float32(MAX_EXACT)) / np.float32(math.log(MAX_DISTANCE / MAX_EXACT))
                         * np.float32(NUM_BUCKETS - MAX_EXACT)).astype(np.int32)
    large = np.minimum(large, NUM_BUCKETS - 1)
    return np.where(n < MAX_EXACT, n, large)


def _bias_of_bucket(rel_bias, bucket, bucket_axis):
    onehot = (jnp.expand_dims(bucket, bucket_axis)
              == jnp.arange(NUM_BUCKETS, dtype=jnp.int32).reshape(
                  (NUM_BUCKETS,) + (1,) * (bucket.ndim - bucket_axis))).astype(F32)
    out = jnp.tensordot(rel_bias.astype(F32).T, onehot, axes=((1,), (bucket_axis,)),
                        precision=lax.Precision.HIGHEST)
    return jnp.moveaxis(out, 0, bucket_axis)


def _silu(x):
    return x * jax.nn.sigmoid(x)


def _dot_nt(a, b, precision=None):
    return lax.dot_general(a, b, (((1,), (1,)), ((), ())), precision=precision,
                           preferred_element_type=F32)


def _select_top_blocks(gate, block_id, n_past, axis=-1):
    gate = jnp.where(block_id < n_past, gate, GATE_MASKED)
    id_f = block_id.astype(F32)
    sel = jnp.zeros(gate.shape, jnp.bool_)
    for _ in range(MOBA_TOPK):
        mx = jnp.max(gate, axis=axis, keepdims=True)
        first = jnp.min(jnp.where(gate == mx, id_f, float(LANES)), axis=axis, keepdims=True)
        pick = (id_f == first) & (mx > GATE_VALID_MIN)
        sel = sel | pick
        gate = jnp.where(pick, GATE_MASKED, gate)
    return sel


def _inproj_kernel(x_ref, gain_ref, w_ref, o_ref, xn_ref, *, row_chunk):
    @pl.when(pl.program_id(1) == 0)
    def _():
        for r in range(0, x_ref.shape[0], row_chunk):
            x = x_ref[r:r + row_chunk, :]
            ms = jnp.mean(x * x, axis=-1, keepdims=True)
            xn_ref[r:r + row_chunk, :] = (x * lax.rsqrt(ms + RMS_EPS) * gain_ref[...]).astype(BF16)

    o_ref[...] = jnp.dot(xn_ref[...], w_ref[...], preferred_element_type=F32)


def _in_projection(x, gain, w_bf16, tm, tn):
    m, d = x.shape
    n = w_bf16.shape[1]
    return pl.pallas_call(
        functools.partial(_inproj_kernel, row_chunk=min(tm, 128)),
        grid=(m // tm, n // tn),
        in_specs=[pl.BlockSpec((tm, d), lambda i, j: (i, 0)),
                  pl.BlockSpec((1, d), lambda i, j: (0, 0)),
                  pl.BlockSpec((d, tn), lambda i, j: (0, j))],
        out_specs=pl.BlockSpec((tm, tn), lambda i, j: (i, j)),
        out_shape=jax.ShapeDtypeStruct((m, n), F32),
        scratch_shapes=[pltpu.VMEM((tm, d), BF16)],
        compiler_params=pltpu.CompilerParams(
            dimension_semantics=("parallel", "arbitrary"), vmem_limit_bytes=VMEM_LIMIT),
    )(x, gain.reshape(1, d), w_bf16)


def _stage_kv_kernel(k_ref, v_ref, km_ref, kaug_ref, vaug_ref, *, blocks_per_step):
    rows = blocks_per_step * MOBA_BLOCK
    k = k_ref[...]
    km_ref[...] = jnp.sum(k.reshape(blocks_per_step, MOBA_BLOCK, KV_WIDTH), axis=1) * (1.0 / MOBA_BLOCK)
    block_of_row = (pl.program_id(0) * blocks_per_step
                    + lax.broadcasted_iota(jnp.int32, (rows, LANES), 0) // MOBA_BLOCK)
    onehot = jnp.where(lax.broadcasted_iota(jnp.int32, (rows, LANES), 1) == block_of_row, 1.0, 0.0).astype(BF16)
    ones = jnp.ones((rows, LANES), BF16)
    for gi in range(N_KV_HEADS):
        cols = slice(gi * HEAD_DIM, (gi + 1) * HEAD_DIM)
        kaug_ref[gi, :, 0:HEAD_DIM] = k[:, cols].astype(BF16)
        kaug_ref[gi, :, HEAD_DIM:HEAD_DIM + LANES] = onehot
        vaug_ref[gi, :, 0:HEAD_DIM] = v_ref[:, cols].astype(BF16)
        vaug_ref[gi, :, HEAD_DIM:HEAD_DIM + LANES] = ones


def _stage_kv(z, seq, k_col0, v_col0):
    bps = 8
    n_blk = seq // MOBA_BLOCK
    assert n_blk % bps == 0 and k_col0 % KV_WIDTH == 0 and v_col0 % KV_WIDTH == 0
    rows = bps * MOBA_BLOCK
    aug = jax.ShapeDtypeStruct((N_KV_HEADS, seq, HEAD_DIM + LANES), BF16)
    aug_spec = pl.BlockSpec((N_KV_HEADS, rows, HEAD_DIM + LANES), lambda i: (0, i, 0))
    return pl.pallas_call(
        functools.partial(_stage_kv_kernel, blocks_per_step=bps),
        grid=(n_blk // bps,),
        in_specs=[pl.BlockSpec((rows, KV_WIDTH), lambda i: (i, k_col0 // KV_WIDTH)),
                  pl.BlockSpec((rows, KV_WIDTH), lambda i: (i, v_col0 // KV_WIDTH))],
        out_specs=[pl.BlockSpec((bps, KV_WIDTH), lambda i: (i, 0)), aug_spec, aug_spec],
        out_shape=[jax.ShapeDtypeStruct((n_blk, KV_WIDTH), F32), aug, aug],
        compiler_params=pltpu.CompilerParams(vmem_limit_bytes=VMEM_LIMIT),
    )(z, z)


class _SampleShape(NamedTuple):
    n_seq: int
    dec_seq: int
    n_past: int
    blocks_per_step: int
    steps_per_seq: int


def _attn_kernel(thr_ref, pt_ref,
                 q_ref, kaug_ref, vaug_ref, km_ref, ga_ref, tab_ref, blo_ref, bhi_ref,
                 qs_ref, knew_ref, vnew_ref, gas_ref, *rest, n_blk_pad, sample):
    n_cache_pages = sample.blocks_per_step * PAGES_PER_BLOCK
    k_pages, v_pages = rest[:n_cache_pages], rest[n_cache_pages:2 * n_cache_pages]
    (sbias_ref, sbias_own_ref, o_ref, so_ref, qaug_ref, m_ref, l_ref, acc_ref, dmat_ref,
     qf_ref, qb_ref, gate_ref, mx_ref, sl_ref, part_ref) = rest[2 * n_cache_pages:]
    g = pl.program_id(0)
    i = pl.program_id(1)
    blk = MOBA_BLOCK
    all_rows = KV_GROUP * blk

    def select_blocks():
        dmat_ref[...] = (lax.broadcasted_iota(jnp.int32, (blk, blk), 0)
                         - lax.broadcasted_iota(jnp.int32, (blk, blk), 1))
        m_ref[...] = jnp.full(m_ref.shape, NEG_INF, F32)
        l_ref[...] = jnp.zeros(l_ref.shape, F32)
        acc_ref[...] = jnp.zeros(acc_ref.shape, F32)
        q_all = jnp.concatenate(
            [q_ref[:, hh * HEAD_DIM:(hh + 1) * HEAD_DIM] for hh in range(KV_GROUP)], axis=0)
        gate_t = _dot_nt(km_ref[0:n_blk_pad, :], q_all, precision=lax.Precision.HIGHEST)
        blk_id = lax.broadcasted_iota(jnp.int32, (n_blk_pad, all_rows), 0)
        sel_t = _select_top_blocks(gate_t, blk_id, i, axis=0)
        mask_t = jnp.where(sel_t | (blk_id == i), 0.0, NEG_INF)
        if n_blk_pad < LANES:
            mask_t = jnp.concatenate([mask_t, jnp.zeros((LANES - n_blk_pad, all_rows), F32)], axis=0)
        qaug_ref[:, 0:HEAD_DIM] = (q_all * (ATTN_SCALE * LOG2_E)).astype(BF16)
        qaug_ref[:, HEAD_DIM:HEAD_DIM + LANES] = mask_t.T.astype(BF16)

    step = g * pl.num_programs(1) + i
    part = step % sample.steps_per_seq
    sample_init, sample_past, sample_own = _sample_phases(
        part, sample, qs_ref, knew_ref, vnew_ref, gas_ref, k_pages, v_pages, sbias_ref, sbias_own_ref,
        so_ref, qf_ref, qb_ref, gate_ref, mx_ref, sl_ref, part_ref)
    sample_steps = sample.n_seq * sample.steps_per_seq
    if sample_steps == N_KV_HEADS * (kaug_ref.shape[1] // blk):
        pl.when(part == 0)(sample_init)
        select_blocks()
        sample_past()
        pl.when(part == sample.steps_per_seq - 1)(sample_own)
    else:
        @pl.when(step < sample_steps)
        def _sample_chunk():
            pl.when(part == 0)(sample_init)
            sample_past()
            pl.when(part == sample.steps_per_seq - 1)(sample_own)

        select_blocks()

    def sweep(groups):
        m_all, l_all, acc_all = m_ref[...], l_ref[...], acc_ref[...]
        chunks = [slice(c, c + SWEEP_ROWS) for c in range(0, all_rows, SWEEP_ROWS)]
        m_c = [m_all[rows] for rows in chunks]
        l_c = [l_all[rows] for rows in chunks]
        acc_c = [acc_all[rows] for rows in chunks]
        logits = []
        for n0, nb, bias in groups:
            r = pl.multiple_of(n0 * blk, blk)
            keys = kaug_ref[0, pl.ds(r, nb * blk), :]
            logits.append([_dot_nt(qaug_ref[rows, :], keys) + bias[rows] for rows in chunks])
        for (n0, nb, bias), group_logits in zip(groups, logits):
            r = pl.multiple_of(n0 * blk, blk)
            values = vaug_ref[0, pl.ds(r, nb * blk), :]
            for ci, rows in enumerate(chunks):
                sc = group_logits[ci]
                m_new = jnp.maximum(m_c[ci], jnp.max(sc, axis=-1, keepdims=True))
                alpha = jnp.exp2(m_c[ci] - m_new)
                p = jnp.exp2(sc - jnp.concatenate([m_new] * (nb * blk // LANES), axis=1))
                pv = jnp.dot(p.astype(BF16), values, preferred_element_type=F32)
                l_c[ci] = alpha * l_c[ci] + pv[:, HEAD_DIM:]
                acc_c[ci] = alpha * acc_c[ci] + pv[:, :HEAD_DIM]
                m_c[ci] = m_new
        m_ref[...] = jnp.concatenate(m_c, axis=0)
        l_ref[...] = jnp.concatenate(l_c, axis=0)
        acc_ref[...] = jnp.concatenate(acc_c, axis=0)

    def far_bias(n0, nb):
        tiles = []
        for j in range(nb):
            delta = i - (n0 + j)
            high = dmat_ref[...] >= thr_ref[delta]
            base = delta * N_HEADS + g * KV_GROUP
            tiles.append(jnp.concatenate([jnp.where(high, bhi_ref[base + hh], blo_ref[base + hh])
                                          for hh in range(KV_GROUP)], axis=0))
        return jnp.concatenate(tiles, axis=1)

    n_far = jnp.maximum(i - (TABLE_DELTAS - 1), 0)
    per_iter = FAR_GROUPS * FAR_GROUP_BLOCKS

    def far_step(t, carry):
        first = t * per_iter
        sweep([(first + j * FAR_GROUP_BLOCKS, FAR_GROUP_BLOCKS,
                far_bias(first + j * FAR_GROUP_BLOCKS, FAR_GROUP_BLOCKS)) for j in range(FAR_GROUPS)])
        return carry

    lax.fori_loop(0, n_far // per_iter, far_step, 0)

    left = n_far % per_iter
    near = jnp.minimum(i + 1, TABLE_DELTAS)
    for count in range(1, TABLE_DELTAS + 1):
        for rem in range(per_iter if count == TABLE_DELTAS else 1):
            @pl.when((near == count) & (left == rem))
            def _tail(count=count, rem=rem):
                groups = []
                for first in range(0, rem, FAR_GROUP_BLOCKS):
                    nb = min(FAR_GROUP_BLOCKS, rem - first)
                    groups.append((n_far - rem + first, nb, far_bias(n_far - rem + first, nb)))
                groups.append((i - (count - 1), count, jnp.concatenate(
                    [tab_ref[count - 1 - j].reshape(all_rows, blk) for j in range(count)], axis=1)))
                sweep(groups)

    attn = acc_ref[...] / l_ref[...]
    for hh in range(KV_GROUP):
        cols = slice(hh * HEAD_DIM, (hh + 1) * HEAD_DIM)
        o_ref[:, cols] = (attn[hh * blk:(hh + 1) * blk, :] * _silu(ga_ref[:, cols])).astype(BF16)


def _block_of_pages(page_refs):
    return jnp.concatenate(
        [jnp.concatenate([ref[0, pl.ds(gi, PAGE_SIZE, stride=N_KV_HEADS), :] for gi in range(N_KV_HEADS)],
                         axis=1) for ref in page_refs], axis=0)


def _sample_phases(part, sample, q_ref, knew_ref, vnew_ref, ga_ref, k_pages, v_pages, bias_ref, bias_own_ref,
                   o_ref, qf_ref, qb_ref, gate_ref, mx_ref, l_ref, part_ref):
    dec_seq, n_past, blocks_per_step = sample.dec_seq, sample.n_past, sample.blocks_per_step
    rows = N_HEADS * dec_seq
    grp_rows = KV_GROUP * dec_seq

    def _init():
        gate_ref[...] = jnp.zeros(gate_ref.shape, F32)
        mx_ref[...] = jnp.zeros(mx_ref.shape, F32)
        l_ref[...] = jnp.zeros(l_ref.shape, F32)
        col_grp = lax.broadcasted_iota(jnp.int32, (dec_seq, KV_WIDTH), 1) // HEAD_DIM
        for h in range(N_HEADS):
            qh = q_ref[:, h * HEAD_DIM:(h + 1) * HEAD_DIM]
            wide = jnp.concatenate([qh] * N_KV_HEADS, axis=1)
            wide = jnp.where(col_grp == h // KV_GROUP, wide, 0.0)
            qf_ref[h * dec_seq:(h + 1) * dec_seq, :] = wide
            qb_ref[h * dec_seq:(h + 1) * dec_seq, :] = (wide * ATTN_SCALE).astype(BF16)

    def block_partials(kblks, vblks, biases):
        scs = [_dot_nt(qb_ref[...], kblk.astype(BF16)) + bias for kblk, bias in zip(kblks, biases)]
        ms = [jnp.max(sc, axis=-1, keepdims=True) for sc in scs]
        ps = [jnp.exp(sc - m_n) for sc, m_n in zip(scs, ms)]
        ls = [jnp.sum(p, axis=-1, keepdims=True) for p in ps]
        outs = []
        for p, vblk in zip(ps, vblks):
            o_all = jnp.dot(p.astype(BF16), vblk.astype(BF16), preferred_element_type=F32)
            outs.append(jnp.concatenate(
                [o_all[gi * grp_rows:(gi + 1) * grp_rows, gi * HEAD_DIM:(gi + 1) * HEAD_DIM]
                 for gi in range(N_KV_HEADS)], axis=0))
        return ms, ls, outs

    def _past():
        lane = lax.broadcasted_iota(jnp.int32, (rows, LANES), 1)
        kblks = [_block_of_pages(k_pages[jb * PAGES_PER_BLOCK:(jb + 1) * PAGES_PER_BLOCK])
                 for jb in range(blocks_per_step)]
        vblks = [_block_of_pages(v_pages[jb * PAGES_PER_BLOCK:(jb + 1) * PAGES_PER_BLOCK])
                 for jb in range(blocks_per_step)]
        ms, lsums, outs = block_partials(kblks, vblks, [bias_ref[jb] for jb in range(blocks_per_step)])
        mx, ls, gt = mx_ref[...], l_ref[...], gate_ref[...]
        for jb in range(blocks_per_step):
            n = part * blocks_per_step + jb
            kmean = jnp.sum(kblks[jb], axis=0, keepdims=True) * (1.0 / MOBA_BLOCK)
            gate = jnp.sum(qf_ref[...] * kmean, axis=-1, keepdims=True)
            part_ref[n] = outs[jb]
            here = lane == n
            mx = jnp.where(here, ms[jb], mx)
            ls = jnp.where(here, lsums[jb], ls)
            gt = jnp.where(here, gate, gt)
        mx_ref[...] = mx
        l_ref[...] = ls
        gate_ref[...] = gt

    def _own():
        lane = lax.broadcasted_iota(jnp.int32, (rows, LANES), 1)
        pad = jnp.zeros((MOBA_BLOCK - dec_seq, KV_WIDTH), F32)
        (m_n,), (l_n,), (o_n,) = block_partials([jnp.concatenate([knew_ref[...], pad], axis=0)],
                                                [jnp.concatenate([vnew_ref[...], pad], axis=0)],
                                                [bias_own_ref[0]])
        own = lane == n_past
        mx = jnp.where(own, m_n, mx_ref[...])
        ls = jnp.where(own, l_n, l_ref[...])
        sel = _select_top_blocks(gate_ref[...], lane, n_past)
        include = sel | own
        m_tot = jnp.max(jnp.where(include, mx, GATE_MASKED), axis=-1, keepdims=True)
        e = jnp.where(include, jnp.exp(jnp.minimum(mx - m_tot, 0.0)), 0.0)
        denom = jnp.sum(e * ls, axis=-1, keepdims=True)
        num = e[:, n_past:n_past + 1] * o_n
        for j in range(n_past):
            num = num + e[:, j:j + 1] * part_ref[j]
        attn = num / denom
        for h in range(N_HEADS):
            cols = slice(h * HEAD_DIM, (h + 1) * HEAD_DIM)
            o_ref[0, :, cols] = (attn[h * dec_seq:(h + 1) * dec_seq, :] * _silu(ga_ref[:, cols])).astype(BF16)

    return _init, _past, _own


def _prompt_bias_tables(rel_bias, n_blk):
    blk = MOBA_BLOCK
    r = jnp.arange(blk, dtype=jnp.int32)
    dist = (jnp.arange(TABLE_DELTAS, dtype=jnp.int32)[:, None, None] * blk
            + r[None, :, None] - r[None, None, :])
    tab = _bias_of_bucket(rel_bias, _rel_bucket(dist), 1)
    tab = jnp.where((dist >= 0)[:, None], tab * LOG2_E, NEG_INF)

    nd = max(n_blk, TABLE_DELTAS + 1)
    off = np.arange(-(blk - 1), blk, dtype=np.int32)
    dist_np = np.arange(nd, dtype=np.int32)[:, None] * blk + off[None, :]
    buckets_np = _rel_bucket_np(dist_np)
    span = buckets_np[TABLE_DELTAS:, -1] - buckets_np[TABLE_DELTAS:, 0]
    assert np.all(np.diff(buckets_np[TABLE_DELTAS:], axis=1) >= 0) and np.all(span <= 1), \
        "more than one bias bucket boundary inside a far block pair"
    buckets = _rel_bucket(jnp.asarray(dist_np))
    b_lo, b_hi = buckets[:, 0], buckets[:, -1]
    first_hi = jnp.argmax(buckets == b_hi[:, None], axis=1).astype(jnp.int32)
    thr = first_hi - (blk - 1)
    blo = (rel_bias.astype(F32)[b_lo] * LOG2_E).reshape(-1)
    bhi = (rel_bias.astype(F32)[b_hi] * LOG2_E).reshape(-1)
    return tab, thr, blo, bhi


def _sample_bias_table(rel_bias, past_len, dec_seq, n_blocks):
    kpos = jnp.arange(n_blocks * MOBA_BLOCK, dtype=jnp.int32).reshape(n_blocks, 1, MOBA_BLOCK)
    qpos = (past_len + jnp.arange(dec_seq, dtype=jnp.int32)).reshape(1, dec_seq, 1)
    dist = qpos - kpos
    vals = _bias_of_bucket(rel_bias, _rel_bucket(dist), 1)
    vals = jnp.where((dist >= 0)[:, None], vals, NEG_INF)
    return vals.reshape(n_blocks, N_HEADS * dec_seq, MOBA_BLOCK)


def _attention(z, kmean_pad, kaug, vaug, rel_bias, seq, ga_col0,
               q_s, k_s, v_s, ga_s, cache_k, cache_v, page_table, past_len):
    blk = MOBA_BLOCK
    n_blk = seq // blk
    assert n_blk <= LANES
    tab, thr, blo, bhi = _prompt_bias_tables(rel_bias, n_blk)
    gc = ga_col0 // GROUP_WIDTH

    bsz, n_pages = page_table.shape
    dec_seq = q_s.shape[0] // bsz
    n_past = past_len // blk
    bps = math.gcd(SAMPLE_BLOCKS_PER_STEP, n_past)
    assert past_len % blk == 0 and n_past + 1 <= LANES and n_pages == n_past * PAGES_PER_BLOCK
    sample = _SampleShape(bsz, dec_seq, n_past, bps, n_past // bps)
    sample_steps = sample.n_seq * sample.steps_per_seq
    assert sample_steps <= N_KV_HEADS * n_blk, "sample attention chunks must fit in the prompt grid"
    pages_per_step = bps * PAGES_PER_BLOCK
    sbias = _sample_bias_table(rel_bias, past_len, dec_seq, n_past + 1)
    rows = N_HEADS * dec_seq

    def sample_pos(g, i):
        s = jnp.minimum(g * n_blk + i, sample_steps - 1)
        return s // sample.steps_per_seq, s % sample.steps_per_seq

    def page_spec(which):
        def index_map(g, i, th, pt):
            b, part = sample_pos(g, i)
            return (pt[b * n_pages + part * pages_per_step + which], 0, 0)
        return pl.BlockSpec((1, PAGE_SIZE * N_KV_HEADS, HEAD_DIM), index_map)

    row_spec = lambda width: pl.BlockSpec((dec_seq, width), lambda g, i, th, pt: (sample_pos(g, i)[0], 0))
    resident = dict(pipeline_mode=pl.Buffered(1))
    smem = pl.BlockSpec(memory_space=pltpu.SMEM)
    grid_spec = pltpu.PrefetchScalarGridSpec(
        num_scalar_prefetch=2,
        grid=(N_KV_HEADS, n_blk),
        in_specs=([
            pl.BlockSpec((blk, GROUP_WIDTH), lambda g, i, th, pt: (i, g)),
            pl.BlockSpec((1, seq, HEAD_DIM + LANES), lambda g, i, th, pt: (g, 0, 0), **resident),
            pl.BlockSpec((1, seq, HEAD_DIM + LANES), lambda g, i, th, pt: (g, 0, 0), **resident),
            pl.BlockSpec((LANES, HEAD_DIM), lambda g, i, th, pt: (0, g)),
            pl.BlockSpec((blk, GROUP_WIDTH), lambda g, i, th, pt: (i, gc + g)),
            pl.BlockSpec((TABLE_DELTAS, KV_GROUP, blk, blk), lambda g, i, th, pt: (0, g, 0, 0), **resident),
            smem, smem,
            row_spec(ATTN_WIDTH), row_spec(KV_WIDTH), row_spec(KV_WIDTH), row_spec(ATTN_WIDTH)]
            + [page_spec(j) for j in range(pages_per_step)]
            + [page_spec(j) for j in range(pages_per_step)]
            + [pl.BlockSpec((bps, rows, blk), lambda g, i, th, pt: (sample_pos(g, i)[1], 0, 0)),
               pl.BlockSpec((1, rows, blk), lambda g, i, th, pt: (n_past, 0, 0))]),
        out_specs=[pl.BlockSpec((blk, GROUP_WIDTH), lambda g, i, th, pt: (i, g)),
                   pl.BlockSpec((1, dec_seq, ATTN_WIDTH), lambda g, i, th, pt: (sample_pos(g, i)[0], 0, 0))],
        scratch_shapes=[
            pltpu.VMEM((KV_GROUP * blk, HEAD_DIM + LANES), BF16),
            pltpu.VMEM((KV_GROUP * blk, LANES), F32),
            pltpu.VMEM((KV_GROUP * blk, LANES), F32),
            pltpu.VMEM((KV_GROUP * blk, HEAD_DIM), F32),
            pltpu.VMEM((blk, blk), jnp.int32),
            pltpu.VMEM((rows, KV_WIDTH), F32),
            pltpu.VMEM((rows, KV_WIDTH), BF16),
            pltpu.VMEM((rows, LANES), F32),
            pltpu.VMEM((rows, LANES), F32),
            pltpu.VMEM((rows, LANES), F32),
            pltpu.VMEM((n_past, rows, HEAD_DIM), F32),
        ],
    )
    ya, ya_s = pl.pallas_call(
        functools.partial(_attn_kernel, n_blk_pad=-(-n_blk // 8) * 8, sample=sample),
        grid_spec=grid_spec,
        out_shape=[jax.ShapeDtypeStruct((seq, ATTN_WIDTH), BF16),
                   jax.ShapeDtypeStruct((bsz, dec_seq, ATTN_WIDTH), BF16)],
        compiler_params=pltpu.CompilerParams(
            dimension_semantics=("arbitrary", "arbitrary"), vmem_limit_bytes=VMEM_LIMIT),
    )(thr, page_table.reshape(-1), z, kaug, vaug, kmean_pad, z, tab, blo, bhi, q_s, k_s, v_s, ga_s,
      *([cache_k] * pages_per_step), *([cache_v] * pages_per_step), sbias, sbias)
    return ya, ya_s.reshape(bsz * dec_seq, ATTN_WIDTH)


def _window_mix(ext_ref, window, rows, cols, count, u, gp, w, scale):
    win = u
    for j in range(1, window):
        win = win + ext_ref[POOL_HALO - j:POOL_HALO - j + rows, cols]
    zed = win / count - u
    y = jnp.dot(zed.astype(BF16), w, preferred_element_type=F32)
    return (y * scale) * _silu(gp)


def _pool_prompt_kernel(prev_ref, u_ref, gp_ref, w_ref, scale_ref, o_ref, ext_ref, *, groups_per_step):
    i = pl.program_id(0)
    rows = u_ref.shape[0]
    gw = w_ref.shape[1]
    ext_ref[0:POOL_HALO, :] = jnp.where(i == 0, 0.0, prev_ref[...])
    ext_ref[POOL_HALO:POOL_HALO + rows, :] = u_ref[...]
    pos = i * rows + lax.broadcasted_iota(jnp.int32, (rows, 1), 0)
    for part in range(N_POOL_GROUPS // groups_per_step):
        @pl.when(pl.program_id(1) == part)
        def _(part=part):
            for gl in range(groups_per_step):
                window = POOL_WINDOWS[part * groups_per_step + gl]
                cols = slice(gl * gw, (gl + 1) * gw)
                count = jnp.minimum(window, pos + 1).astype(F32)
                y = _window_mix(ext_ref, window, rows, cols, count, u_ref[:, cols], gp_ref[:, cols],
                                w_ref[gl], scale_ref[:, cols])
                o_ref[:, cols] = y.astype(BF16)


def _pool_prompt(z, w_pool_bf16, pool_scale, seq, u_col0, gp_col0, tm):
    n_groups, gw, _ = w_pool_bf16.shape
    gps = 2
    bw = gps * gw
    assert n_groups == N_POOL_GROUPS and u_col0 % bw == 0 and gp_col0 % bw == 0 and tm % POOL_HALO == 0
    uc, gc = u_col0 // bw, gp_col0 // bw
    halo_blocks = tm // POOL_HALO
    return pl.pallas_call(
        functools.partial(_pool_prompt_kernel, groups_per_step=gps),
        grid=(seq // tm, n_groups // gps),
        in_specs=[pl.BlockSpec((POOL_HALO, bw), lambda i, c: (jnp.maximum(i * halo_blocks - 1, 0), uc + c)),
                  pl.BlockSpec((tm, bw), lambda i, c: (i, uc + c)),
                  pl.BlockSpec((tm, bw), lambda i, c: (i, gc + c)),
                  pl.BlockSpec((gps, gw, gw), lambda i, c: (c, 0, 0)),
                  pl.BlockSpec((1, bw), lambda i, c: (0, c))],
        out_specs=pl.BlockSpec((tm, bw), lambda i, c: (i, c)),
        out_shape=jax.ShapeDtypeStruct((seq, n_groups * gw), BF16),
        scratch_shapes=[pltpu.VMEM((POOL_HALO + tm, bw), F32)],
        compiler_params=pltpu.CompilerParams(
            dimension_semantics=("parallel", "arbitrary"), vmem_limit_bytes=VMEM_LIMIT),
    )(z, z, z, w_pool_bf16, pool_scale.reshape(1, n_groups * gw))


def _pool_sample_kernel(ext_in_ref, gp_ref, w_ref, scale_ref, o_ref, ext_ref, *, bsz, dec_seq):
    seg = POOL_HALO + dec_seq
    rows = bsz * seg
    gw = w_ref.shape[1]
    ext_ref[0:POOL_HALO, :] = jnp.zeros((POOL_HALO, ext_ref.shape[1]), F32)
    ext_ref[POOL_HALO:POOL_HALO + rows, :] = ext_in_ref[...]
    for gi, window in enumerate(POOL_WINDOWS):
        cols = slice(gi * gw, (gi + 1) * gw)
        u_all = ext_in_ref[:, cols]
        win = u_all
        for j in range(1, window):
            win = win + ext_ref[POOL_HALO - j:POOL_HALO - j + rows, cols]
        zed = (win / float(window) - u_all).reshape(bsz, seg, gw)[:, POOL_HALO:, :].reshape(bsz * dec_seq, gw)
        y = jnp.dot(zed.astype(BF16), w_ref[gi], preferred_element_type=F32)
        o_ref[:, cols] = ((y * scale_ref[:, cols]) * _silu(gp_ref[:, cols])).astype(BF16)


def _pool_sample(u_ext, gp_s, w_pool_bf16, pool_scale, bsz, dec_seq):
    pw = u_ext.shape[1]
    rows = u_ext.shape[0]
    vm = pl.BlockSpec(memory_space=pltpu.VMEM)
    return pl.pallas_call(
        functools.partial(_pool_sample_kernel, bsz=bsz, dec_seq=dec_seq),
        in_specs=[vm, vm, vm, vm],
        out_specs=vm,
        out_shape=jax.ShapeDtypeStruct((bsz * dec_seq, pw), BF16),
        scratch_shapes=[pltpu.VMEM((POOL_HALO + rows, pw), F32)],
        compiler_params=pltpu.CompilerParams(vmem_limit_bytes=VMEM_LIMIT),
    )(u_ext, gp_s, w_pool_bf16, pool_scale.reshape(1, pw))


def _outproj_kernel(ya_ref, yp_ref, w_ref, x_ref, gain_ref, o_ref, *, k_attn):
    k = pl.program_id(1)

    @pl.when(k == 0)
    def _():
        o_ref[...] = x_ref[...] + jnp.dot(ya_ref[...], w_ref[...], preferred_element_type=F32)

    @pl.when((k > 0) & (k < k_attn))
    def _():
        o_ref[...] += jnp.dot(ya_ref[...], w_ref[...], preferred_element_type=F32)

    @pl.when(k >= k_attn)
    def _():
        o_ref[...] += jnp.dot(yp_ref[...], w_ref[...], preferred_element_type=F32)

    @pl.when(k == pl.num_programs(1) - 1)
    def _():
        h = o_ref[...]
        ms = jnp.mean(h * h, axis=-1, keepdims=True)
        o_ref[...] = h * lax.rsqrt(ms + RMS_EPS) * gain_ref[...]


def _out_projection(ya, yp, w_bf16, x, gain, tm, tk):
    m, d = x.shape
    ka, kp = ya.shape[1] // tk, yp.shape[1] // tk
    return pl.pallas_call(
        functools.partial(_outproj_kernel, k_attn=ka),
        grid=(m // tm, ka + kp),
        in_specs=[pl.BlockSpec((tm, tk), lambda i, k: (i, jnp.minimum(k, ka - 1))),
                  pl.BlockSpec((tm, tk), lambda i, k: (i, jnp.maximum(k - ka, 0))),
                  pl.BlockSpec((tk, d), lambda i, k: (k, 0)),
                  pl.BlockSpec((tm, d), lambda i, k: (i, 0)),
                  pl.BlockSpec((1, d), lambda i, k: (0, 0))],
        out_specs=pl.BlockSpec((tm, d), lambda i, k: (i, 0)),
        out_shape=jax.ShapeDtypeStruct((m, d), F32),
        compiler_params=pltpu.CompilerParams(
            dimension_semantics=("parallel", "arbitrary"), vmem_limit_bytes=VMEM_LIMIT),
    )(ya, yp, w_bf16, x, gain.reshape(1, d))


def kernel(x_prompt, x_sample, cache_k, cache_v, state_pool, page_table, norm_in, w_in, w_pool,
           pool_scale, w_out, rel_bias, norm_out):
    depth = norm_in.shape[0]
    assert depth == 1, "single layer step"
    n_batch, seq, d_model = x_prompt.shape
    bsz, dec_seq, _ = x_sample.shape
    assert n_batch == 1 and seq % (8 * MOBA_BLOCK) == 0 and dec_seq == 8
    past_len = page_table.shape[1] * PAGE_SIZE
    assert past_len >= POOL_HIST
    pool_width = pool_scale.shape[1]
    k0 = ATTN_WIDTH
    v0 = k0 + KV_WIDTH
    ga0 = v0 + KV_WIDTH
    u0 = ga0 + ATTN_WIDTH
    gp0 = u0 + pool_width
    assert w_in.shape[2] == gp0 + pool_width and w_out.shape[1] == ATTN_WIDTH + pool_width

    w_in_b = w_in[0].astype(BF16)
    w_out_b = w_out[0].astype(BF16)
    w_pool_b = w_pool[0].astype(BF16)
    xp = x_prompt[0]
    xs = x_sample.reshape(bsz * dec_seq, d_model)

    z = _in_projection(xp, norm_in[0], w_in_b, tm=512, tn=1024)
    zs = _in_projection(xs, norm_in[0], w_in_b, tm=bsz * dec_seq, tn=1024)
    q_s, k_s, v_s = zs[:, :k0], zs[:, k0:v0], zs[:, v0:ga0]
    ga_s, u_s, gp_s = zs[:, ga0:u0], zs[:, u0:gp0], zs[:, gp0:]

    n_blk = seq // MOBA_BLOCK
    kmean, kaug, vaug = _stage_kv(z, seq, k0, v0)
    kmean_pad = jnp.pad(kmean, ((0, LANES - n_blk), (0, 0)))
    n_phys = cache_k.shape[1]
    page_rows = PAGE_SIZE * N_KV_HEADS
    ya, ya_s = _attention(z, kmean_pad, kaug, vaug, rel_bias, seq, ga0, q_s, k_s, v_s, ga_s,
                          cache_k.reshape(n_phys, page_rows, HEAD_DIM),
                          cache_v.reshape(n_phys, page_rows, HEAD_DIM), page_table, past_len)
    yp = _pool_prompt(z, w_pool_b, pool_scale[0], seq, u0, gp0, tm=512)
    y_prompt = _out_projection(ya, yp, w_out_b, xp, norm_out, tm=512, tk=512)

    u_s3 = u_s.reshape(bsz, dec_seq, pool_width)
    u_ext = jnp.concatenate([jnp.zeros((bsz, 1, pool_width), F32), state_pool[0].astype(F32), u_s3], axis=1)
    yp_s = _pool_sample(u_ext.reshape(bsz * (POOL_HALO + dec_seq), pool_width), gp_s, w_pool_b,
                        pool_scale[0], bsz, dec_seq)
    y_sample = _out_projection(ya_s, yp_s, w_out_b, xs, norm_out, tm=bsz * dec_seq, tk=512)

    k_prompt = z[:, k0:v0].reshape(1, 1, seq, N_KV_HEADS, HEAD_DIM)
    v_prompt = z[:, v0:ga0].reshape(1, 1, seq, N_KV_HEADS, HEAD_DIM)
    pool_prompt = z[seq - POOL_HIST:, u0:gp0].reshape(1, 1, POOL_HIST, pool_width)
    k_sample = k_s.reshape(1, bsz, dec_seq, N_KV_HEADS, HEAD_DIM)
    v_sample = v_s.reshape(1, bsz, dec_seq, N_KV_HEADS, HEAD_DIM)
    pool_sample = u_ext[:, 1 + dec_seq:, :].reshape(1, bsz, POOL_HIST, pool_width).astype(state_pool.dtype)
    return (y_prompt.reshape(1, seq, d_model), y_sample.reshape(bsz, dec_seq, d_model),
            k_prompt, v_prompt, pool_prompt, k_sample, v_sample, pool_sample)
```

```python
import functools
import math
from typing import NamedTuple

import numpy as np
import jax
import jax.numpy as jnp
from jax import lax
from jax.experimental import pallas as pl
from jax.experimental.pallas import tpu as pltpu

F32 = jnp.float32
BF16 = jnp.bfloat16

HEAD_DIM = 128
N_HEADS = 16
N_KV_HEADS = 4
KV_GROUP = N_HEADS // N_KV_HEADS
ATTN_WIDTH = N_HEADS * HEAD_DIM
KV_WIDTH = N_KV_HEADS * HEAD_DIM
GROUP_WIDTH = KV_GROUP * HEAD_DIM
ATTN_SCALE = HEAD_DIM ** -0.5
LOG2_E = math.log2(math.e)
MOBA_BLOCK = 256
MOBA_TOPK = 3
PAGE_SIZE = 128
PAGES_PER_BLOCK = MOBA_BLOCK // PAGE_SIZE
POOL_WINDOWS = (2, 4, 8, 16)
N_POOL_GROUPS = len(POOL_WINDOWS)
POOL_HIST = max(POOL_WINDOWS) - 1
POOL_HALO = POOL_HIST + 1
NUM_BUCKETS = 32
MAX_EXACT = NUM_BUCKETS // 2
MAX_DISTANCE = 4096
RMS_EPS = 1e-6
NEG_INF = -1e30
GATE_MASKED = -3.0e38
GATE_VALID_MIN = -1.0e38
LANES = 128
TABLE_DELTAS = 3
SWEEP_ROWS = 256
FAR_GROUP_BLOCKS = 8
FAR_GROUPS = 1
SAMPLE_BLOCKS_PER_STEP = 8

VMEM_LIMIT = 56 * 1024 * 1024


def _rel_bucket(dist):
    n = jnp.maximum(dist, 0)
    nf = jnp.maximum(n, 1).astype(F32)
    large = MAX_EXACT + (jnp.log(nf / MAX_EXACT) / math.log(MAX_DISTANCE / MAX_EXACT)
                         * (NUM_BUCKETS - MAX_EXACT)).astype(jnp.int32)
    large = jnp.minimum(large, NUM_BUCKETS - 1)
    return jnp.where(n < MAX_EXACT, n, large)


def _rel_bucket_np(dist):
    n = np.maximum(dist, 0)
    nf = np.maximum(n, 1).astype(np.float32)
    large = MAX_EXACT + (np.log(nf / np.float32(MAX_EXACT)) / np.float32(math.log(MAX_DISTANCE / MAX_EXACT))
                         * np.float32(NUM_BUCKETS - MAX_EXACT)).astype(np.int32)
    large = np.minimum(large, NUM_BUCKETS - 1)
    return np.where(n < MAX_EXACT, n, large)


def _bias_of_bucket(rel_bias, bucket, bucket_axis):
    onehot = (jnp.expand_dims(bucket, bucket_axis)
              == jnp.arange(NUM_BUCKETS, dtype=jnp.int32).reshape(
                  (NUM_BUCKETS,) + (1,) * (bucket.ndim - bucket_axis))).astype(F32)
    out = jnp.tensordot(rel_bias.astype(F32).T, onehot, axes=((1,), (bucket_axis,)),
                        precision=lax.Precision.HIGHEST)
    return jnp.moveaxis(out, 0, bucket_axis)


def _silu(x):
    return x * jax.nn.sigmoid(x)


def _dot_nt(a, b, precision=None):
    return lax.dot_general(a, b, (((1,), (1,)), ((), ())), precision=precision,
                           preferred_element_type=F32)


def _select_top_blocks(gate, block_id, n_past, axis=-1):
    gate = jnp.where(block_id < n_past, gate, GATE_MASKED)
    id_f = block_id.astype(F32)
    sel = jnp.zeros(gate.shape, jnp.bool_)
    for _ in range(MOBA_TOPK):
        mx = jnp.max(gate, axis=axis, keepdims=True)
        first = jnp.min(jnp.where(gate == mx, id_f, float(LANES)), axis=axis, keepdims=True)
        pick = (id_f == first) & (mx > GATE_VALID_MIN)
        sel = sel | pick
        gate = jnp.where(pick, GATE_MASKED, gate)
    return sel


def _inproj_kernel(x_ref, gain_ref, w_ref, o_ref, xn_ref, *, row_chunk):
    @pl.when(pl.program_id(1) == 0)
    def _():
        for r in range(0, x_ref.shape[0], row_chunk):
            x = x_ref[r:r + row_chunk, :]
            ms = jnp.mean(x * x, axis=-1, keepdims=True)
            xn_ref[r:r + row_chunk, :] = (x * lax.rsqrt(ms + RMS_EPS) * gain_ref[...]).astype(BF16)

    o_ref[...] = jnp.dot(xn_ref[...], w_ref[...], preferred_element_type=F32)


def _in_projection(x, gain, w_bf16, tm, tn):
    m, d = x.shape
    n = w_bf16.shape[1]
    return pl.pallas_call(
        functools.partial(_inproj_kernel, row_chunk=min(tm, 128)),
        grid=(m // tm, n // tn),
        in_specs=[pl.BlockSpec((tm, d), lambda i, j: (i, 0)),
                  pl.BlockSpec((1, d), lambda i, j: (0, 0)),
                  pl.BlockSpec((d, tn), lambda i, j: (0, j))],
        out_specs=pl.BlockSpec((tm, tn), lambda i, j: (i, j)),
        out_shape=jax.ShapeDtypeStruct((m, n), F32),
        scratch_shapes=[pltpu.VMEM((tm, d), BF16)],
        compiler_params=pltpu.CompilerParams(
            dimension_semantics=("parallel", "arbitrary"), vmem_limit_bytes=VMEM_LIMIT),
    )(x, gain.reshape(1, d), w_bf16)


def _stage_kv_kernel(k_ref, v_ref, km_ref, kaug_ref, vaug_ref, krows_ref, vrows_ref, *, blocks_per_step):
    rows = blocks_per_step * MOBA_BLOCK
    k = k_ref[...]
    v = v_ref[...]
    km_ref[...] = jnp.sum(k.reshape(blocks_per_step, MOBA_BLOCK, KV_WIDTH), axis=1) * (1.0 / MOBA_BLOCK)
    block_of_row = (pl.program_id(0) * blocks_per_step
                    + lax.broadcasted_iota(jnp.int32, (rows, LANES), 0) // MOBA_BLOCK)
    onehot = jnp.where(lax.broadcasted_iota(jnp.int32, (rows, LANES), 1) == block_of_row, 1.0, 0.0).astype(BF16)
    ones = jnp.ones((rows, LANES), BF16)
    for gi in range(N_KV_HEADS):
        cols = slice(gi * HEAD_DIM, (gi + 1) * HEAD_DIM)
        kaug_ref[gi, :, 0:HEAD_DIM] = k[:, cols].astype(BF16)
        kaug_ref[gi, :, HEAD_DIM:HEAD_DIM + LANES] = onehot
        vaug_ref[gi, :, 0:HEAD_DIM] = v[:, cols].astype(BF16)
        vaug_ref[gi, :, HEAD_DIM:HEAD_DIM + LANES] = ones
        krows_ref[pl.ds(gi, rows, stride=N_KV_HEADS), :] = k[:, cols]
        vrows_ref[pl.ds(gi, rows, stride=N_KV_HEADS), :] = v[:, cols]


def _stage_kv(z, seq, k_col0, v_col0):
    bps = 8
    n_blk = seq // MOBA_BLOCK
    assert n_blk % bps == 0 and k_col0 % KV_WIDTH == 0 and v_col0 % KV_WIDTH == 0
    rows = bps * MOBA_BLOCK
    aug = jax.ShapeDtypeStruct((N_KV_HEADS, seq, HEAD_DIM + LANES), BF16)
    aug_spec = pl.BlockSpec((N_KV_HEADS, rows, HEAD_DIM + LANES), lambda i: (0, i, 0))
    kv_rows = jax.ShapeDtypeStruct((seq * N_KV_HEADS, HEAD_DIM), F32)
    kv_rows_spec = pl.BlockSpec((rows * N_KV_HEADS, HEAD_DIM), lambda i: (i, 0))
    return pl.pallas_call(
        functools.partial(_stage_kv_kernel, blocks_per_step=bps),
        grid=(n_blk // bps,),
        in_specs=[pl.BlockSpec((rows, KV_WIDTH), lambda i: (i, k_col0 // KV_WIDTH)),
                  pl.BlockSpec((rows, KV_WIDTH), lambda i: (i, v_col0 // KV_WIDTH))],
        out_specs=[pl.BlockSpec((bps, KV_WIDTH), lambda i: (i, 0)), aug_spec, aug_spec,
                   kv_rows_spec, kv_rows_spec],
        out_shape=[jax.ShapeDtypeStruct((n_blk, KV_WIDTH), F32), aug, aug, kv_rows, kv_rows],
        compiler_params=pltpu.CompilerParams(vmem_limit_bytes=VMEM_LIMIT),
    )(z, z)


class _SampleShape(NamedTuple):
    n_seq: int
    dec_seq: int
    n_past: int
    blocks_per_step: int
    steps_per_seq: int


def _attn_kernel(thr_ref, pt_ref,
                 q_ref, kaug_ref, vaug_ref, km_ref, ga_ref, tab_ref, blo_ref, bhi_ref,
                 qs_ref, knew_ref, vnew_ref, gas_ref, *rest, n_blk_pad, sample):
    n_cache_pages = sample.blocks_per_step * PAGES_PER_BLOCK
    k_pages, v_pages = rest[:n_cache_pages], rest[n_cache_pages:2 * n_cache_pages]
    (sbias_ref, sbias_own_ref, o_ref, so_ref, qaug_ref, m_ref, l_ref, acc_ref, dmat_ref,
     qf_ref, qb_ref, gate_ref, mx_ref, sl_ref, part_ref) = rest[2 * n_cache_pages:]
    g = pl.program_id(0)
    i = pl.program_id(1)
    blk = MOBA_BLOCK
    all_rows = KV_GROUP * blk

    def select_blocks():
        dmat_ref[...] = (lax.broadcasted_iota(jnp.int32, (blk, blk), 0)
                         - lax.broadcasted_iota(jnp.int32, (blk, blk), 1))
        m_ref[...] = jnp.full(m_ref.shape, NEG_INF, F32)
        l_ref[...] = jnp.zeros(l_ref.shape, F32)
        acc_ref[...] = jnp.zeros(acc_ref.shape, F32)
        q_all = jnp.concatenate(
            [q_ref[:, hh * HEAD_DIM:(hh + 1) * HEAD_DIM] for hh in range(KV_GROUP)], axis=0)
        gate_t = _dot_nt(km_ref[0:n_blk_pad, :], q_all, precision=lax.Precision.HIGHEST)
        blk_id = lax.broadcasted_iota(jnp.int32, (n_blk_pad, all_rows), 0)
        sel_t = _select_top_blocks(gate_t, blk_id, i, axis=0)
        mask_t = jnp.where(sel_t | (blk_id == i), 0.0, NEG_INF)
        if n_blk_pad < LANES:
            mask_t = jnp.concatenate([mask_t, jnp.zeros((LANES - n_blk_pad, all_rows), F32)], axis=0)
        qaug_ref[:, 0:HEAD_DIM] = (q_all * (ATTN_SCALE * LOG2_E)).astype(BF16)
        qaug_ref[:, HEAD_DIM:HEAD_DIM + LANES] = mask_t.T.astype(BF16)

    step = g * pl.num_programs(1) + i
    part = step % sample.steps_per_seq
    sample_init, sample_past, sample_own = _sample_phases(
        part, sample, qs_ref, knew_ref, vnew_ref, gas_ref, k_pages, v_pages, sbias_ref, sbias_own_ref,
        so_ref, qf_ref, qb_ref, gate_ref, mx_ref, sl_ref, part_ref)
    sample_steps = sample.n_seq * sample.steps_per_seq
    if sample_steps == N_KV_HEADS * (kaug_ref.shape[1] // blk):
        pl.when(part == 0)(sample_init)
        select_blocks()
        sample_past()
        pl.when(part == sample.steps_per_seq - 1)(sample_own)
    else:
        @pl.when(step < sample_steps)
        def _sample_chunk():
            pl.when(part == 0)(sample_init)
            sample_past()
            pl.when(part == sample.steps_per_seq - 1)(sample_own)

        select_blocks()

    def sweep(groups):
        m_all, l_all, acc_all = m_ref[...], l_ref[...], acc_ref[...]
        chunks = [slice(c, c + SWEEP_ROWS) for c in range(0, all_rows, SWEEP_ROWS)]
        m_c = [m_all[rows] for rows in chunks]
        l_c = [l_all[rows] for rows in chunks]
        acc_c = [acc_all[rows] for rows in chunks]
        logits = []
        for n0, nb, bias in groups:
            r = pl.multiple_of(n0 * blk, blk)
            keys = kaug_ref[0, pl.ds(r, nb * blk), :]
            logits.append([_dot_nt(qaug_ref[rows, :], keys) + bias[rows] for rows in chunks])
        for (n0, nb, bias), group_logits in zip(groups, logits):
            r = pl.multiple_of(n0 * blk, blk)
            values = vaug_ref[0, pl.ds(r, nb * blk), :]
            for ci, rows in enumerate(chunks):
                sc = group_logits[ci]
                m_new = jnp.maximum(m_c[ci], jnp.max(sc, axis=-1, keepdims=True))
                alpha = jnp.exp2(m_c[ci] - m_new)
                p = jnp.exp2(sc - jnp.concatenate([m_new] * (nb * blk // LANES), axis=1))
                pv = jnp.dot(p.astype(BF16), values, preferred_element_type=F32)
                l_c[ci] = alpha * l_c[ci] + pv[:, HEAD_DIM:]
                acc_c[ci] = alpha * acc_c[ci] + pv[:, :HEAD_DIM]
                m_c[ci] = m_new
        m_ref[...] = jnp.concatenate(m_c, axis=0)
        l_ref[...] = jnp.concatenate(l_c, axis=0)
        acc_ref[...] = jnp.concatenate(acc_c, axis=0)

    def far_bias(n0, nb):
        tiles = []
        for j in range(nb):
            delta = i - (n0 + j)
            high = dmat_ref[...] >= thr_ref[delta]
            base = delta * N_HEADS + g * KV_GROUP
            tiles.append(jnp.concatenate([jnp.where(high, bhi_ref[base + hh], blo_ref[base + hh])
                                          for hh in range(KV_GROUP)], axis=0))
        return jnp.concatenate(tiles, axis=1)

    n_far = jnp.maximum(i - (TABLE_DELTAS - 1), 0)
    per_iter = FAR_GROUPS * FAR_GROUP_BLOCKS

    def far_step(t, carry):
        first = t * per_iter
        sweep([(first + j * FAR_GROUP_BLOCKS, FAR_GROUP_BLOCKS,
                far_bias(first + j * FAR_GROUP_BLOCKS, FAR_GROUP_BLOCKS)) for j in range(FAR_GROUPS)])
        return carry

    lax.fori_loop(0, n_far // per_iter, far_step, 0)

    left = n_far % per_iter
    near = jnp.minimum(i + 1, TABLE_DELTAS)
    for count in range(1, TABLE_DELTAS + 1):
        for rem in range(per_iter if count == TABLE_DELTAS else 1):
            @pl.when((near == count) & (left == rem))
            def _tail(count=count, rem=rem):
                groups = []
                for first in range(0, rem, FAR_GROUP_BLOCKS):
                    nb = min(FAR_GROUP_BLOCKS, rem - first)
                    groups.append((n_far - rem + first, nb, far_bias(n_far - rem + first, nb)))
                groups.append((i - (count - 1), count, jnp.concatenate(
                    [tab_ref[count - 1 - j].reshape(all_rows, blk) for j in range(count)], axis=1)))
                sweep(groups)

    attn = acc_ref[...] / l_ref[...]
    for hh in range(KV_GROUP):
        cols = slice(hh * HEAD_DIM, (hh + 1) * HEAD_DIM)
        o_ref[:, cols] = (attn[hh * blk:(hh + 1) * blk, :] * _silu(ga_ref[:, cols])).astype(BF16)


def _block_of_pages(page_refs):
    return jnp.concatenate(
        [jnp.concatenate([ref[0, pl.ds(gi, PAGE_SIZE, stride=N_KV_HEADS), :] for gi in range(N_KV_HEADS)],
                         axis=1) for ref in page_refs], axis=0)


def _sample_phases(part, sample, q_ref, knew_ref, vnew_ref, ga_ref, k_pages, v_pages, bias_ref, bias_own_ref,
                   o_ref, qf_ref, qb_ref, gate_ref, mx_ref, l_ref, part_ref):
    dec_seq, n_past, blocks_per_step = sample.dec_seq, sample.n_past, sample.blocks_per_step
    rows = N_HEADS * dec_seq
    grp_rows = KV_GROUP * dec_seq

    def _init():
        gate_ref[...] = jnp.zeros(gate_ref.shape, F32)
        mx_ref[...] = jnp.zeros(mx_ref.shape, F32)
        l_ref[...] = jnp.zeros(l_ref.shape, F32)
        col_grp = lax.broadcasted_iota(jnp.int32, (dec_seq, KV_WIDTH), 1) // HEAD_DIM
        for h in range(N_HEADS):
            qh = q_ref[:, h * HEAD_DIM:(h + 1) * HEAD_DIM]
            wide = jnp.concatenate([qh] * N_KV_HEADS, axis=1)
            wide = jnp.where(col_grp == h // KV_GROUP, wide, 0.0)
            qf_ref[h * dec_seq:(h + 1) * dec_seq, :] = wide
            qb_ref[h * dec_seq:(h + 1) * dec_seq, :] = (wide * ATTN_SCALE).astype(BF16)

    def block_partials(kblks, vblks, biases):
        scs = [_dot_nt(qb_ref[...], kblk.astype(BF16)) + bias for kblk, bias in zip(kblks, biases)]
        ms = [jnp.max(sc, axis=-1, keepdims=True) for sc in scs]
        ps = [jnp.exp(sc - m_n) for sc, m_n in zip(scs, ms)]
        ls = [jnp.sum(p, axis=-1, keepdims=True) for p in ps]
        outs = []
        for p, vblk in zip(ps, vblks):
            o_all = jnp.dot(p.astype(BF16), vblk.astype(BF16), preferred_element_type=F32)
            outs.append(jnp.concatenate(
                [o_all[gi * grp_rows:(gi + 1) * grp_rows, gi * HEAD_DIM:(gi + 1) * HEAD_DIM]
                 for gi in range(N_KV_HEADS)], axis=0))
        return ms, ls, outs

    def _past():
        lane = lax.broadcasted_iota(jnp.int32, (rows, LANES), 1)
        kblks = [_block_of_pages(k_pages[jb * PAGES_PER_BLOCK:(jb + 1) * PAGES_PER_BLOCK])
                 for jb in range(blocks_per_step)]
        vblks = [_block_of_pages(v_pages[jb * PAGES_PER_BLOCK:(jb + 1) * PAGES_PER_BLOCK])
                 for jb in range(blocks_per_step)]
        ms, lsums, outs = block_partials(kblks, vblks, [bias_ref[jb] for jb in range(blocks_per_step)])
        mx, ls, gt = mx_ref[...], l_ref[...], gate_ref[...]
        for jb in range(blocks_per_step):
            n = part * blocks_per_step + jb
            kmean = jnp.sum(kblks[jb], axis=0, keepdims=True) * (1.0 / MOBA_BLOCK)
            gate = jnp.sum(qf_ref[...] * kmean, axis=-1, keepdims=True)
            part_ref[n] = outs[jb]
            here = lane == n
            mx = jnp.where(here, ms[jb], mx)
            ls = jnp.where(here, lsums[jb], ls)
            gt = jnp.where(here, gate, gt)
        mx_ref[...] = mx
        l_ref[...] = ls
        gate_ref[...] = gt

    def _own():
        lane = lax.broadcasted_iota(jnp.int32, (rows, LANES), 1)
        pad = jnp.zeros((MOBA_BLOCK - dec_seq, KV_WIDTH), F32)
        (m_n,), (l_n,), (o_n,) = block_partials([jnp.concatenate([knew_ref[...], pad], axis=0)],
                                                [jnp.concatenate([vnew_ref[...], pad], axis=0)],
                                                [bias_own_ref[0]])
        own = lane == n_past
        mx = jnp.where(own, m_n, mx_ref[...])
        ls = jnp.where(own, l_n, l_ref[...])
        sel = _select_top_blocks(gate_ref[...], lane, n_past)
        include = sel | own
        m_tot = jnp.max(jnp.where(include, mx, GATE_MASKED), axis=-1, keepdims=True)
        e = jnp.where(include, jnp.exp(jnp.minimum(mx - m_tot, 0.0)), 0.0)
        denom = jnp.sum(e * ls, axis=-1, keepdims=True)
        num = e[:, n_past:n_past + 1] * o_n
        for j in range(n_past):
            num = num + e[:, j:j + 1] * part_ref[j]
        attn = num / denom
        for h in range(N_HEADS):
            cols = slice(h * HEAD_DIM, (h + 1) * HEAD_DIM)
            o_ref[0, :, cols] = (attn[h * dec_seq:(h + 1) * dec_seq, :] * _silu(ga_ref[:, cols])).astype(BF16)

    return _init, _past, _own


def _prompt_bias_tables(rel_bias, n_blk):
    blk = MOBA_BLOCK
    r = jnp.arange(blk, dtype=jnp.int32)
    dist = (jnp.arange(TABLE_DELTAS, dtype=jnp.int32)[:, None, None] * blk
            + r[None, :, None] - r[None, None, :])
    tab = _bias_of_bucket(rel_bias, _rel_bucket(dist), 1)
    tab = jnp.where((dist >= 0)[:, None], tab * LOG2_E, NEG_INF)

    nd = max(n_blk, TABLE_DELTAS + 1)
    off = np.arange(-(blk - 1), blk, dtype=np.int32)
    dist_np = np.arange(nd, dtype=np.int32)[:, None] * blk + off[None, :]
    buckets_np = _rel_bucket_np(dist_np)
    span = buckets_np[TABLE_DELTAS:, -1] - buckets_np[TABLE_DELTAS:, 0]
    assert np.all(np.diff(buckets_np[TABLE_DELTAS:], axis=1) >= 0) and np.all(span <= 1), \
        "more than one bias bucket boundary inside a far block pair"
    buckets = _rel_bucket(jnp.asarray(dist_np))
    b_lo, b_hi = buckets[:, 0], buckets[:, -1]
    first_hi = jnp.argmax(buckets == b_hi[:, None], axis=1).astype(jnp.int32)
    thr = first_hi - (blk - 1)
    blo = (rel_bias.astype(F32)[b_lo] * LOG2_E).reshape(-1)
    bhi = (rel_bias.astype(F32)[b_hi] * LOG2_E).reshape(-1)
    return tab, thr, blo, bhi


def _sample_bias_table(rel_bias, past_len, dec_seq, n_blocks):
    kpos = jnp.arange(n_blocks * MOBA_BLOCK, dtype=jnp.int32).reshape(n_blocks, 1, MOBA_BLOCK)
    qpos = (past_len + jnp.arange(dec_seq, dtype=jnp.int32)).reshape(1, dec_seq, 1)
    dist = qpos - kpos
    vals = _bias_of_bucket(rel_bias, _rel_bucket(dist), 1)
    vals = jnp.where((dist >= 0)[:, None], vals, NEG_INF)
    return vals.reshape(n_blocks, N_HEADS * dec_seq, MOBA_BLOCK)


def _attention(z, kmean_pad, kaug, vaug, rel_bias, seq, ga_col0,
               q_s, k_s, v_s, ga_s, cache_k, cache_v, page_table, past_len):
    blk = MOBA_BLOCK
    n_blk = seq // blk
    assert n_blk <= LANES
    tab, thr, blo, bhi = _prompt_bias_tables(rel_bias, n_blk)
    gc = ga_col0 // GROUP_WIDTH

    bsz, n_pages = page_table.shape
    dec_seq = q_s.shape[0] // bsz
    n_past = past_len // blk
    bps = math.gcd(SAMPLE_BLOCKS_PER_STEP, n_past)
    assert past_len % blk == 0 and n_past + 1 <= LANES and n_pages == n_past * PAGES_PER_BLOCK
    sample = _SampleShape(bsz, dec_seq, n_past, bps, n_past // bps)
    sample_steps = sample.n_seq * sample.steps_per_seq
    assert sample_steps <= N_KV_HEADS * n_blk, "sample attention chunks must fit in the prompt grid"
    pages_per_step = bps * PAGES_PER_BLOCK
    sbias = _sample_bias_table(rel_bias, past_len, dec_seq, n_past + 1)
    rows = N_HEADS * dec_seq

    def sample_pos(g, i):
        s = jnp.minimum(g * n_blk + i, sample_steps - 1)
        return s // sample.steps_per_seq, s % sample.steps_per_seq

    def page_spec(which):
        def index_map(g, i, th, pt):
            b, part = sample_pos(g, i)
            return (pt[b * n_pages + part * pages_per_step + which], 0, 0)
        return pl.BlockSpec((1, PAGE_SIZE * N_KV_HEADS, HEAD_DIM), index_map)

    row_spec = lambda width: pl.BlockSpec((dec_seq, width), lambda g, i, th, pt: (sample_pos(g, i)[0], 0))
    resident = dict(pipeline_mode=pl.Buffered(1))
    smem = pl.BlockSpec(memory_space=pltpu.SMEM)
    grid_spec = pltpu.PrefetchScalarGridSpec(
        num_scalar_prefetch=2,
        grid=(N_KV_HEADS, n_blk),
        in_specs=([
            pl.BlockSpec((blk, GROUP_WIDTH), lambda g, i, th, pt: (i, g)),
            pl.BlockSpec((1, seq, HEAD_DIM + LANES), lambda g, i, th, pt: (g, 0, 0), **resident),
            pl.BlockSpec((1, seq, HEAD_DIM + LANES), lambda g, i, th, pt: (g, 0, 0), **resident),
            pl.BlockSpec((LANES, HEAD_DIM), lambda g, i, th, pt: (0, g)),
            pl.BlockSpec((blk, GROUP_WIDTH), lambda g, i, th, pt: (i, gc + g)),
            pl.BlockSpec((TABLE_DELTAS, KV_GROUP, blk, blk), lambda g, i, th, pt: (0, g, 0, 0), **resident),
            smem, smem,
            row_spec(ATTN_WIDTH), row_spec(KV_WIDTH), row_spec(KV_WIDTH), row_spec(ATTN_WIDTH)]
            + [page_spec(j) for j in range(pages_per_step)]
            + [page_spec(j) for j in range(pages_per_step)]
            + [pl.BlockSpec((bps, rows, blk), lambda g, i, th, pt: (sample_pos(g, i)[1], 0, 0)),
               pl.BlockSpec((1, rows, blk), lambda g, i, th, pt: (n_past, 0, 0))]),
        out_specs=[pl.BlockSpec((blk, GROUP_WIDTH), lambda g, i, th, pt: (i, g)),
                   pl.BlockSpec((1, dec_seq, ATTN_WIDTH), lambda g, i, th, pt: (sample_pos(g, i)[0], 0, 0))],
        scratch_shapes=[
            pltpu.VMEM((KV_GROUP * blk, HEAD_DIM + LANES), BF16),
            pltpu.VMEM((KV_GROUP * blk, LANES), F32),
            pltpu.VMEM((KV_GROUP * blk, LANES), F32),
            pltpu.VMEM((KV_GROUP * blk, HEAD_DIM), F32),
            pltpu.VMEM((blk, blk), jnp.int32),
            pltpu.VMEM((rows, KV_WIDTH), F32),
            pltpu.VMEM((rows, KV_WIDTH), BF16),
            pltpu.VMEM((rows, LANES), F32),
            pltpu.VMEM((rows, LANES), F32),
            pltpu.VMEM((rows, LANES), F32),
            pltpu.VMEM((n_past, rows, HEAD_DIM), F32),
        ],
    )
    ya, ya_s = pl.pallas_call(
        functools.partial(_attn_kernel, n_blk_pad=-(-n_blk // 8) * 8, sample=sample),
        grid_spec=grid_spec,
        out_shape=[jax.ShapeDtypeStruct((seq, ATTN_WIDTH), BF16),
                   jax.ShapeDtypeStruct((bsz, dec_seq, ATTN_WIDTH), BF16)],
        compiler_params=pltpu.CompilerParams(
            dimension_semantics=("arbitrary", "arbitrary"), vmem_limit_bytes=VMEM_LIMIT),
    )(thr, page_table.reshape(-1), z, kaug, vaug, kmean_pad, z, tab, blo, bhi, q_s, k_s, v_s, ga_s,
      *([cache_k] * pages_per_step), *([cache_v] * pages_per_step), sbias, sbias)
    return ya, ya_s.reshape(bsz * dec_seq, ATTN_WIDTH)


def _window_mix(ext_ref, window, rows, cols, count, u, gp, w, scale):
    win = u
    for j in range(1, window):
        win = win + ext_ref[POOL_HALO - j:POOL_HALO - j + rows, cols]
    zed = win / count - u
    y = jnp.dot(zed.astype(BF16), w, preferred_element_type=F32)
    return (y * scale) * _silu(gp)


def _pool_prompt_kernel(prev_ref, u_ref, gp_ref, w_ref, scale_ref, o_ref, ext_ref, *, groups_per_step):
    i = pl.program_id(0)
    rows = u_ref.shape[0]
    gw = w_ref.shape[1]
    ext_ref[0:POOL_HALO, :] = jnp.where(i == 0, 0.0, prev_ref[...])
    ext_ref[POOL_HALO:POOL_HALO + rows, :] = u_ref[...]
    pos = i * rows + lax.broadcasted_iota(jnp.int32, (rows, 1), 0)
    for part in range(N_POOL_GROUPS // groups_per_step):
        @pl.when(pl.program_id(1) == part)
        def _(part=part):
            for gl in range(groups_per_step):
                window = POOL_WINDOWS[part * groups_per_step + gl]
                cols = slice(gl * gw, (gl + 1) * gw)
                count = jnp.minimum(window, pos + 1).astype(F32)
                y = _window_mix(ext_ref, window, rows, cols, count, u_ref[:, cols], gp_ref[:, cols],
                                w_ref[gl], scale_ref[:, cols])
                o_ref[:, cols] = y.astype(BF16)


def _pool_prompt(z, w_pool_bf16, pool_scale, seq, u_col0, gp_col0, tm):
    n_groups, gw, _ = w_pool_bf16.shape
    gps = 2
    bw = gps * gw
    assert n_groups == N_POOL_GROUPS and u_col0 % bw == 0 and gp_col0 % bw == 0 and tm % POOL_HALO == 0
    uc, gc = u_col0 // bw, gp_col0 // bw
    halo_blocks = tm // POOL_HALO
    return pl.pallas_call(
        functools.partial(_pool_prompt_kernel, groups_per_step=gps),
        grid=(seq // tm, n_groups // gps),
        in_specs=[pl.BlockSpec((POOL_HALO, bw), lambda i, c: (jnp.maximum(i * halo_blocks - 1, 0), uc + c)),
                  pl.BlockSpec((tm, bw), lambda i, c: (i, uc + c)),
                  pl.BlockSpec((tm, bw), lambda i, c: (i, gc + c)),
                  pl.BlockSpec((gps, gw, gw), lambda i, c: (c, 0, 0)),
                  pl.BlockSpec((1, bw), lambda i, c: (0, c))],
        out_specs=pl.BlockSpec((tm, bw), lambda i, c: (i, c)),
        out_shape=jax.ShapeDtypeStruct((seq, n_groups * gw), BF16),
        scratch_shapes=[pltpu.VMEM((POOL_HALO + tm, bw), F32)],
        compiler_params=pltpu.CompilerParams(
            dimension_semantics=("parallel", "arbitrary"), vmem_limit_bytes=VMEM_LIMIT),
    )(z, z, z, w_pool_bf16, pool_scale.reshape(1, n_groups * gw))


def _pool_sample_kernel(ext_in_ref, gp_ref, w_ref, scale_ref, o_ref, ext_ref, *, bsz, dec_seq):
    seg = POOL_HALO + dec_seq
    rows = bsz * seg
    gw = w_ref.shape[1]
    ext_ref[0:POOL_HALO, :] = jnp.zeros((POOL_HALO, ext_ref.shape[1]), F32)
    ext_ref[POOL_HALO:POOL_HALO + rows, :] = ext_in_ref[...]
    for gi, window in enumerate(POOL_WINDOWS):
        cols = slice(gi * gw, (gi + 1) * gw)
        u_all = ext_in_ref[:, cols]
        win = u_all
        for j in range(1, window):
            win = win + ext_ref[POOL_HALO - j:POOL_HALO - j + rows, cols]
        zed = (win / float(window) - u_all).reshape(bsz, seg, gw)[:, POOL_HALO:, :].reshape(bsz * dec_seq, gw)
        y = jnp.dot(zed.astype(BF16), w_ref[gi], preferred_element_type=F32)
        o_ref[:, cols] = ((y * scale_ref[:, cols]) * _silu(gp_ref[:, cols])).astype(BF16)


def _pool_sample(u_ext, gp_s, w_pool_bf16, pool_scale, bsz, dec_seq):
    pw = u_ext.shape[1]
    rows = u_ext.shape[0]
    vm = pl.BlockSpec(memory_space=pltpu.VMEM)
    return pl.pallas_call(
        functools.partial(_pool_sample_kernel, bsz=bsz, dec_seq=dec_seq),
        in_specs=[vm, vm, vm, vm],
        out_specs=vm,
        out_shape=jax.ShapeDtypeStruct((bsz * dec_seq, pw), BF16),
        scratch_shapes=[pltpu.VMEM((POOL_HALO + rows, pw), F32)],
        compiler_params=pltpu.CompilerParams(vmem_limit_bytes=VMEM_LIMIT),
    )(u_ext, gp_s, w_pool_bf16, pool_scale.reshape(1, pw))


def _outproj_kernel(ya_ref, yp_ref, w_ref, x_ref, gain_ref, o_ref, *, k_attn):
    k = pl.program_id(1)

    @pl.when(k == 0)
    def _():
        o_ref[...] = x_ref[...] + jnp.dot(ya_ref[...], w_ref[...], preferred_element_type=F32)

    @pl.when((k > 0) & (k < k_attn))
    def _():
        o_ref[...] += jnp.dot(ya_ref[...], w_ref[...], preferred_element_type=F32)

    @pl.when(k >= k_attn)
    def _():
        o_ref[...] += jnp.dot(yp_ref[...], w_ref[...], preferred_element_type=F32)

    @pl.when(k == pl.num_programs(1) - 1)
    def _():
        h = o_ref[...]
        ms = jnp.mean(h * h, axis=-1, keepdims=True)
        o_ref[...] = h * lax.rsqrt(ms + RMS_EPS) * gain_ref[...]


def _out_projection(ya, yp, w_bf16, x, gain, tm, tk):
    m, d = x.shape
    ka, kp = ya.shape[1] // tk, yp.shape[1] // tk
    return pl.pallas_call(
        functools.partial(_outproj_kernel, k_attn=ka),
        grid=(m // tm, ka + kp),
        in_specs=[pl.BlockSpec((tm, tk), lambda i, k: (i, jnp.minimum(k, ka - 1))),
                  pl.BlockSpec((tm, tk), lambda i, k: (i, jnp.maximum(k - ka, 0))),
                  pl.BlockSpec((tk, d), lambda i, k: (k, 0)),
                  pl.BlockSpec((tm, d), lambda i, k: (i, 0)),
                  pl.BlockSpec((1, d), lambda i, k: (0, 0))],
        out_specs=pl.BlockSpec((tm, d), lambda i, k: (i, 0)),
        out_shape=jax.ShapeDtypeStruct((m, d), F32),
        compiler_params=pltpu.CompilerParams(
            dimension_semantics=("parallel", "arbitrary"), vmem_limit_bytes=VMEM_LIMIT),
    )(ya, yp, w_bf16, x, gain.reshape(1, d))


def kernel(x_prompt, x_sample, cache_k, cache_v, state_pool, page_table, norm_in, w_in, w_pool,
           pool_scale, w_out, rel_bias, norm_out):
    depth = norm_in.shape[0]
    assert depth == 1, "single layer step"
    n_batch, seq, d_model = x_prompt.shape
    bsz, dec_seq, _ = x_sample.shape
    assert n_batch == 1 and seq % (8 * MOBA_BLOCK) == 0 and dec_seq == 8
    past_len = page_table.shape[1] * PAGE_SIZE
    assert past_len >= POOL_HIST
    pool_width = pool_scale.shape[1]
    k0 = ATTN_WIDTH
    v0 = k0 + KV_WIDTH
    ga0 = v0 + KV_WIDTH
    u0 = ga0 + ATTN_WIDTH
    gp0 = u0 + pool_width
    assert w_in.shape[2] == gp0 + pool_width and w_out.shape[1] == ATTN_WIDTH + pool_width

    w_in_b = w_in[0].astype(BF16)
    w_out_b = w_out[0].astype(BF16)
    w_pool_b = w_pool[0].astype(BF16)
    xp = x_prompt[0]
    xs = x_sample.reshape(bsz * dec_seq, d_model)

    z = _in_projection(xp, norm_in[0], w_in_b, tm=512, tn=1536)
    zs = _in_projection(xs, norm_in[0], w_in_b, tm=bsz * dec_seq, tn=1024)
    q_s, k_s, v_s = zs[:, :k0], zs[:, k0:v0], zs[:, v0:ga0]
    ga_s, u_s, gp_s = zs[:, ga0:u0], zs[:, u0:gp0], zs[:, gp0:]

    n_blk = seq // MOBA_BLOCK
    kmean, kaug, vaug, k_rows, v_rows = _stage_kv(z, seq, k0, v0)
    kmean_pad = jnp.pad(kmean, ((0, LANES - n_blk), (0, 0)))
    n_phys = cache_k.shape[1]
    page_rows = PAGE_SIZE * N_KV_HEADS
    ya, ya_s = _attention(z, kmean_pad, kaug, vaug, rel_bias, seq, ga0, q_s, k_s, v_s, ga_s,
                          cache_k.reshape(n_phys, page_rows, HEAD_DIM),
                          cache_v.reshape(n_phys, page_rows, HEAD_DIM), page_table, past_len)
    yp = _pool_prompt(z, w_pool_b, pool_scale[0], seq, u0, gp0, tm=512)
    y_prompt = _out_projection(ya, yp, w_out_b, xp, norm_out, tm=512, tk=512)

    u_s3 = u_s.reshape(bsz, dec_seq, pool_width)
    u_ext = jnp.concatenate([jnp.zeros((bsz, 1, pool_width), F32), state_pool[0].astype(F32), u_s3], axis=1)
    yp_s = _pool_sample(u_ext.reshape(bsz * (POOL_HALO + dec_seq), pool_width), gp_s, w_pool_b,
                        pool_scale[0], bsz, dec_seq)
    y_sample = _out_projection(ya_s, yp_s, w_out_b, xs, norm_out, tm=bsz * dec_seq, tk=512)

    k_prompt = k_rows.reshape(1, 1, seq, N_KV_HEADS, HEAD_DIM)
    v_prompt = v_rows.reshape(1, 1, seq, N_KV_HEADS, HEAD_DIM)
    pool_prompt = z[seq - POOL_HIST:, u0:gp0].reshape(1, 1, POOL_HIST, pool_width)
    k_sample = k_s.reshape(1, bsz, dec_seq, N_KV_HEADS, HEAD_DIM)
    v_sample = v_s.reshape(1, bsz, dec_seq, N_KV_HEADS, HEAD_DIM)
    pool_sample = u_ext[:, 1 + dec_seq:, :].reshape(1, bsz, POOL_HIST, pool_width).astype(state_pool.dtype)
    return (y_prompt.reshape(1, seq, d_model), y_sample.reshape(bsz, dec_seq, d_model),
            k_prompt, v_prompt, pool_prompt, k_sample, v_sample, pool_sample)
```

```python
import functools
import math
from typing import NamedTuple

import numpy as np
import jax
import jax.numpy as jnp
from jax import lax
from jax.experimental import pallas as pl
from jax.experimental.pallas import tpu as pltpu

F32 = jnp.float32
BF16 = jnp.bfloat16

HEAD_DIM = 128
N_HEADS = 16
N_KV_HEADS = 4
KV_GROUP = N_HEADS // N_KV_HEADS
ATTN_WIDTH = N_HEADS * HEAD_DIM
KV_WIDTH = N_KV_HEADS * HEAD_DIM
GROUP_WIDTH = KV_GROUP * HEAD_DIM
ATTN_SCALE = HEAD_DIM ** -0.5
LOG2_E = math.log2(math.e)
MOBA_BLOCK = 256
MOBA_TOPK = 3
PAGE_SIZE = 128
PAGES_PER_BLOCK = MOBA_BLOCK // PAGE_SIZE
POOL_WINDOWS = (2, 4, 8, 16)
N_POOL_GROUPS = len(POOL_WINDOWS)
POOL_HIST = max(POOL_WINDOWS) - 1
POOL_HALO = POOL_HIST + 1
NUM_BUCKETS = 32
MAX_EXACT = NUM_BUCKETS // 2
MAX_DISTANCE = 4096
RMS_EPS = 1e-6
NEG_INF = -1e30
GATE_MASKED = -3.0e38
GATE_VALID_MIN = -1.0e38
LANES = 128
TABLE_DELTAS = 3
SWEEP_ROWS = 256
FAR_GROUP_BLOCKS = 8
FAR_GROUPS = 1
SAMPLE_BLOCKS_PER_STEP = 8

VMEM_LIMIT = 56 * 1024 * 1024


def _rel_bucket(dist):
    n = jnp.maximum(dist, 0)
    nf = jnp.maximum(n, 1).astype(F32)
    large = MAX_EXACT + (jnp.log(nf / MAX_EXACT) / math.log(MAX_DISTANCE / MAX_EXACT)
                         * (NUM_BUCKETS - MAX_EXACT)).astype(jnp.int32)
    large = jnp.minimum(large, NUM_BUCKETS - 1)
    return jnp.where(n < MAX_EXACT, n, large)


def _rel_bucket_np(dist):
    n = np.maximum(dist, 0)
    nf = np.maximum(n, 1).astype(np.float32)
    large = MAX_EXACT + (np.log(nf / np.float32(MAX_EXACT)) / np.float32(math.log(MAX_DISTANCE / MAX_EXACT))
                         * np.float32(NUM_BUCKETS - MAX_EXACT)).astype(np.int32)
    large = np.minimum(large, NUM_BUCKETS - 1)
    return np.where(n < MAX_EXACT, n, large)


def _bias_of_bucket(rel_bias, bucket, bucket_axis):
    onehot = (jnp.expand_dims(bucket, bucket_axis)
              == jnp.arange(NUM_BUCKETS, dtype=jnp.int32).reshape(
                  (NUM_BUCKETS,) + (1,) * (bucket.ndim - bucket_axis))).astype(F32)
    out = jnp.tensordot(rel_bias.astype(F32).T, onehot, axes=((1,), (bucket_axis,)),
                        precision=lax.Precision.HIGHEST)
    return jnp.moveaxis(out, 0, bucket_axis)


def _silu(x):
    return x * jax.nn.sigmoid(x)


def _dot_nt(a, b, precision=None):
    return lax.dot_general(a, b, (((1,), (1,)), ((), ())), precision=precision,
                           preferred_element_type=F32)


def _select_top_blocks(gate, block_id, n_past, axis=-1):
    gate = jnp.where(block_id < n_past, gate, GATE_MASKED)
    id_f = block_id.astype(F32)
    sel = jnp.zeros(gate.shape, jnp.bool_)
    for _ in range(MOBA_TOPK):
        mx = jnp.max(gate, axis=axis, keepdims=True)
        first = jnp.min(jnp.where(gate == mx, id_f, float(LANES)), axis=axis, keepdims=True)
        pick = (id_f == first) & (mx > GATE_VALID_MIN)
        sel = sel | pick
        gate = jnp.where(pick, GATE_MASKED, gate)
    return sel


def _inproj_kernel(x_ref, gain_ref, w_ref, o_ref, xn_ref, *, row_chunk):
    @pl.when(pl.program_id(1) == 0)
    def _():
        for r in range(0, x_ref.shape[0], row_chunk):
            x = x_ref[r:r + row_chunk, :]
            ms = jnp.mean(x * x, axis=-1, keepdims=True)
            xn_ref[r:r + row_chunk, :] = (x * lax.rsqrt(ms + RMS_EPS) * gain_ref[...]).astype(BF16)

    o_ref[...] = jnp.dot(xn_ref[...], w_ref[...], preferred_element_type=F32)


def _in_projection(x, gain, w_bf16, tm, tn):
    m, d = x.shape
    n = w_bf16.shape[1]
    return pl.pallas_call(
        functools.partial(_inproj_kernel, row_chunk=min(tm, 128)),
        grid=(m // tm, n // tn),
        in_specs=[pl.BlockSpec((tm, d), lambda i, j: (i, 0)),
                  pl.BlockSpec((1, d), lambda i, j: (0, 0)),
                  pl.BlockSpec((d, tn), lambda i, j: (0, j))],
        out_specs=pl.BlockSpec((tm, tn), lambda i, j: (i, j)),
        out_shape=jax.ShapeDtypeStruct((m, n), F32),
        scratch_shapes=[pltpu.VMEM((tm, d), BF16)],
        compiler_params=pltpu.CompilerParams(
            dimension_semantics=("parallel", "arbitrary"), vmem_limit_bytes=VMEM_LIMIT),
    )(x, gain.reshape(1, d), w_bf16)


def _stage_kv_kernel(k_ref, v_ref, km_ref, kaug_ref, vaug_ref, krows_ref, vrows_ref, *, blocks_per_step):
    rows = blocks_per_step * MOBA_BLOCK
    k = k_ref[...]
    v = v_ref[...]
    km_ref[...] = jnp.sum(k.reshape(blocks_per_step, MOBA_BLOCK, KV_WIDTH), axis=1) * (1.0 / MOBA_BLOCK)
    block_of_row = (pl.program_id(0) * blocks_per_step
                    + lax.broadcasted_iota(jnp.int32, (rows, LANES), 0) // MOBA_BLOCK)
    onehot = jnp.where(lax.broadcasted_iota(jnp.int32, (rows, LANES), 1) == block_of_row, 1.0, 0.0).astype(BF16)
    ones = jnp.ones((rows, LANES), BF16)
    for gi in range(N_KV_HEADS):
        cols = slice(gi * HEAD_DIM, (gi + 1) * HEAD_DIM)
        kaug_ref[gi, :, 0:HEAD_DIM] = k[:, cols].astype(BF16)
        kaug_ref[gi, :, HEAD_DIM:HEAD_DIM + LANES] = onehot
        vaug_ref[gi, :, 0:HEAD_DIM] = v[:, cols].astype(BF16)
        vaug_ref[gi, :, HEAD_DIM:HEAD_DIM + LANES] = ones
        krows_ref[pl.ds(gi, rows, stride=N_KV_HEADS), :] = k[:, cols]
        vrows_ref[pl.ds(gi, rows, stride=N_KV_HEADS), :] = v[:, cols]


def _stage_kv(z, seq, k_col0, v_col0):
    bps = 8
    n_blk = seq // MOBA_BLOCK
    assert n_blk % bps == 0 and k_col0 % KV_WIDTH == 0 and v_col0 % KV_WIDTH == 0
    rows = bps * MOBA_BLOCK
    aug = jax.ShapeDtypeStruct((N_KV_HEADS, seq, HEAD_DIM + LANES), BF16)
    aug_spec = pl.BlockSpec((N_KV_HEADS, rows, HEAD_DIM + LANES), lambda i: (0, i, 0))
    kv_rows = jax.ShapeDtypeStruct((seq * N_KV_HEADS, HEAD_DIM), F32)
    kv_rows_spec = pl.BlockSpec((rows * N_KV_HEADS, HEAD_DIM), lambda i: (i, 0))
    return pl.pallas_call(
        functools.partial(_stage_kv_kernel, blocks_per_step=bps),
        grid=(n_blk // bps,),
        in_specs=[pl.BlockSpec((rows, KV_WIDTH), lambda i: (i, k_col0 // KV_WIDTH)),
                  pl.BlockSpec((rows, KV_WIDTH), lambda i: (i, v_col0 // KV_WIDTH))],
        out_specs=[pl.BlockSpec((bps, KV_WIDTH), lambda i: (i, 0)), aug_spec, aug_spec,
                   kv_rows_spec, kv_rows_spec],
        out_shape=[jax.ShapeDtypeStruct((n_blk, KV_WIDTH), F32), aug, aug, kv_rows, kv_rows],
        compiler_params=pltpu.CompilerParams(vmem_limit_bytes=VMEM_LIMIT),
    )(z, z)


class _SampleShape(NamedTuple):
    n_seq: int
    dec_seq: int
    n_past: int
    blocks_per_step: int
    steps_per_seq: int


def _attn_kernel(thr_ref, pt_ref,
                 q_ref, kaug_ref, vaug_ref, km_ref, ga_ref, tab_ref, blo_ref, bhi_ref,
                 qs_ref, knew_ref, vnew_ref, gas_ref, *rest, n_blk_pad, sample):
    n_cache_pages = sample.blocks_per_step * PAGES_PER_BLOCK
    k_pages, v_pages = rest[:n_cache_pages], rest[n_cache_pages:2 * n_cache_pages]
    (sbias_ref, sbias_own_ref, o_ref, so_ref, qaug_ref, m_ref, l_ref, acc_ref, dmat_ref,
     qf_ref, qb_ref, gate_ref, mx_ref, sl_ref, part_ref) = rest[2 * n_cache_pages:]
    g = pl.program_id(0)
    i = pl.program_id(1)
    blk = MOBA_BLOCK
    all_rows = KV_GROUP * blk

    def select_blocks():
        dmat_ref[...] = (lax.broadcasted_iota(jnp.int32, (blk, blk), 0)
                         - lax.broadcasted_iota(jnp.int32, (blk, blk), 1))
        m_ref[...] = jnp.full(m_ref.shape, NEG_INF, F32)
        l_ref[...] = jnp.zeros(l_ref.shape, F32)
        acc_ref[...] = jnp.zeros(acc_ref.shape, F32)
        q_all = jnp.concatenate(
            [q_ref[:, hh * HEAD_DIM:(hh + 1) * HEAD_DIM] for hh in range(KV_GROUP)], axis=0)
        gate_t = _dot_nt(km_ref[0:n_blk_pad, :], q_all, precision=lax.Precision.HIGHEST)
        blk_id = lax.broadcasted_iota(jnp.int32, (n_blk_pad, all_rows), 0)
        sel_t = _select_top_blocks(gate_t, blk_id, i, axis=0)
        mask_t = jnp.where(sel_t | (blk_id == i), 0.0, NEG_INF)
        if n_blk_pad < LANES:
            mask_t = jnp.concatenate([mask_t, jnp.zeros((LANES - n_blk_pad, all_rows), F32)], axis=0)
        qaug_ref[:, 0:HEAD_DIM] = (q_all * (ATTN_SCALE * LOG2_E)).astype(BF16)
        qaug_ref[:, HEAD_DIM:HEAD_DIM + LANES] = mask_t.T.astype(BF16)

    step = g * pl.num_programs(1) + i
    part = step % sample.steps_per_seq
    sample_init, sample_past, sample_own = _sample_phases(
        part, sample, qs_ref, knew_ref, vnew_ref, gas_ref, k_pages, v_pages, sbias_ref, sbias_own_ref,
        so_ref, qf_ref, qb_ref, gate_ref, mx_ref, sl_ref, part_ref)
    sample_steps = sample.n_seq * sample.steps_per_seq
    if sample_steps == N_KV_HEADS * (kaug_ref.shape[1] // blk):
        pl.when(part == 0)(sample_init)
        select_blocks()
        sample_past()
        pl.when(part == sample.steps_per_seq - 1)(sample_own)
    else:
        @pl.when(step < sample_steps)
        def _sample_chunk():
            pl.when(part == 0)(sample_init)
            sample_past()
            pl.when(part == sample.steps_per_seq - 1)(sample_own)

        select_blocks()

    def sweep(groups):
        m_all, l_all, acc_all = m_ref[...], l_ref[...], acc_ref[...]
        chunks = [slice(c, c + SWEEP_ROWS) for c in range(0, all_rows, SWEEP_ROWS)]
        m_c = [m_all[rows] for rows in chunks]
        l_c = [l_all[rows] for rows in chunks]
        acc_c = [acc_all[rows] for rows in chunks]
        logits = []
        for n0, nb, bias in groups:
            r = pl.multiple_of(n0 * blk, blk)
            keys = kaug_ref[0, pl.ds(r, nb * blk), :]
            logits.append([_dot_nt(qaug_ref[rows, :], keys) + bias[rows] for rows in chunks])
        for (n0, nb, bias), group_logits in zip(groups, logits):
            r = pl.multiple_of(n0 * blk, blk)
            values = vaug_ref[0, pl.ds(r, nb * blk), :]
            for ci, rows in enumerate(chunks):
                sc = group_logits[ci]
                m_new = jnp.maximum(m_c[ci], jnp.max(sc, axis=-1, keepdims=True))
                alpha = jnp.exp2(m_c[ci] - m_new)
                p = jnp.exp2(sc - jnp.concatenate([m_new] * (nb * blk // LANES), axis=1))
                pv = jnp.dot(p.astype(BF16), values, preferred_element_type=F32)
                l_c[ci] = alpha * l_c[ci] + pv[:, HEAD_DIM:]
                acc_c[ci] = alpha * acc_c[ci] + pv[:, :HEAD_DIM]
                m_c[ci] = m_new
        m_ref[...] = jnp.concatenate(m_c, axis=0)
        l_ref[...] = jnp.concatenate(l_c, axis=0)
        acc_ref[...] = jnp.concatenate(acc_c, axis=0)

    def far_bias(n0, nb):
        tiles = []
        for j in range(nb):
            delta = i - (n0 + j)
            high = dmat_ref[...] >= thr_ref[delta]
            base = delta * N_HEADS + g * KV_GROUP
            tiles.append(jnp.concatenate([jnp.where(high, bhi_ref[base + hh], blo_ref[base + hh])
                                          for hh in range(KV_GROUP)], axis=0))
        return jnp.concatenate(tiles, axis=1)

    n_far = jnp.maximum(i - (TABLE_DELTAS - 1), 0)
    per_iter = FAR_GROUPS * FAR_GROUP_BLOCKS

    def far_step(t, carry):
        first = t * per_iter
        sweep([(first + j * FAR_GROUP_BLOCKS, FAR_GROUP_BLOCKS,
                far_bias(first + j * FAR_GROUP_BLOCKS, FAR_GROUP_BLOCKS)) for j in range(FAR_GROUPS)])
        return carry

    lax.fori_loop(0, n_far // per_iter, far_step, 0)

    left = n_far % per_iter
    near = jnp.minimum(i + 1, TABLE_DELTAS)
    for count in range(1, TABLE_DELTAS + 1):
        for rem in range(per_iter if count == TABLE_DELTAS else 1):
            @pl.when((near == count) & (left == rem))
            def _tail(count=count, rem=rem):
                groups = []
                for first in range(0, rem, FAR_GROUP_BLOCKS):
                    nb = min(FAR_GROUP_BLOCKS, rem - first)
                    groups.append((n_far - rem + first, nb, far_bias(n_far - rem + first, nb)))
                groups.append((i - (count - 1), count, jnp.concatenate(
                    [tab_ref[count - 1 - j].reshape(all_rows, blk) for j in range(count)], axis=1)))
                sweep(groups)

    attn = acc_ref[...] / l_ref[...]
    for hh in range(KV_GROUP):
        cols = slice(hh * HEAD_DIM, (hh + 1) * HEAD_DIM)
        o_ref[:, cols] = (attn[hh * blk:(hh + 1) * blk, :] * _silu(ga_ref[:, cols])).astype(BF16)


def _block_of_pages(page_refs):
    return jnp.concatenate(
        [jnp.concatenate([ref[0, pl.ds(gi, PAGE_SIZE, stride=N_KV_HEADS), :] for gi in range(N_KV_HEADS)],
                         axis=1) for ref in page_refs], axis=0)


def _sample_phases(part, sample, q_ref, knew_ref, vnew_ref, ga_ref, k_pages, v_pages, bias_ref, bias_own_ref,
                   o_ref, qf_ref, qb_ref, gate_ref, mx_ref, l_ref, part_ref):
    dec_seq, n_past, blocks_per_step = sample.dec_seq, sample.n_past, sample.blocks_per_step
    rows = N_HEADS * dec_seq
    grp_rows = KV_GROUP * dec_seq

    def _init():
        gate_ref[...] = jnp.zeros(gate_ref.shape, F32)
        mx_ref[...] = jnp.zeros(mx_ref.shape, F32)
        l_ref[...] = jnp.zeros(l_ref.shape, F32)
        col_grp = lax.broadcasted_iota(jnp.int32, (dec_seq, KV_WIDTH), 1) // HEAD_DIM
        for h in range(N_HEADS):
            qh = q_ref[:, h * HEAD_DIM:(h + 1) * HEAD_DIM]
            wide = jnp.concatenate([qh] * N_KV_HEADS, axis=1)
            wide = jnp.where(col_grp == h // KV_GROUP, wide, 0.0)
            qf_ref[h * dec_seq:(h + 1) * dec_seq, :] = wide
            qb_ref[h * dec_seq:(h + 1) * dec_seq, :] = (wide * ATTN_SCALE).astype(BF16)

    def block_partials(kblks, vblks, biases):
        scs = [_dot_nt(qb_ref[...], kblk.astype(BF16)) + bias for kblk, bias in zip(kblks, biases)]
        ms = [jnp.max(sc, axis=-1, keepdims=True) for sc in scs]
        ps = [jnp.exp(sc - m_n) for sc, m_n in zip(scs, ms)]
        ls = [jnp.sum(p, axis=-1, keepdims=True) for p in ps]
        outs = []
        for p, vblk in zip(ps, vblks):
            o_all = jnp.dot(p.astype(BF16), vblk.astype(BF16), preferred_element_type=F32)
            outs.append(jnp.concatenate(
                [o_all[gi * grp_rows:(gi + 1) * grp_rows, gi * HEAD_DIM:(gi + 1) * HEAD_DIM]
                 for gi in range(N_KV_HEADS)], axis=0))
        return ms, ls, outs

    def _past():
        lane = lax.broadcasted_iota(jnp.int32, (rows, LANES), 1)
        kblks = [_block_of_pages(k_pages[jb * PAGES_PER_BLOCK:(jb + 1) * PAGES_PER_BLOCK])
                 for jb in range(blocks_per_step)]
        vblks = [_block_of_pages(v_pages[jb * PAGES_PER_BLOCK:(jb + 1) * PAGES_PER_BLOCK])
                 for jb in range(blocks_per_step)]
        ms, lsums, outs = block_partials(kblks, vblks, [bias_ref[jb] for jb in range(blocks_per_step)])
        mx, ls, gt = mx_ref[...], l_ref[...], gate_ref[...]
        for jb in range(blocks_per_step):
            n = part * blocks_per_step + jb
            kmean = jnp.sum(kblks[jb], axis=0, keepdims=True) * (1.0 / MOBA_BLOCK)
            gate = jnp.sum(qf_ref[...] * kmean, axis=-1, keepdims=True)
            part_ref[n] = outs[jb]
            here = lane == n
            mx = jnp.where(here, ms[jb], mx)
            ls = jnp.where(here, lsums[jb], ls)
            gt = jnp.where(here, gate, gt)
        mx_ref[...] = mx
        l_ref[...] = ls
        gate_ref[...] = gt

    def _own():
        lane = lax.broadcasted_iota(jnp.int32, (rows, LANES), 1)
        pad = jnp.zeros((MOBA_BLOCK - dec_seq, KV_WIDTH), F32)
        (m_n,), (l_n,), (o_n,) = block_partials([jnp.concatenate([knew_ref[...], pad], axis=0)],
                                                [jnp.concatenate([vnew_ref[...], pad], axis=0)],
                                                [bias_own_ref[0]])
        own = lane == n_past
        mx = jnp.where(own, m_n, mx_ref[...])
        ls = jnp.where(own, l_n, l_ref[...])
        sel = _select_top_blocks(gate_ref[...], lane, n_past)
        include = sel | own
        m_tot = jnp.max(jnp.where(include, mx, GATE_MASKED), axis=-1, keepdims=True)
        e = jnp.where(include, jnp.exp(jnp.minimum(mx - m_tot, 0.0)), 0.0)
        denom = jnp.sum(e * ls, axis=-1, keepdims=True)
        num = e[:, n_past:n_past + 1] * o_n
        for j in range(n_past):
            num = num + e[:, j:j + 1] * part_ref[j]
        attn = num / denom
        for h in range(N_HEADS):
            cols = slice(h * HEAD_DIM, (h + 1) * HEAD_DIM)
            o_ref[0, :, cols] = (attn[h * dec_seq:(h + 1) * dec_seq, :] * _silu(ga_ref[:, cols])).astype(BF16)

    return _init, _past, _own


def _prompt_bias_tables(rel_bias, n_blk):
    blk = MOBA_BLOCK
    r = jnp.arange(blk, dtype=jnp.int32)
    dist = (jnp.arange(TABLE_DELTAS, dtype=jnp.int32)[:, None, None] * blk
            + r[None, :, None] - r[None, None, :])
    tab = _bias_of_bucket(rel_bias, _rel_bucket(dist), 1)
    tab = jnp.where((dist >= 0)[:, None], tab * LOG2_E, NEG_INF)

    nd = max(n_blk, TABLE_DELTAS + 1)
    off = np.arange(-(blk - 1), blk, dtype=np.int32)
    dist_np = np.arange(nd, dtype=np.int32)[:, None] * blk + off[None, :]
    buckets_np = _rel_bucket_np(dist_np)
    span = buckets_np[TABLE_DELTAS:, -1] - buckets_np[TABLE_DELTAS:, 0]
    assert np.all(np.diff(buckets_np[TABLE_DELTAS:], axis=1) >= 0) and np.all(span <= 1), \
        "more than one bias bucket boundary inside a far block pair"
    buckets = _rel_bucket(jnp.asarray(dist_np))
    b_lo, b_hi = buckets[:, 0], buckets[:, -1]
    first_hi = jnp.argmax(buckets == b_hi[:, None], axis=1).astype(jnp.int32)
    thr = first_hi - (blk - 1)
    blo = (rel_bias.astype(F32)[b_lo] * LOG2_E).reshape(-1)
    bhi = (rel_bias.astype(F32)[b_hi] * LOG2_E).reshape(-1)
    return tab, thr, blo, bhi


def _sample_bias_table(rel_bias, past_len, dec_seq, n_blocks):
    kpos = jnp.arange(n_blocks * MOBA_BLOCK, dtype=jnp.int32).reshape(n_blocks, 1, MOBA_BLOCK)
    qpos = (past_len + jnp.arange(dec_seq, dtype=jnp.int32)).reshape(1, dec_seq, 1)
    dist = qpos - kpos
    vals = _bias_of_bucket(rel_bias, _rel_bucket(dist), 1)
    vals = jnp.where((dist >= 0)[:, None], vals, NEG_INF)
    return vals.reshape(n_blocks, N_HEADS * dec_seq, MOBA_BLOCK)


def _attention(z, kmean_pad, kaug, vaug, rel_bias, seq, ga_col0,
               q_s, k_s, v_s, ga_s, cache_k, cache_v, page_table, past_len):
    blk = MOBA_BLOCK
    n_blk = seq // blk
    assert n_blk <= LANES
    tab, thr, blo, bhi = _prompt_bias_tables(rel_bias, n_blk)
    gc = ga_col0 // GROUP_WIDTH

    bsz, n_pages = page_table.shape
    dec_seq = q_s.shape[0] // bsz
    n_past = past_len // blk
    bps = math.gcd(SAMPLE_BLOCKS_PER_STEP, n_past)
    assert past_len % blk == 0 and n_past + 1 <= LANES and n_pages == n_past * PAGES_PER_BLOCK
    sample = _SampleShape(bsz, dec_seq, n_past, bps, n_past // bps)
    sample_steps = sample.n_seq * sample.steps_per_seq
    assert sample_steps <= N_KV_HEADS * n_blk, "sample attention chunks must fit in the prompt grid"
    pages_per_step = bps * PAGES_PER_BLOCK
    sbias = _sample_bias_table(rel_bias, past_len, dec_seq, n_past + 1)
    rows = N_HEADS * dec_seq

    def sample_pos(g, i):
        s = jnp.minimum(g * n_blk + i, sample_steps - 1)
        return s // sample.steps_per_seq, s % sample.steps_per_seq

    def page_spec(which):
        def index_map(g, i, th, pt):
            b, part = sample_pos(g, i)
            return (pt[b * n_pages + part * pages_per_step + which], 0, 0)
        return pl.BlockSpec((1, PAGE_SIZE * N_KV_HEADS, HEAD_DIM), index_map)

    row_spec = lambda width: pl.BlockSpec((dec_seq, width), lambda g, i, th, pt: (sample_pos(g, i)[0], 0))
    resident = dict(pipeline_mode=pl.Buffered(1))
    smem = pl.BlockSpec(memory_space=pltpu.SMEM)
    grid_spec = pltpu.PrefetchScalarGridSpec(
        num_scalar_prefetch=2,
        grid=(N_KV_HEADS, n_blk),
        in_specs=([
            pl.BlockSpec((blk, GROUP_WIDTH), lambda g, i, th, pt: (i, g)),
            pl.BlockSpec((1, seq, HEAD_DIM + LANES), lambda g, i, th, pt: (g, 0, 0), **resident),
            pl.BlockSpec((1, seq, HEAD_DIM + LANES), lambda g, i, th, pt: (g, 0, 0), **resident),
            pl.BlockSpec((LANES, HEAD_DIM), lambda g, i, th, pt: (0, g)),
            pl.BlockSpec((blk, GROUP_WIDTH), lambda g, i, th, pt: (i, gc + g)),
            pl.BlockSpec((TABLE_DELTAS, KV_GROUP, blk, blk), lambda g, i, th, pt: (0, g, 0, 0), **resident),
            smem, smem,
            row_spec(ATTN_WIDTH), row_spec(KV_WIDTH), row_spec(KV_WIDTH), row_spec(ATTN_WIDTH)]
            + [page_spec(j) for j in range(pages_per_step)]
            + [page_spec(j) for j in range(pages_per_step)]
            + [pl.BlockSpec((bps, rows, blk), lambda g, i, th, pt: (sample_pos(g, i)[1], 0, 0)),
               pl.BlockSpec((1, rows, blk), lambda g, i, th, pt: (n_past, 0, 0))]),
        out_specs=[pl.BlockSpec((blk, GROUP_WIDTH), lambda g, i, th, pt: (i, g)),
                   pl.BlockSpec((1, dec_seq, ATTN_WIDTH), lambda g, i, th, pt: (sample_pos(g, i)[0], 0, 0))],
        scratch_shapes=[
            pltpu.VMEM((KV_GROUP * blk, HEAD_DIM + LANES), BF16),
            pltpu.VMEM((KV_GROUP * blk, LANES), F32),
            pltpu.VMEM((KV_GROUP * blk, LANES), F32),
            pltpu.VMEM((KV_GROUP * blk, HEAD_DIM), F32),
            pltpu.VMEM((blk, blk), jnp.int32),
            pltpu.VMEM((rows, KV_WIDTH), F32),
            pltpu.VMEM((rows, KV_WIDTH), BF16),
            pltpu.VMEM((rows, LANES), F32),
            pltpu.VMEM((rows, LANES), F32),
            pltpu.VMEM((rows, LANES), F32),
            pltpu.VMEM((n_past, rows, HEAD_DIM), F32),
        ],
    )
    ya, ya_s = pl.pallas_call(
        functools.partial(_attn_kernel, n_blk_pad=-(-n_blk // 8) * 8, sample=sample),
        grid_spec=grid_spec,
        out_shape=[jax.ShapeDtypeStruct((seq, ATTN_WIDTH), BF16),
                   jax.ShapeDtypeStruct((bsz, dec_seq, ATTN_WIDTH), BF16)],
        compiler_params=pltpu.CompilerParams(
            dimension_semantics=("arbitrary", "arbitrary"), vmem_limit_bytes=VMEM_LIMIT),
    )(thr, page_table.reshape(-1), z, kaug, vaug, kmean_pad, z, tab, blo, bhi, q_s, k_s, v_s, ga_s,
      *([cache_k] * pages_per_step), *([cache_v] * pages_per_step), sbias, sbias)
    return ya, ya_s.reshape(bsz * dec_seq, ATTN_WIDTH)


def _window_sum(ext_ref, tmp_ref, window, rows, cols):
    total = POOL_HALO + rows
    width, first, slot = 1, 0, 0
    read = lambda lo, hi: ext_ref[lo:hi, cols]
    while 2 * width < window:
        lo = first + width
        tmp_ref[slot, lo:total, :] = read(lo, total) + read(lo - width, total - width)
        read = functools.partial(lambda s, lo, hi: tmp_ref[s, lo:hi, :], slot)
        first, width, slot = lo, 2 * width, 1 - slot
    assert POOL_HALO - width >= first
    return read(POOL_HALO, total) + read(POOL_HALO - width, total - width)


def _window_mix(win, count, u, gp, w, scale):
    zed = win / count - u
    y = jnp.dot(zed.astype(BF16), w, preferred_element_type=F32)
    return (y * scale) * _silu(gp)


def _pool_prompt_kernel(prev_ref, u_ref, gp_ref, w_ref, scale_ref, o_ref, ext_ref, tmp_ref, *, groups_per_step):
    i = pl.program_id(0)
    rows = u_ref.shape[0]
    gw = w_ref.shape[1]
    ext_ref[0:POOL_HALO, :] = jnp.where(i == 0, 0.0, prev_ref[...])
    ext_ref[POOL_HALO:POOL_HALO + rows, :] = u_ref[...]
    pos = i * rows + lax.broadcasted_iota(jnp.int32, (rows, 1), 0)
    for part in range(N_POOL_GROUPS // groups_per_step):
        @pl.when(pl.program_id(1) == part)
        def _(part=part):
            for gl in range(groups_per_step):
                window = POOL_WINDOWS[part * groups_per_step + gl]
                cols = slice(gl * gw, (gl + 1) * gw)
                count = jnp.minimum(window, pos + 1).astype(F32)
                y = _window_mix(_window_sum(ext_ref, tmp_ref, window, rows, cols), count, u_ref[:, cols],
                                gp_ref[:, cols], w_ref[gl], scale_ref[:, cols])
                o_ref[:, cols] = y.astype(BF16)


def _pool_prompt(z, w_pool_bf16, pool_scale, seq, u_col0, gp_col0, tm):
    n_groups, gw, _ = w_pool_bf16.shape
    gps = 2
    bw = gps * gw
    assert n_groups == N_POOL_GROUPS and u_col0 % bw == 0 and gp_col0 % bw == 0 and tm % POOL_HALO == 0
    uc, gc = u_col0 // bw, gp_col0 // bw
    halo_blocks = tm // POOL_HALO
    return pl.pallas_call(
        functools.partial(_pool_prompt_kernel, groups_per_step=gps),
        grid=(seq // tm, n_groups // gps),
        in_specs=[pl.BlockSpec((POOL_HALO, bw), lambda i, c: (jnp.maximum(i * halo_blocks - 1, 0), uc + c)),
                  pl.BlockSpec((tm, bw), lambda i, c: (i, uc + c)),
                  pl.BlockSpec((tm, bw), lambda i, c: (i, gc + c)),
                  pl.BlockSpec((gps, gw, gw), lambda i, c: (c, 0, 0)),
                  pl.BlockSpec((1, bw), lambda i, c: (0, c))],
        out_specs=pl.BlockSpec((tm, bw), lambda i, c: (i, c)),
        out_shape=jax.ShapeDtypeStruct((seq, n_groups * gw), BF16),
        scratch_shapes=[pltpu.VMEM((POOL_HALO + tm, bw), F32), pltpu.VMEM((2, POOL_HALO + tm, gw), F32)],
        compiler_params=pltpu.CompilerParams(
            dimension_semantics=("parallel", "arbitrary"), vmem_limit_bytes=VMEM_LIMIT),
    )(z, z, z, w_pool_bf16, pool_scale.reshape(1, n_groups * gw))


def _pool_sample_kernel(ext_in_ref, gp_ref, w_ref, scale_ref, o_ref, ext_ref, *, bsz, dec_seq):
    seg = POOL_HALO + dec_seq
    rows = bsz * seg
    gw = w_ref.shape[1]
    ext_ref[0:POOL_HALO, :] = jnp.zeros((POOL_HALO, ext_ref.shape[1]), F32)
    ext_ref[POOL_HALO:POOL_HALO + rows, :] = ext_in_ref[...]
    for gi, window in enumerate(POOL_WINDOWS):
        cols = slice(gi * gw, (gi + 1) * gw)
        u_all = ext_in_ref[:, cols]
        win = u_all
        for j in range(1, window):
            win = win + ext_ref[POOL_HALO - j:POOL_HALO - j + rows, cols]
        zed = (win / float(window) - u_all).reshape(bsz, seg, gw)[:, POOL_HALO:, :].reshape(bsz * dec_seq, gw)
        y = jnp.dot(zed.astype(BF16), w_ref[gi], preferred_element_type=F32)
        o_ref[:, cols] = ((y * scale_ref[:, cols]) * _silu(gp_ref[:, cols])).astype(BF16)


def _pool_sample(u_ext, gp_s, w_pool_bf16, pool_scale, bsz, dec_seq):
    pw = u_ext.shape[1]
    rows = u_ext.shape[0]
    vm = pl.BlockSpec(memory_space=pltpu.VMEM)
    return pl.pallas_call(
        functools.partial(_pool_sample_kernel, bsz=bsz, dec_seq=dec_seq),
        in_specs=[vm, vm, vm, vm],
        out_specs=vm,
        out_shape=jax.ShapeDtypeStruct((bsz * dec_seq, pw), BF16),
        scratch_shapes=[pltpu.VMEM((POOL_HALO + rows, pw), F32)],
        compiler_params=pltpu.CompilerParams(vmem_limit_bytes=VMEM_LIMIT),
    )(u_ext, gp_s, w_pool_bf16, pool_scale.reshape(1, pw))


def _outproj_kernel(ya_ref, yp_ref, w_ref, x_ref, gain_ref, o_ref, *, k_attn):
    k = pl.program_id(1)

    @pl.when(k == 0)
    def _():
        o_ref[...] = x_ref[...] + jnp.dot(ya_ref[...], w_ref[...], preferred_element_type=F32)

    @pl.when((k > 0) & (k < k_attn))
    def _():
        o_ref[...] += jnp.dot(ya_ref[...], w_ref[...], preferred_element_type=F32)

    @pl.when(k >= k_attn)
    def _():
        o_ref[...] += jnp.dot(yp_ref[...], w_ref[...], preferred_element_type=F32)

    @pl.when(k == pl.num_programs(1) - 1)
    def _():
        h = o_ref[...]
        ms = jnp.mean(h * h, axis=-1, keepdims=True)
        o_ref[...] = h * lax.rsqrt(ms + RMS_EPS) * gain_ref[...]


def _out_projection(ya, yp, w_bf16, x, gain, tm, tk):
    m, d = x.shape
    ka, kp = ya.shape[1] // tk, yp.shape[1] // tk
    return pl.pallas_call(
        functools.partial(_outproj_kernel, k_attn=ka),
        grid=(m // tm, ka + kp),
        in_specs=[pl.BlockSpec((tm, tk), lambda i, k: (i, jnp.minimum(k, ka - 1))),
                  pl.BlockSpec((tm, tk), lambda i, k: (i, jnp.maximum(k - ka, 0))),
                  pl.BlockSpec((tk, d), lambda i, k: (k, 0)),
                  pl.BlockSpec((tm, d), lambda i, k: (i, 0)),
                  pl.BlockSpec((1, d), lambda i, k: (0, 0))],
        out_specs=pl.BlockSpec((tm, d), lambda i, k: (i, 0)),
        out_shape=jax.ShapeDtypeStruct((m, d), F32),
        compiler_params=pltpu.CompilerParams(
            dimension_semantics=("parallel", "arbitrary"), vmem_limit_bytes=VMEM_LIMIT),
    )(ya, yp, w_bf16, x, gain.reshape(1, d))


def kernel(x_prompt, x_sample, cache_k, cache_v, state_pool, page_table, norm_in, w_in, w_pool,
           pool_scale, w_out, rel_bias, norm_out):
    depth = norm_in.shape[0]
    assert depth == 1, "single layer step"
    n_batch, seq, d_model = x_prompt.shape
    bsz, dec_seq, _ = x_sample.shape
    assert n_batch == 1 and seq % (8 * MOBA_BLOCK) == 0 and dec_seq == 8
    past_len = page_table.shape[1] * PAGE_SIZE
    assert past_len >= POOL_HIST
    pool_width = pool_scale.shape[1]
    k0 = ATTN_WIDTH
    v0 = k0 + KV_WIDTH
    ga0 = v0 + KV_WIDTH
    u0 = ga0 + ATTN_WIDTH
    gp0 = u0 + pool_width
    assert w_in.shape[2] == gp0 + pool_width and w_out.shape[1] == ATTN_WIDTH + pool_width

    w_in_b = w_in[0].astype(BF16)
    w_out_b = w_out[0].astype(BF16)
    w_pool_b = w_pool[0].astype(BF16)
    xp = x_prompt[0]
    xs = x_sample.reshape(bsz * dec_seq, d_model)

    z = _in_projection(xp, norm_in[0], w_in_b, tm=512, tn=1536)
    zs = _in_projection(xs, norm_in[0], w_in_b, tm=bsz * dec_seq, tn=1024)
    q_s, k_s, v_s = zs[:, :k0], zs[:, k0:v0], zs[:, v0:ga0]
    ga_s, u_s, gp_s = zs[:, ga0:u0], zs[:, u0:gp0], zs[:, gp0:]

    n_blk = seq // MOBA_BLOCK
    kmean, kaug, vaug, k_rows, v_rows = _stage_kv(z, seq, k0, v0)
    kmean_pad = jnp.pad(kmean, ((0, LANES - n_blk), (0, 0)))
    n_phys = cache_k.shape[1]
    page_rows = PAGE_SIZE * N_KV_HEADS
    ya, ya_s = _attention(z, kmean_pad, kaug, vaug, rel_bias, seq, ga0, q_s, k_s, v_s, ga_s,
                          cache_k.reshape(n_phys, page_rows, HEAD_DIM),
                          cache_v.reshape(n_phys, page_rows, HEAD_DIM), page_table, past_len)
    yp = _pool_prompt(z, w_pool_b, pool_scale[0], seq, u0, gp0, tm=512)
    y_prompt = _out_projection(ya, yp, w_out_b, xp, norm_out, tm=512, tk=512)

    u_s3 = u_s.reshape(bsz, dec_seq, pool_width)
    u_ext = jnp.concatenate([jnp.zeros((bsz, 1, pool_width), F32), state_pool[0].astype(F32), u_s3], axis=1)
    yp_s = _pool_sample(u_ext.reshape(bsz * (POOL_HALO + dec_seq), pool_width), gp_s, w_pool_b,
                        pool_scale[0], bsz, dec_seq)
    y_sample = _out_projection(ya_s, yp_s, w_out_b, xs, norm_out, tm=bsz * dec_seq, tk=512)

    k_prompt = k_rows.reshape(1, 1, seq, N_KV_HEADS, HEAD_DIM)
    v_prompt = v_rows.reshape(1, 1, seq, N_KV_HEADS, HEAD_DIM)
    pool_prompt = z[seq - POOL_HIST:, u0:gp0].reshape(1, 1, POOL_HIST, pool_width)
    k_sample = k_s.reshape(1, bsz, dec_seq, N_KV_HEADS, HEAD_DIM)
    v_sample = v_s.reshape(1, bsz, dec_seq, N_KV_HEADS, HEAD_DIM)
    pool_sample = u_ext[:, 1 + dec_seq:, :].reshape(1, bsz, POOL_HIST, pool_width).astype(state_pool.dtype)
    return (y_prompt.reshape(1, seq, d_model), y_sample.reshape(bsz, dec_seq, d_model),
            k_prompt, v_prompt, pool_prompt, k_sample, v_sample, pool_sample)
```
